```python
import jax, jax.numpy as jnp
from jax import lax
import numpy as np

D_MODEL = 1024
BATCH = 16
SEQ = 4096
DEPTH = 1
DEC_BATCH = 1
DEC_SEQ = 16384
PAST_LEN = 128

N_META = 16
GRID_W = 64
EPS = 1e-6
NEG = -1e30
N_Q_HEADS = 8
N_KV_HEADS = 2
Q_PER_KV = N_Q_HEADS // N_KV_HEADS
HEAD_DIM = 64
ATTN_WIDTH = N_Q_HEADS * HEAD_DIM
KV_WIDTH = N_KV_HEADS * HEAD_DIM
ROPE_AXIS_DIM = HEAD_DIM // 2
ROPE_FREQS = ROPE_AXIS_DIM // 2
ROPE_THETA = 10000.0
Q_BLOCK = 128
M_HEADS = 4
M_HEAD_DIM = 128
M_WIDTH = M_HEADS * M_HEAD_DIM
CHUNK = 64
META_PAD = CHUNK - N_META
D_FF = 2816
IN_SPLITS = (ATTN_WIDTH, KV_WIDTH, KV_WIDTH, M_WIDTH, M_WIDTH, M_WIDTH, M_WIDTH, 2 * M_HEADS, 2 * M_HEADS, D_MODEL, D_MODEL)
IN_WIDTH = sum(IN_SPLITS)
IN_OFFSETS = tuple(int(o) for o in np.cumsum(IN_SPLITS)[:-1])

kernel_name = 'hybrid_gqa_mlstm_macaron_encoder'


def rmsnorm(x, g):
    xf = x.astype(jnp.float32)
    y = xf * lax.rsqrt(jnp.mean(xf * xf, axis=-1, keepdims=True) + EPS)
    return (y * g.astype(jnp.float32)).astype(x.dtype)


def swiglu(x, w1, w3, w2):
    return (jax.nn.silu(x @ w1) * (x @ w3)) @ w2


def axial_rope_tables(n_tok):
    rows = n_tok // GRID_W
    row = jnp.repeat(jnp.arange(rows, dtype=jnp.float32), GRID_W)
    col = jnp.tile(jnp.arange(GRID_W, dtype=jnp.float32), rows)
    pos = jnp.concatenate([jnp.zeros((N_META, 2), jnp.float32), jnp.stack([row, col], axis=-1)], axis=0)
    inv_freq = ROPE_THETA ** (-2.0 * jnp.arange(ROPE_FREQS, dtype=jnp.float32) / ROPE_AXIS_DIM)
    ang = pos[:, :, None] * inv_freq
    return jnp.cos(ang), jnp.sin(ang)


def apply_axial_rope(x, cos, sin):
    xr = x.astype(jnp.float32).reshape(*x.shape[:-1], 2, 2, ROPE_FREQS)
    x1, x2 = xr[..., 0, :], xr[..., 1, :]
    c, s = cos[None, :, None], sin[None, :, None]
    out = jnp.stack([x1 * c - x2 * s, x2 * c + x1 * s], axis=-2)
    return out.reshape(x.shape).astype(x.dtype)


def head_rms(x, g):
    xf = x.astype(jnp.float32)
    y = xf * lax.rsqrt(jnp.mean(xf * xf, axis=-1, keepdims=True) + EPS)
    return (y * g.astype(jnp.float32)).astype(x.dtype)


def gqa_attention(q, k, v):
    B, L = q.shape[0], q.shape[1]
    n_tok = L - N_META
    scale = HEAD_DIM ** -0.5
    qg = q.reshape(B, L, N_KV_HEADS, Q_PER_KV, HEAD_DIM)
    vf = v.astype(jnp.float32)

    def attend(qb):
        s = jnp.einsum('bqhgd,bkhd->bhgqk', qb, k, preferred_element_type=jnp.float32) * scale
        p = jax.nn.softmax(s, axis=-1)
        o = jnp.einsum('bhgqk,bkhd->bqhgd', p, vf)
        return o.astype(q.dtype)

    o_meta = attend(qg[:, :N_META])
    nb = n_tok // Q_BLOCK
    q_blocks = jnp.moveaxis(qg[:, N_META:].reshape(B, nb, Q_BLOCK, N_KV_HEADS, Q_PER_KV, HEAD_DIM), 1, 0)
    o_blocks = lax.map(attend, q_blocks)
    o_tok = jnp.moveaxis(o_blocks, 0, 1).reshape(B, n_tok, N_KV_HEADS, Q_PER_KV, HEAD_DIM)
    return jnp.concatenate([o_meta, o_tok], axis=1).reshape(B, L, ATTN_WIDTH)


def mlstm_direction(q, k, v, log_i, log_f):
    B, H = q.shape[0], q.shape[1]
    b = jnp.cumsum(log_f, axis=-1)
    b_last = b[..., -1]
    w_end = b_last[..., None] - b + log_i
    m_loc = jnp.max(w_end, axis=-1)
    e_end = jnp.exp(w_end - m_loc[..., None])
    C_loc = jnp.einsum('bhnc,bhncd,bhnce->bhnde', e_end, v, k)
    n_loc = jnp.einsum('bhnc,bhnce->bhne', e_end, k)

    def step(carry, inp):
        C_prev, n_prev, m_prev = carry
        C_l, n_l, m_l, bl = inp
        m_new = jnp.maximum(bl + m_prev, m_l)
        a = jnp.exp(bl + m_prev - m_new)
        c = jnp.exp(m_l - m_new)
        C_new = a[..., None, None] * C_prev + c[..., None, None] * C_l
        n_new = a[..., None] * n_prev + c[..., None] * n_l
        return (C_new, n_new, m_new), (C_prev, n_prev, m_prev)

    init = (jnp.zeros((B, H, M_HEAD_DIM, M_HEAD_DIM), jnp.float32),
            jnp.zeros((B, H, M_HEAD_DIM), jnp.float32),
            jnp.zeros((B, H), jnp.float32))
    xs = (jnp.moveaxis(C_loc, 2, 0), jnp.moveaxis(n_loc, 2, 0), jnp.moveaxis(m_loc, 2, 0), jnp.moveaxis(b_last, 2, 0))
    _, (C_in, n_in, m_in) = lax.scan(step, init, xs)
    C_in = jnp.moveaxis(C_in, 0, 2)
    n_in = jnp.moveaxis(n_in, 0, 2)
    m_in = jnp.moveaxis(m_in, 0, 2)

    mask = jnp.tril(jnp.ones((CHUNK, CHUNK), dtype=bool))
    D = jnp.where(mask, b[..., :, None] - b[..., None, :] + log_i[..., None, :], -jnp.inf)
    inter = b + m_in[..., None]
    m_t = jnp.maximum(inter, jnp.max(D, axis=-1))
    S = jnp.einsum('bhnte,bhnse->bhnts', q, k) * jnp.exp(D - m_t[..., None])
    w_inter = jnp.exp(inter - m_t)
    num = jnp.einsum('bhnts,bhnsd->bhntd', S, v) + w_inter[..., None] * jnp.einsum('bhnde,bhnte->bhntd', C_in, q)
    den = jnp.sum(S, axis=-1) + w_inter * jnp.einsum('bhne,bhnte->bhnt', n_in, q)
    den = jnp.maximum(jnp.abs(den), jnp.exp(-m_t))
    return num / den[..., None]


def bidir_mlstm(mq, mk, mv, gi, gf, b_i, b_f):
    B, L = mq.shape[0], mq.shape[1]
    Lp = L + META_PAD
    nc = Lp // CHUNK

    def heads(t):
        t = t.astype(jnp.float32).reshape(B, L, M_HEADS, M_HEAD_DIM).transpose(0, 2, 1, 3)
        return jnp.pad(t, ((0, 0), (0, 0), (META_PAD, 0), (0, 0)))

    def gates(t):
        return t.reshape(B, L, 2, M_HEADS).transpose(2, 0, 3, 1)

    q = heads(mq)
    k = heads(mk) * (M_HEAD_DIM ** -0.5)
    v = heads(mv)
    log_i = jnp.pad(gates(gi.astype(jnp.float32) + b_i), ((0, 0), (0, 0), (0, 0), (META_PAD, 0)), constant_values=NEG)
    log_f = jnp.pad(gates(jax.nn.log_sigmoid(gf.astype(jnp.float32) + b_f)), ((0, 0), (0, 0), (0, 0), (META_PAD, 0)))

    def run(qd, kd, vd, li, lf):
        ch = lambda t: t.reshape(B, M_HEADS, nc, CHUNK, M_HEAD_DIM)
        cg = lambda t: t.reshape(B, M_HEADS, nc, CHUNK)
        return mlstm_direction(ch(qd), ch(kd), ch(vd), cg(li), cg(lf)).reshape(B, M_HEADS, Lp, M_HEAD_DIM)

    h_fwd = run(q, k, v, log_i[0], log_f[0])
    fl = lambda t: jnp.flip(t, axis=2)
    h_bwd = fl(run(fl(q), fl(k), fl(v), jnp.flip(log_i[1], axis=-1), jnp.flip(log_f[1], axis=-1)))
    return (h_fwd + h_bwd)[:, :, META_PAD:].transpose(0, 2, 1, 3)


def hybrid_mixer(u, w_in, b_i, b_f, q_gain, k_gain, m_gain, w_attn_br, w_mlstm_br, w_out, cos, sin):
    B, L, _ = u.shape
    z = u @ w_in
    aq, ak, av, mq, mk, mv, mo, gi, gf, ga, gm = jnp.split(z, IN_OFFSETS, axis=-1)
    aq = apply_axial_rope(head_rms(aq.reshape(B, L, N_Q_HEADS, HEAD_DIM), q_gain), cos, sin)
    ak = apply_axial_rope(head_rms(ak.reshape(B, L, N_KV_HEADS, HEAD_DIM), k_gain), cos, sin)
    av = av.reshape(B, L, N_KV_HEADS, HEAD_DIM)
    a_out = gqa_attention(aq, ak, av) @ w_attn_br
    h = bidir_mlstm(mq, mk, mv, gi, gf, b_i, b_f)
    o_gate = jax.nn.sigmoid(mo.astype(jnp.float32)).reshape(B, L, M_HEADS, M_HEAD_DIM)
    m_out = (head_rms(h, m_gain) * o_gate).reshape(B, L, M_WIDTH).astype(u.dtype) @ w_mlstm_br
    merged = jax.nn.sigmoid(ga) * a_out + jax.nn.sigmoid(gm) * m_out
    return merged @ w_out


def encoder_trunk(x, meta, g_ffn1, w1_ffn1, w3_ffn1, w2_ffn1, g_mix, w_in, b_i, b_f, q_gain, k_gain, m_gain,
                  w_attn_br, w_mlstm_br, w_out, g_ffn2, w1_ffn2, w3_ffn2, w2_ffn2):
    B, n_tok, _ = x.shape
    h = jnp.concatenate([jnp.broadcast_to(meta.astype(x.dtype)[None], (B, N_META, D_MODEL)), x], axis=1)
    cos, sin = axial_rope_tables(n_tok)
    for l in range(DEPTH):
        h = h + 0.5 * swiglu(rmsnorm(h, g_ffn1[l]), w1_ffn1[l], w3_ffn1[l], w2_ffn1[l])
        h = h + hybrid_mixer(rmsnorm(h, g_mix[l]), w_in[l], b_i[l], b_f[l], q_gain[l], k_gain[l], m_gain[l],
                             w_attn_br[l], w_mlstm_br[l], w_out[l], cos, sin)
        h = h + 0.5 * swiglu(rmsnorm(h, g_ffn2[l]), w1_ffn2[l], w3_ffn2[l], w2_ffn2[l])
    return h[:, N_META:]


def setup_inputs(seed: int = 0) -> dict:
    key = jax.random.key(seed)
    ks = jax.random.split(key, 24)
    f32 = jnp.float32

    def nrm(k, shape, scale):
        return jax.random.normal(k, shape, f32) * scale

    def gain(k, shape):
        return 1.0 + 0.05 * jax.random.normal(k, shape, f32)

    b_f = jnp.linspace(3.0, 6.0, 2 * M_HEADS, dtype=f32)[None, :] + nrm(ks[10], (DEPTH, 2 * M_HEADS), 0.1)
    return {
        'x_prompt': nrm(ks[0], (BATCH, SEQ, D_MODEL), 1.0),
        'x_sample': nrm(ks[1], (DEC_BATCH, DEC_SEQ, D_MODEL), 1.0),
        'meta': nrm(ks[2], (N_META, D_MODEL), 1.0),
        'g_ffn1': gain(ks[3], (DEPTH, D_MODEL)),
        'w1_ffn1': nrm(ks[4], (DEPTH, D_MODEL, D_FF), D_MODEL ** -0.5),
        'w3_ffn1': nrm(ks[5], (DEPTH, D_MODEL, D_FF), D_MODEL ** -0.5),
        'w2_ffn1': nrm(ks[6], (DEPTH, D_FF, D_MODEL), D_FF ** -0.5),
        'g_mix': gain(ks[7], (DEPTH, D_MODEL)),
        'w_in': nrm(ks[8], (DEPTH, D_MODEL, IN_WIDTH), D_MODEL ** -0.5),
        'b_i': nrm(ks[9], (DEPTH, 2 * M_HEADS), 0.1),
        'b_f': b_f,
        'q_gain': gain(ks[11], (DEPTH, HEAD_DIM)),
        'k_gain': gain(ks[12], (DEPTH, HEAD_DIM)),
        'm_gain': gain(ks[13], (DEPTH, M_HEADS, M_HEAD_DIM)),
        'w_attn_br': nrm(ks[14], (DEPTH, ATTN_WIDTH, D_MODEL), ATTN_WIDTH ** -0.5),
        'w_mlstm_br': nrm(ks[15], (DEPTH, M_WIDTH, D_MODEL), M_WIDTH ** -0.5),
        'w_out': nrm(ks[16], (DEPTH, D_MODEL, D_MODEL), D_MODEL ** -0.5),
        'g_ffn2': gain(ks[17], (DEPTH, D_MODEL)),
        'w1_ffn2': nrm(ks[18], (DEPTH, D_MODEL, D_FF), D_MODEL ** -0.5),
        'w3_ffn2': nrm(ks[19], (DEPTH, D_MODEL, D_FF), D_MODEL ** -0.5),
        'w2_ffn2': nrm(ks[20], (DEPTH, D_FF, D_MODEL), D_FF ** -0.5),
    }


def reference(x_prompt, x_sample, meta, g_ffn1, w1_ffn1, w3_ffn1, w2_ffn1, g_mix, w_in, b_i, b_f, q_gain, k_gain,
              m_gain, w_attn_br, w_mlstm_br, w_out, g_ffn2, w1_ffn2, w3_ffn2, w2_ffn2):
    y_prompt = encoder_trunk(x_prompt, meta, g_ffn1, w1_ffn1, w3_ffn1, w2_ffn1, g_mix, w_in, b_i, b_f, q_gain, k_gain,
                             m_gain, w_attn_br, w_mlstm_br, w_out, g_ffn2, w1_ffn2, w3_ffn2, w2_ffn2)
    y_sample = encoder_trunk(x_sample, meta, g_ffn1, w1_ffn1, w3_ffn1, w2_ffn1, g_mix, w_in, b_i, b_f, q_gain, k_gain,
                             m_gain, w_attn_br, w_mlstm_br, w_out, g_ffn2, w1_ffn2, w3_ffn2, w2_ffn2)
    return (y_prompt, y_sample)
```

```python
import functools

import jax
import jax.numpy as jnp
import numpy as np
from jax import lax
from jax.experimental import pallas as pl
from jax.experimental.pallas import tpu as pltpu

F32 = jnp.float32
BF16 = jnp.bfloat16

N_META = 16
GRID_W = 64
EPS = 1e-6
NEG = -1e30
N_Q_HEADS = 8
N_KV_HEADS = 2
Q_PER_KV = N_Q_HEADS // N_KV_HEADS
HEAD_DIM = 64
ROPE_AXIS_DIM = HEAD_DIM // 2
ROPE_FREQS = ROPE_AXIS_DIM // 2
ROPE_THETA = 10000.0
M_HEADS = 4
M_HEAD_DIM = 128
M_WIDTH = M_HEADS * M_HEAD_DIM

LANES = 128
VMEM_LIMIT_BYTES = 60 * 1024 * 1024

FRONT_ROWS = 256
ATTN_Q_ROWS = 512
ATTN_K_ROWS = 1024
MLSTM_CHUNK = 256

Q_PAD = N_Q_HEADS * LANES
KV_PAD = N_KV_HEADS * LANES
STATE_COLS = 2 * M_HEAD_DIM


def _const_spec(shape):
    zeros = (0,) * len(shape)
    return pl.BlockSpec(shape, lambda *_: zeros, pipeline_mode=pl.Buffered(1))


def _rmsnorm(x, g):
    return x * lax.rsqrt(jnp.mean(x * x, axis=-1, keepdims=True) + EPS) * g


def _swiglu_half_residual(x, g, w1_ref, w3_ref, w2_ref):
    u = _rmsnorm(x, g).astype(BF16)
    a = jnp.dot(u, w1_ref[...], preferred_element_type=F32)
    b = jnp.dot(u, w3_ref[...], preferred_element_type=F32)
    act = (a * jax.nn.sigmoid(a) * b).astype(BF16)
    return x + 0.5 * jnp.dot(act, w2_ref[...], preferred_element_type=F32)


def _log_sigmoid(x):
    return jnp.minimum(x, 0.0) - jnp.log1p(jnp.exp(-jnp.abs(x)))


def _head_norm_rope(x, gain, cos, sin_signed, first_half):
    ms = jnp.sum(x * x, axis=-1, keepdims=True) * (1.0 / HEAD_DIM)
    y = x * lax.rsqrt(ms + EPS) * gain
    partner = jnp.where(first_half, pltpu.roll(y, LANES - ROPE_FREQS, 1), pltpu.roll(y, ROPE_FREQS, 1))
    return y * cos + partner * sin_signed


def _front_kernel(x_ref, cos_ref, sin_ref, g1_ref, w1_ref, w3_ref, w2_ref, gmix_ref, wq_ref, wk_ref, wv_ref,
                  wm_ref, wg_ref, wmerge_ref, qgain_ref, kgain_ref, gbias_ref,
                  h_ref, q_ref, k_ref, v_ref, m_ref, gate_ref, merge_ref):
    h = _swiglu_half_residual(x_ref[...], g1_ref[...], w1_ref, w3_ref, w2_ref)
    h_ref[...] = h
    u = _rmsnorm(h, gmix_ref[...]).astype(BF16)

    rows = h.shape[0]
    lane = lax.broadcasted_iota(jnp.int32, (rows, LANES), 1)
    first_half = (lane % ROPE_AXIS_DIM) < ROPE_FREQS
    cos = cos_ref[...]
    sin = sin_ref[...]

    zq = jnp.dot(u, wq_ref[...], preferred_element_type=F32)
    for hd in range(N_Q_HEADS):
        sl = slice(hd * LANES, (hd + 1) * LANES)
        y = _head_norm_rope(zq[:, sl], qgain_ref[...], cos, sin, first_half)
        q_ref[:, sl] = (y * (HEAD_DIM ** -0.5)).astype(BF16)

    zk = jnp.dot(u, wk_ref[...], preferred_element_type=F32)
    zv = jnp.dot(u, wv_ref[...], preferred_element_type=F32)
    for hd in range(N_KV_HEADS):
        sl = slice(hd * LANES, (hd + 1) * LANES)
        k_ref[:, sl] = _head_norm_rope(zk[:, sl], kgain_ref[...], cos, sin, first_half).astype(BF16)
        v_ref[:, sl] = jnp.where(lane == HEAD_DIM, 1.0, zv[:, sl]).astype(BF16)

    zm = jnp.dot(u, wm_ref[...], preferred_element_type=F32)
    m_ref[:, :M_WIDTH] = zm[:, :M_WIDTH].astype(BF16)
    m_ref[:, M_WIDTH:2 * M_WIDTH] = (zm[:, M_WIDTH:2 * M_WIDTH] * (M_HEAD_DIM ** -0.5)).astype(BF16)
    m_ref[:, 2 * M_WIDTH:] = zm[:, 2 * M_WIDTH:].astype(BF16)

    zg = jnp.dot(u, wg_ref[...], preferred_element_type=F32) + gbias_ref[...]
    gate_ref[...] = jnp.where(lane < 2 * M_HEADS, zg, _log_sigmoid(zg))

    zmerge = jnp.dot(u, wmerge_ref[...], preferred_element_type=F32)
    merge_ref[...] = jax.nn.sigmoid(zmerge).astype(BF16)


def _front_call(x, cos, sin, wts, rows):
    tokens, d_model = x.shape
    steps = tokens // rows
    table_steps = cos.shape[0] // rows

    def row_spec(width):
        return pl.BlockSpec((rows, width), lambda i: (i, 0))

    table_spec = pl.BlockSpec((rows, LANES), lambda i: (i % table_steps, 0))
    consts = (wts["g1"], wts["w1a"], wts["w3a"], wts["w2a"], wts["gmix"], wts["wq"], wts["wk"], wts["wv"],
              wts["wm"], wts["wg"], wts["wmerge"], wts["qgain"], wts["kgain"], wts["gbias"])
    out_widths = (d_model, Q_PAD, KV_PAD, KV_PAD, 4 * M_WIDTH, LANES, 2 * d_model)
    out_dtypes = (F32, BF16, BF16, BF16, BF16, F32, BF16)
    return pl.pallas_call(
        _front_kernel,
        grid=(steps,),
        in_specs=[row_spec(d_model), table_spec, table_spec] + [_const_spec(c.shape) for c in consts],
        out_specs=[row_spec(w) for w in out_widths],
        out_shape=[jax.ShapeDtypeStruct((tokens, w), dt) for w, dt in zip(out_widths, out_dtypes)],
        compiler_params=pltpu.CompilerParams(dimension_semantics=("parallel",),
                                             vmem_limit_bytes=VMEM_LIMIT_BYTES),
        name="front",
    )(x, cos, sin, *consts)


def _attn_kernel(q_ref, k_ref, v_ref, kmeta_ref, vmeta_ref, o_ref, m_scr, acc_scr):
    ki = pl.program_id(2)
    nt_dims = (((1,), (1,)), ((), ()))
    q_rows = q_ref.shape[0]

    def head_slices(hd):
        g = hd // Q_PER_KV
        return slice(hd * LANES, (hd + 1) * LANES), slice(g * LANES, (g + 1) * LANES)

    @pl.when(ki == 0)
    def _seed_from_meta_keys():
        for hd in range(N_Q_HEADS):
            qs, gs = head_slices(hd)
            s = lax.dot_general(q_ref[:, qs], kmeta_ref[:, gs], nt_dims, preferred_element_type=F32)
            m = jnp.max(s, axis=-1, keepdims=True)
            p = jnp.exp(s - m)
            acc_scr[hd] = jnp.dot(p.astype(BF16), vmeta_ref[:, gs], preferred_element_type=F32)
            m_scr[hd] = jnp.broadcast_to(m, (q_rows, LANES))

    for hd in range(N_Q_HEADS):
        qs, gs = head_slices(hd)
        s = lax.dot_general(q_ref[:, qs], k_ref[:, gs], nt_dims, preferred_element_type=F32)
        m_old = m_scr[hd]
        m_new = jnp.maximum(m_old, jnp.max(s, axis=-1, keepdims=True))
        p = jnp.exp(s - m_new[:, :1])
        pv = jnp.dot(p.astype(BF16), v_ref[:, gs], preferred_element_type=F32)
        acc_scr[hd] = jnp.exp(m_old - m_new) * acc_scr[hd] + pv
        m_scr[hd] = m_new

    @pl.when(ki == pl.num_programs(2) - 1)
    def _normalise():
        for hd in range(N_Q_HEADS):
            qs, _ = head_slices(hd)
            acc = acc_scr[hd]
            o_ref[:, qs] = (acc / acc[:, HEAD_DIM:HEAD_DIM + 1]).astype(BF16)


def _attn_call(q, k, v, k_meta, v_meta, batch, q_rows, k_rows):
    tokens = q.shape[0]
    n_tok = tokens // batch
    nq = n_tok // q_rows
    nk = n_tok // k_rows
    return pl.pallas_call(
        _attn_kernel,
        grid=(batch, nq, nk),
        in_specs=[
            pl.BlockSpec((q_rows, Q_PAD), lambda b, i, j: (b * nq + i, 0)),
            pl.BlockSpec((k_rows, KV_PAD), lambda b, i, j: (b * nk + j, 0)),
            pl.BlockSpec((k_rows, KV_PAD), lambda b, i, j: (b * nk + j, 0)),
            pl.BlockSpec((N_META, KV_PAD), lambda b, i, j: (0, 0)),
            pl.BlockSpec((N_META, KV_PAD), lambda b, i, j: (0, 0)),
        ],
        out_specs=pl.BlockSpec((q_rows, Q_PAD), lambda b, i, j: (b * nq + i, 0)),
        out_shape=jax.ShapeDtypeStruct((tokens, Q_PAD), BF16),
        scratch_shapes=[pltpu.VMEM((N_Q_HEADS, q_rows, LANES), F32),
                        pltpu.VMEM((N_Q_HEADS, q_rows, LANES), F32)],
        compiler_params=pltpu.CompilerParams(dimension_semantics=("parallel", "parallel", "arbitrary"),
                                             vmem_limit_bytes=VMEM_LIMIT_BYTES),
        name="attn",
    )(q, k, v, k_meta, v_meta)


def _split_hi_lo(x):
    hi = x.astype(BF16)
    lo = (x - hi.astype(F32)).astype(BF16)
    return hi, lo


def _gate_cumsum(gates, causal):
    t = gates.shape[0]
    row = lax.broadcasted_iota(jnp.int32, (t, t), 0)
    col = lax.broadcasted_iota(jnp.int32, (t, t), 1)
    mask = (col <= row) if causal else (col >= row)
    tri = jnp.where(mask, 1.0, 0.0).astype(BF16)
    hi, lo = _split_hi_lo(gates)
    sums = jnp.dot(tri, hi, preferred_element_type=F32) + jnp.dot(tri, lo, preferred_element_type=F32)
    return sums, mask


def _value_with_ones(v):
    lane = lax.broadcasted_iota(jnp.int32, v.shape, 1)
    return jnp.concatenate([v, jnp.where(lane == 0, 1.0, 0.0).astype(v.dtype)], axis=1)


def _state_update(state, m_in, k, vaug, li_col, b_col, b_tot):
    w_end = b_tot - b_col + li_col
    m_new = jnp.maximum(b_tot + m_in, jnp.max(w_end, axis=0, keepdims=True))
    kw = (k.astype(F32) * jnp.exp(w_end - m_new)).astype(BF16)
    local = lax.dot_general(kw, vaug, (((0,), (0,)), ((), ())), preferred_element_type=F32)
    return jnp.exp(b_tot + m_in - m_new) * state + local, m_new


def _chain_chunk(q, k, v, gates, gates_t, sums, sums_t, mask, chain, causal, state, m_in):
    t = q.shape[0]
    i_lane = chain
    f_lane = 2 * M_HEADS + chain
    li_col = gates[:, i_lane:i_lane + 1]
    li_row = gates_t[i_lane:i_lane + 1, :]
    b_col = sums[:, f_lane:f_lane + 1]
    b_row = sums_t[f_lane:f_lane + 1, :]
    last = t - 1 if causal else 0
    b_tot = sums[last:last + 1, f_lane:f_lane + 1]

    vaug = _value_with_ones(v)
    d = jnp.where(mask, b_col - b_row + li_row, NEG)
    inter = b_col + m_in
    m_t = jnp.maximum(inter, jnp.max(d, axis=-1, keepdims=True))
    qk = lax.dot_general(q, k, (((1,), (1,)), ((), ())), preferred_element_type=F32)
    s = (qk * jnp.exp(d - m_t)).astype(BF16)
    tot = (jnp.dot(s, vaug, preferred_element_type=F32)
           + jnp.exp(inter - m_t) * jnp.dot(q, state.astype(BF16), preferred_element_type=F32))
    den = jnp.maximum(jnp.abs(tot[:, M_HEAD_DIM:M_HEAD_DIM + 1]), jnp.exp(-m_t))
    out = tot[:, :M_HEAD_DIM] / den

    new_state, m_new = _state_update(state, m_in, k, vaug, li_col, b_col, b_tot)
    return out, new_state, m_new


def _mlstm_kernel(qf_ref, kf_ref, vf_ref, gf_ref, qb_ref, kb_ref, vb_ref, gb_ref, mmeta_ref, gmeta_ref,
                  hf_ref, hb_ref, state_scr, m_scr):
    c = pl.program_id(1)

    @pl.when(c == 0)
    def _reset_and_absorb_meta():
        state_scr[...] = jnp.zeros_like(state_scr)
        m_scr[...] = jnp.zeros_like(m_scr)
        gates = gmeta_ref[...]
        sums, _ = _gate_cumsum(gates, causal=True)
        for hd in range(M_HEADS):
            sl = slice(hd * M_HEAD_DIM, (hd + 1) * M_HEAD_DIM)
            f_lane = 2 * M_HEADS + hd
            k = mmeta_ref[:, M_WIDTH + hd * M_HEAD_DIM:M_WIDTH + (hd + 1) * M_HEAD_DIM]
            v = mmeta_ref[:, 2 * M_WIDTH + hd * M_HEAD_DIM:2 * M_WIDTH + (hd + 1) * M_HEAD_DIM]
            del sl
            new_state, m_new = _state_update(
                state_scr[hd], m_scr[hd][:1, :1], k, _value_with_ones(v),
                gates[:, hd:hd + 1], sums[:, f_lane:f_lane + 1], sums[N_META - 1:N_META, f_lane:f_lane + 1])
            state_scr[hd] = new_state
            m_scr[hd] = jnp.broadcast_to(m_new, m_scr.shape[1:])

    for causal, q_ref, k_ref, v_ref, g_ref, o_ref in ((True, qf_ref, kf_ref, vf_ref, gf_ref, hf_ref),
                                                      (False, qb_ref, kb_ref, vb_ref, gb_ref, hb_ref)):
        gates = g_ref[...]
        sums, mask = _gate_cumsum(gates, causal)
        gates_t = gates.T
        sums_t = sums.T
        for hd in range(M_HEADS):
            chain = hd if causal else M_HEADS + hd
            sl = slice(hd * M_HEAD_DIM, (hd + 1) * M_HEAD_DIM)
            out, new_state, m_new = _chain_chunk(
                q_ref[:, sl], k_ref[:, sl], v_ref[:, sl], gates, gates_t, sums, sums_t, mask, chain, causal,
                state_scr[chain], m_scr[chain][:1, :1])
            o_ref[:, sl] = out
            state_scr[chain] = new_state
            m_scr[chain] = jnp.broadcast_to(m_new, m_scr.shape[1:])


def _mlstm_call(m, gates, m_meta, gates_meta, batch, chunk):
    tokens = m.shape[0]
    nc = tokens // batch // chunk

    def fwd(col):
        return pl.BlockSpec((chunk, M_WIDTH), lambda b, c: (b * nc + c, col))

    def bwd(col):
        return pl.BlockSpec((chunk, M_WIDTH), lambda b, c: (b * nc + nc - 1 - c, col))

    gate_fwd = pl.BlockSpec((chunk, LANES), lambda b, c: (b * nc + c, 0))
    gate_bwd = pl.BlockSpec((chunk, LANES), lambda b, c: (b * nc + nc - 1 - c, 0))
    return pl.pallas_call(
        _mlstm_kernel,
        grid=(batch, nc),
        in_specs=[fwd(0), fwd(1), fwd(2), gate_fwd, bwd(0), bwd(1), bwd(2), gate_bwd,
                  pl.BlockSpec(m_meta.shape, lambda b, c: (0, 0)),
                  pl.BlockSpec(gates_meta.shape, lambda b, c: (0, 0))],
        out_specs=[pl.BlockSpec((chunk, M_WIDTH), lambda b, c: (b * nc + c, 0)),
                   pl.BlockSpec((chunk, M_WIDTH), lambda b, c: (b * nc + nc - 1 - c, 0))],
        out_shape=[jax.ShapeDtypeStruct((tokens, M_WIDTH), F32)] * 2,
        scratch_shapes=[pltpu.VMEM((2 * M_HEADS, M_HEAD_DIM, STATE_COLS), F32),
                        pltpu.VMEM((2 * M_HEADS, 8, LANES), F32)],
        compiler_params=pltpu.CompilerParams(dimension_semantics=("parallel", "arbitrary"),
                                             vmem_limit_bytes=VMEM_LIMIT_BYTES),
        name="mlstm",
    )(m, m, m, gates, m, m, m, gates, m_meta, gates_meta)


def _back_kernel(h_ref, ao_ref, hf_ref, hb_ref, mo_ref, merge_ref, mgain_ref, wab_ref, wmb_ref, wout_ref,
                 g2_ref, w1_ref, w3_ref, w2_ref, y_ref):
    d_model = h_ref.shape[1]
    a_out = jnp.dot(ao_ref[...], wab_ref[...], preferred_element_type=F32)

    hm = hf_ref[...] + hb_ref[...]
    o_gate = jax.nn.sigmoid(mo_ref[...].astype(F32))
    gated = []
    for hd in range(M_HEADS):
        sl = slice(hd * M_HEAD_DIM, (hd + 1) * M_HEAD_DIM)
        gated.append((_rmsnorm(hm[:, sl], mgain_ref[:, sl]) * o_gate[:, sl]).astype(BF16))
    m_out = jnp.dot(jnp.concatenate(gated, axis=1), wmb_ref[...], preferred_element_type=F32)

    merged = (merge_ref[:, :d_model].astype(F32) * a_out + merge_ref[:, d_model:].astype(F32) * m_out)
    h2 = h_ref[...] + jnp.dot(merged.astype(BF16), wout_ref[...], preferred_element_type=F32)
    y_ref[...] = _swiglu_half_residual(h2, g2_ref[...], w1_ref, w3_ref, w2_ref)


def _back_call(h, attn_o, hf, hb, m, merge, wts, rows):
    tokens, d_model = h.shape

    def row_spec(width, col=0):
        return pl.BlockSpec((rows, width), lambda i: (i, col))

    consts = (wts["mgain"], wts["wab"], wts["wmb"], wts["wout"], wts["g2"], wts["w1b"], wts["w3b"], wts["w2b"])
    return pl.pallas_call(
        _back_kernel,
        grid=(tokens // rows,),
        in_specs=[row_spec(d_model), row_spec(Q_PAD), row_spec(M_WIDTH), row_spec(M_WIDTH),
                  row_spec(M_WIDTH, 3), row_spec(2 * d_model)] + [_const_spec(c.shape) for c in consts],
        out_specs=row_spec(d_model),
        out_shape=jax.ShapeDtypeStruct((tokens, d_model), F32),
        compiler_params=pltpu.CompilerParams(dimension_semantics=("parallel",),
                                             vmem_limit_bytes=VMEM_LIMIT_BYTES),
        name="back",
    )(h, attn_o, hf, hb, m, merge, *consts)


def _pad_heads(w, n_heads):
    lead = w.shape[:-1]
    w = w.reshape(*lead, n_heads, HEAD_DIM)
    w = jnp.pad(w, [(0, 0)] * len(lead) + [(0, 0), (0, LANES - HEAD_DIM)])
    return w.reshape(*lead, n_heads * LANES)


def _prepare_weights(g_ffn1, w1_ffn1, w3_ffn1, w2_ffn1, g_mix, w_in, b_i, b_f, q_gain, k_gain, m_gain,
                     w_attn_br, w_mlstm_br, w_out, g_ffn2, w1_ffn2, w3_ffn2, w2_ffn2):
    d_model = w_in.shape[1]
    attn_w = N_Q_HEADS * HEAD_DIM
    kv_w = N_KV_HEADS * HEAD_DIM
    splits = (attn_w, kv_w, kv_w, M_WIDTH, M_WIDTH, M_WIDTH, M_WIDTH, 2 * M_HEADS, 2 * M_HEADS, d_model, d_model)
    offs = np.cumsum((0,) + splits)
    assert offs[-1] == w_in.shape[2]
    cols = [w_in[0, :, offs[i]:offs[i + 1]] for i in range(len(splits))]
    aq, ak, av, mq, mk, mv, mo, gi, gf, ga, gm = cols
    gate_w = jnp.pad(jnp.concatenate([gi, gf], axis=1), ((0, 0), (0, LANES - 4 * M_HEADS)))
    gate_b = jnp.pad(jnp.concatenate([b_i[0], b_f[0]]), (0, LANES - 4 * M_HEADS))[None]
    row = lambda g: g.astype(F32)[None]
    pad_gain = lambda g: jnp.pad(g[0].astype(F32), (0, LANES - HEAD_DIM))[None]
    wab = _pad_heads(w_attn_br[0].T, N_Q_HEADS).T
    return {
        "g1": row(g_ffn1[0]), "w1a": w1_ffn1[0].astype(BF16), "w3a": w3_ffn1[0].astype(BF16),
        "w2a": w2_ffn1[0].astype(BF16),
        "gmix": row(g_mix[0]),
        "wq": _pad_heads(aq, N_Q_HEADS).astype(BF16), "wk": _pad_heads(ak, N_KV_HEADS).astype(BF16),
        "wv": _pad_heads(av, N_KV_HEADS).astype(BF16),
        "wm": jnp.concatenate([mq, mk, mv, mo], axis=1).astype(BF16),
        "wg": gate_w.astype(BF16), "gbias": gate_b.astype(F32),
        "wmerge": jnp.concatenate([ga, gm], axis=1).astype(BF16),
        "qgain": pad_gain(q_gain), "kgain": pad_gain(k_gain),
        "mgain": m_gain[0].astype(F32).reshape(1, M_WIDTH),
        "wab": wab.astype(BF16), "wmb": w_mlstm_br[0].astype(BF16), "wout": w_out[0].astype(BF16),
        "g2": row(g_ffn2[0]), "w1b": w1_ffn2[0].astype(BF16), "w3b": w3_ffn2[0].astype(BF16),
        "w2b": w2_ffn2[0].astype(BF16),
    }


def _rope_tables(n_tok):
    t = jnp.arange(n_tok)
    pos = jnp.stack([(t // GRID_W).astype(F32), (t % GRID_W).astype(F32)], axis=-1)
    inv_freq = ROPE_THETA ** (-2.0 * jnp.arange(ROPE_FREQS, dtype=F32) / ROPE_AXIS_DIM)
    ang = pos[:, :, None] * inv_freq
    cos, sin = jnp.cos(ang), jnp.sin(ang)
    cos = jnp.stack([cos, cos], axis=2).reshape(n_tok, HEAD_DIM)
    sin = jnp.stack([-sin, sin], axis=2).reshape(n_tok, HEAD_DIM)
    cos = jnp.pad(cos, ((0, 0), (0, LANES - HEAD_DIM)), constant_values=1.0)
    sin = jnp.pad(sin, ((0, 0), (0, LANES - HEAD_DIM)))
    return cos, sin


def _trunk(x, meta_parts, wts):
    batch, n_tok, d_model = x.shape
    k_meta, v_meta, m_meta, gates_meta = meta_parts
    cos, sin = _rope_tables(n_tok)
    h, q, k, v, m, gates, merge = _front_call(x.reshape(batch * n_tok, d_model), cos, sin, wts, FRONT_ROWS)
    attn_o = _attn_call(q, k, v, k_meta, v_meta, batch, min(ATTN_Q_ROWS, n_tok), min(ATTN_K_ROWS, n_tok))
    hf, hb = _mlstm_call(m, gates, m_meta, gates_meta, batch, min(MLSTM_CHUNK, n_tok))
    y = _back_call(h, attn_o, hf, hb, m, merge, wts, FRONT_ROWS)
    return y.reshape(batch, n_tok, d_model)


def kernel(x_prompt, x_sample, meta, g_ffn1, w1_ffn1, w3_ffn1, w2_ffn1, g_mix, w_in, b_i, b_f, q_gain, k_gain,
           m_gain, w_attn_br, w_mlstm_br, w_out, g_ffn2, w1_ffn2, w3_ffn2, w2_ffn2):
    assert w_in.shape[0] == 1, "single-layer trunk: the meta rows' mixer outputs are never consumed"
    assert meta.shape[0] == N_META
    wts = _prepare_weights(g_ffn1, w1_ffn1, w3_ffn1, w2_ffn1, g_mix, w_in, b_i, b_f, q_gain, k_gain, m_gain,
                           w_attn_br, w_mlstm_br, w_out, g_ffn2, w1_ffn2, w3_ffn2, w2_ffn2)
    ones = jnp.ones((N_META, LANES), F32)
    _, _, k_meta, v_meta, m_meta, gates_meta, _ = _front_call(
        meta.astype(F32), ones, jnp.zeros_like(ones), wts, N_META)
    meta_parts = (k_meta, v_meta, m_meta, gates_meta)
    return (_trunk(x_prompt, meta_parts, wts), _trunk(x_sample, meta_parts, wts))
```

```python
import functools

import jax
import jax.numpy as jnp
import numpy as np
from jax import lax
from jax.experimental import pallas as pl
from jax.experimental.pallas import tpu as pltpu

F32 = jnp.float32
BF16 = jnp.bfloat16

N_META = 16
GRID_W = 64
EPS = 1e-6
NEG = -1e30
N_Q_HEADS = 8
N_KV_HEADS = 2
Q_PER_KV = N_Q_HEADS // N_KV_HEADS
HEAD_DIM = 64
ROPE_AXIS_DIM = HEAD_DIM // 2
ROPE_FREQS = ROPE_AXIS_DIM // 2
ROPE_THETA = 10000.0
M_HEADS = 4
M_HEAD_DIM = 128
M_WIDTH = M_HEADS * M_HEAD_DIM

LANES = 128
VMEM_LIMIT_BYTES = 60 * 1024 * 1024

FRONT_ROWS = 256
ATTN_Q_ROWS = 512
ATTN_K_ROWS = 1024
MLSTM_CHUNK = 256

Q_PAD = N_Q_HEADS * LANES
KV_PAD = N_KV_HEADS * LANES
Q_SCALE = HEAD_DIM ** -0.5 * 1.4426950408889634
ATTN_EXP_ROWS = 128
ATTN_V_ROWS = 80
META_ROWS = LANES
STATE_COLS = 2 * M_HEAD_DIM


def _const_spec(shape):
    zeros = (0,) * len(shape)
    return pl.BlockSpec(shape, lambda *_: zeros, pipeline_mode=pl.Buffered(1))


def _rmsnorm(x, g):
    return x * lax.rsqrt(jnp.mean(x * x, axis=-1, keepdims=True) + EPS) * g


def _swiglu_half_residual(x, g, w1_ref, w3_ref, w2_ref):
    u = _rmsnorm(x, g).astype(BF16)
    a = jnp.dot(u, w1_ref[...], preferred_element_type=F32)
    b = jnp.dot(u, w3_ref[...], preferred_element_type=F32)
    act = (a * jax.nn.sigmoid(a) * b).astype(BF16)
    return x + 0.5 * jnp.dot(act, w2_ref[...], preferred_element_type=F32)


def _log_sigmoid(x):
    return jnp.minimum(x, 0.0) - jnp.log1p(jnp.exp(-jnp.abs(x)))


def _head_norm_rope(x, gain, cos, sin_signed, first_half):
    ms = jnp.sum(x * x, axis=-1, keepdims=True) * (1.0 / HEAD_DIM)
    y = x * lax.rsqrt(ms + EPS) * gain
    partner = jnp.where(first_half, pltpu.roll(y, LANES - ROPE_FREQS, 1), pltpu.roll(y, ROPE_FREQS, 1))
    return y * cos + partner * sin_signed


def _front_kernel(x_ref, cos_ref, sin_ref, g1_ref, w1_ref, w3_ref, w2_ref, gmix_ref, wq_ref, wk_ref, wv_ref,
                  wm_ref, wg_ref, wmerge_ref, qgain_ref, kgain_ref, gbias_ref,
                  h_ref, q_ref, k_ref, v_ref, m_ref, gate_ref, merge_ref):
    h = _swiglu_half_residual(x_ref[...], g1_ref[...], w1_ref, w3_ref, w2_ref)
    h_ref[...] = h
    u = _rmsnorm(h, gmix_ref[...]).astype(BF16)

    rows = h.shape[0]
    lane = lax.broadcasted_iota(jnp.int32, (rows, LANES), 1)
    first_half = (lane % ROPE_AXIS_DIM) < ROPE_FREQS
    cos = cos_ref[...]
    sin = sin_ref[...]

    zq = jnp.dot(u, wq_ref[...], preferred_element_type=F32)
    for hd in range(N_Q_HEADS):
        sl = slice(hd * LANES, (hd + 1) * LANES)
        y = _head_norm_rope(zq[:, sl], qgain_ref[...], cos, sin, first_half)
        q_ref[sl, :] = (y * Q_SCALE).T.astype(BF16)

    zk = jnp.dot(u, wk_ref[...], preferred_element_type=F32)
    zv = jnp.dot(u, wv_ref[...], preferred_element_type=F32)
    for hd in range(N_KV_HEADS):
        sl = slice(hd * LANES, (hd + 1) * LANES)
        k_ref[:, sl] = _head_norm_rope(zk[:, sl], kgain_ref[...], cos, sin, first_half).astype(BF16)
        v_ref[sl, :] = jnp.where(lane == HEAD_DIM, 1.0, zv[:, sl]).T.astype(BF16)

    zm = jnp.dot(u, wm_ref[...], preferred_element_type=F32)
    m_ref[:, :M_WIDTH] = zm[:, :M_WIDTH].astype(BF16)
    m_ref[:, M_WIDTH:2 * M_WIDTH] = (zm[:, M_WIDTH:2 * M_WIDTH] * (M_HEAD_DIM ** -0.5)).astype(BF16)
    m_ref[:, 2 * M_WIDTH:] = zm[:, 2 * M_WIDTH:].astype(BF16)

    zg = jnp.dot(u, wg_ref[...], preferred_element_type=F32) + gbias_ref[...]
    gate_ref[...] = jnp.where(lane < 2 * M_HEADS, zg, _log_sigmoid(zg))

    zmerge = jnp.dot(u, wmerge_ref[...], preferred_element_type=F32)
    merge_ref[...] = jax.nn.sigmoid(zmerge).astype(BF16)


def _front_call(x, cos, sin, wts, rows):
    tokens, d_model = x.shape
    steps = tokens // rows
    table_steps = cos.shape[0] // rows

    def row_spec(width):
        return pl.BlockSpec((rows, width), lambda i: (i, 0))

    table_spec = pl.BlockSpec((rows, LANES), lambda i: (i % table_steps, 0))
    consts = (wts["g1"], wts["w1a"], wts["w3a"], wts["w2a"], wts["gmix"], wts["wq"], wts["wk"], wts["wv"],
              wts["wm"], wts["wg"], wts["wmerge"], wts["qgain"], wts["kgain"], wts["gbias"])
    def col_spec(height):
        return pl.BlockSpec((height, rows), lambda i: (0, i))

    out_specs = [row_spec(d_model), col_spec(Q_PAD), row_spec(KV_PAD), col_spec(KV_PAD), row_spec(4 * M_WIDTH),
                 row_spec(LANES), row_spec(2 * d_model)]
    out_shape = [jax.ShapeDtypeStruct((tokens, d_model), F32), jax.ShapeDtypeStruct((Q_PAD, tokens), BF16),
                 jax.ShapeDtypeStruct((tokens, KV_PAD), BF16), jax.ShapeDtypeStruct((KV_PAD, tokens), BF16),
                 jax.ShapeDtypeStruct((tokens, 4 * M_WIDTH), BF16), jax.ShapeDtypeStruct((tokens, LANES), F32),
                 jax.ShapeDtypeStruct((tokens, 2 * d_model), BF16)]
    return pl.pallas_call(
        _front_kernel,
        grid=(steps,),
        in_specs=[row_spec(d_model), table_spec, table_spec] + [_const_spec(c.shape) for c in consts],
        out_specs=out_specs,
        out_shape=out_shape,
        compiler_params=pltpu.CompilerParams(dimension_semantics=("parallel",),
                                             vmem_limit_bytes=VMEM_LIMIT_BYTES),
        name="front",
    )(x, cos, sin, *consts)


def _attn_kernel(qt_ref, k_ref, vt_ref, kmeta_ref, vtmeta_ref, o_ref, m_scr, acc_scr, s_scr, p_scr):
    ki = pl.program_id(2)
    q_cols = qt_ref.shape[1]

    def scores_t(hd, keys_ref):
        g = hd // Q_PER_KV
        return jnp.dot(keys_ref[:, g * LANES:(g + 1) * LANES], qt_ref[hd * LANES:(hd + 1) * LANES, :],
                       preferred_element_type=F32)

    def values_t(hd, values_ref):
        g = hd // Q_PER_KV
        return values_ref[g * LANES:g * LANES + ATTN_V_ROWS, :]

    @pl.when(ki == 0)
    def _seed_from_meta_keys():
        real = lax.broadcasted_iota(jnp.int32, (kmeta_ref.shape[0], q_cols), 0) < N_META
        for hd in range(N_Q_HEADS):
            s = jnp.where(real, scores_t(hd, kmeta_ref), NEG)
            m = jnp.max(s, axis=0, keepdims=True)
            p = jnp.exp2(s - m).astype(BF16)
            acc_scr[hd] = jnp.dot(values_t(hd, vtmeta_ref), p, preferred_element_type=F32)
            m_scr[hd] = jnp.broadcast_to(m, m_scr.shape[1:])

    k_rows = k_ref.shape[0]

    def scores_into(hd, slot):
        s = scores_t(hd, k_ref)
        s_scr[slot] = s
        return jnp.max(s, axis=0, keepdims=True)

    smax_next = scores_into(0, 0)
    for hd in range(N_Q_HEADS):
        slot = hd % 2
        smax = smax_next
        if hd + 1 < N_Q_HEADS:
            smax_next = scores_into(hd + 1, 1 - slot)
        m_old = m_scr[hd][:1]
        m_new = jnp.maximum(m_old, smax)
        for r in range(0, k_rows, ATTN_EXP_ROWS):
            rows = slice(r, r + ATTN_EXP_ROWS)
            p_scr[slot, rows, :] = jnp.exp2(s_scr[slot, rows, :] - m_new).astype(BF16)
        pv = jnp.dot(values_t(hd, vt_ref), p_scr[slot], preferred_element_type=F32)
        acc_scr[hd] = jnp.exp2(m_old - m_new) * acc_scr[hd] + pv
        m_scr[hd] = jnp.broadcast_to(m_new, m_scr.shape[1:])

    @pl.when(ki == pl.num_programs(2) - 1)
    def _normalise():
        for hd in range(N_Q_HEADS):
            acc = acc_scr[hd]
            out_t = acc / acc[HEAD_DIM:HEAD_DIM + 1]
            out_t = jnp.concatenate([out_t, jnp.zeros((LANES - ATTN_V_ROWS, q_cols), F32)], axis=0)
            o_ref[:, hd * LANES:(hd + 1) * LANES] = out_t.T.astype(BF16)


def _attn_call(qt, k, vt, k_meta, vt_meta, batch, q_rows, k_rows):
    tokens = k.shape[0]
    n_tok = tokens // batch
    nq = n_tok // q_rows
    nk = n_tok // k_rows
    return pl.pallas_call(
        _attn_kernel,
        grid=(batch, nq, nk),
        in_specs=[
            pl.BlockSpec((Q_PAD, q_rows), lambda b, i, j: (0, b * nq + i)),
            pl.BlockSpec((k_rows, KV_PAD), lambda b, i, j: (b * nk + j, 0)),
            pl.BlockSpec((KV_PAD, k_rows), lambda b, i, j: (0, b * nk + j)),
            pl.BlockSpec(k_meta.shape, lambda b, i, j: (0, 0)),
            pl.BlockSpec(vt_meta.shape, lambda b, i, j: (0, 0)),
        ],
        out_specs=pl.BlockSpec((q_rows, Q_PAD), lambda b, i, j: (b * nq + i, 0)),
        out_shape=jax.ShapeDtypeStruct((tokens, Q_PAD), BF16),
        scratch_shapes=[pltpu.VMEM((N_Q_HEADS, 8, q_rows), F32),
                        pltpu.VMEM((N_Q_HEADS, ATTN_V_ROWS, q_rows), F32),
                        pltpu.VMEM((2, k_rows, q_rows), F32),
                        pltpu.VMEM((2, k_rows, q_rows), BF16)],
        compiler_params=pltpu.CompilerParams(dimension_semantics=("parallel", "parallel", "arbitrary"),
                                             vmem_limit_bytes=VMEM_LIMIT_BYTES),
        name="attn",
    )(qt, k, vt, k_meta, vt_meta)


def _split_hi_lo(x):
    hi = x.astype(BF16)
    lo = (x - hi.astype(F32)).astype(BF16)
    return hi, lo


def _gate_cumsum(gates, causal):
    t = gates.shape[0]
    row = lax.broadcasted_iota(jnp.int32, (t, t), 0)
    col = lax.broadcasted_iota(jnp.int32, (t, t), 1)
    mask = (col <= row) if causal else (col >= row)
    tri = jnp.where(mask, 1.0, 0.0).astype(BF16)
    hi, lo = _split_hi_lo(gates)
    sums = jnp.dot(tri, hi, preferred_element_type=F32) + jnp.dot(tri, lo, preferred_element_type=F32)
    return sums, mask


def _value_with_ones(v):
    lane = lax.broadcasted_iota(jnp.int32, v.shape, 1)
    return jnp.concatenate([v, jnp.where(lane == 0, 1.0, 0.0).astype(v.dtype)], axis=1)


def _state_update(state, m_in, k, vaug, li_col, b_col, b_tot):
    w_end = b_tot - b_col + li_col
    m_new = jnp.maximum(b_tot + m_in, jnp.max(w_end, axis=0, keepdims=True))
    kw = (k.astype(F32) * jnp.exp(w_end - m_new)).astype(BF16)
    local = lax.dot_general(kw, vaug, (((0,), (0,)), ((), ())), preferred_element_type=F32)
    return jnp.exp(b_tot + m_in - m_new) * state + local, m_new


def _chain_chunk(q, k, v, gates, gates_t, sums, sums_t, mask, chain, causal, state, m_in):
    t = q.shape[0]
    i_lane = chain
    f_lane = 2 * M_HEADS + chain
    li_col = gates[:, i_lane:i_lane + 1]
    li_row = gates_t[i_lane:i_lane + 1, :]
    b_col = sums[:, f_lane:f_lane + 1]
    b_row = sums_t[f_lane:f_lane + 1, :]
    last = t - 1 if causal else 0
    b_tot = sums[last:last + 1, f_lane:f_lane + 1]

    vaug = _value_with_ones(v)
    d = jnp.where(mask, b_col - b_row + li_row, NEG)
    inter = b_col + m_in
    m_t = jnp.maximum(inter, jnp.max(d, axis=-1, keepdims=True))
    qk = lax.dot_general(q, k, (((1,), (1,)), ((), ())), preferred_element_type=F32)
    s = (qk * jnp.exp(d - m_t)).astype(BF16)
    tot = (jnp.dot(s, vaug, preferred_element_type=F32)
           + jnp.exp(inter - m_t) * jnp.dot(q, state.astype(BF16), preferred_element_type=F32))
    den = jnp.maximum(jnp.abs(tot[:, M_HEAD_DIM:M_HEAD_DIM + 1]), jnp.exp(-m_t))
    out = tot[:, :M_HEAD_DIM] / den

    new_state, m_new = _state_update(state, m_in, k, vaug, li_col, b_col, b_tot)
    return out, new_state, m_new


def _mlstm_kernel(qf_ref, kf_ref, vf_ref, gf_ref, qb_ref, kb_ref, vb_ref, gb_ref, mmeta_ref, gmeta_ref,
                  hf_ref, hb_ref, state_scr, m_scr):
    c = pl.program_id(1)

    @pl.when(c == 0)
    def _reset_and_absorb_meta():
        state_scr[...] = jnp.zeros_like(state_scr)
        m_scr[...] = jnp.zeros_like(m_scr)
        gates = gmeta_ref[:N_META, :]
        sums, _ = _gate_cumsum(gates, causal=True)
        for hd in range(M_HEADS):
            f_lane = 2 * M_HEADS + hd
            k = mmeta_ref[:N_META, M_WIDTH + hd * M_HEAD_DIM:M_WIDTH + (hd + 1) * M_HEAD_DIM]
            v = mmeta_ref[:N_META, 2 * M_WIDTH + hd * M_HEAD_DIM:2 * M_WIDTH + (hd + 1) * M_HEAD_DIM]
            new_state, m_new = _state_update(
                state_scr[hd], m_scr[hd][:1, :1], k, _value_with_ones(v),
                gates[:, hd:hd + 1], sums[:, f_lane:f_lane + 1], sums[N_META - 1:N_META, f_lane:f_lane + 1])
            state_scr[hd] = new_state
            m_scr[hd] = jnp.broadcast_to(m_new, m_scr.shape[1:])

    for causal, q_ref, k_ref, v_ref, g_ref, o_ref in ((True, qf_ref, kf_ref, vf_ref, gf_ref, hf_ref),
                                                      (False, qb_ref, kb_ref, vb_ref, gb_ref, hb_ref)):
        gates = g_ref[...]
        sums, mask = _gate_cumsum(gates, causal)
        gates_t = gates.T
        sums_t = sums.T
        for hd in range(M_HEADS):
            chain = hd if causal else M_HEADS + hd
            sl = slice(hd * M_HEAD_DIM, (hd + 1) * M_HEAD_DIM)
            out, new_state, m_new = _chain_chunk(
                q_ref[:, sl], k_ref[:, sl], v_ref[:, sl], gates, gates_t, sums, sums_t, mask, chain, causal,
                state_scr[chain], m_scr[chain][:1, :1])
            o_ref[:, sl] = out
            state_scr[chain] = new_state
            m_scr[chain] = jnp.broadcast_to(m_new, m_scr.shape[1:])


def _mlstm_call(m, gates, m_meta, gates_meta, batch, chunk):
    tokens = m.shape[0]
    nc = tokens // batch // chunk

    def fwd(col):
        return pl.BlockSpec((chunk, M_WIDTH), lambda b, c: (b * nc + c, col))

    def bwd(col):
        return pl.BlockSpec((chunk, M_WIDTH), lambda b, c: (b * nc + nc - 1 - c, col))

    gate_fwd = pl.BlockSpec((chunk, LANES), lambda b, c: (b * nc + c, 0))
    gate_bwd = pl.BlockSpec((chunk, LANES), lambda b, c: (b * nc + nc - 1 - c, 0))
    return pl.pallas_call(
        _mlstm_kernel,
        grid=(batch, nc),
        in_specs=[fwd(0), fwd(1), fwd(2), gate_fwd, bwd(0), bwd(1), bwd(2), gate_bwd,
                  pl.BlockSpec(m_meta.shape, lambda b, c: (0, 0)),
                  pl.BlockSpec(gates_meta.shape, lambda b, c: (0, 0))],
        out_specs=[pl.BlockSpec((chunk, M_WIDTH), lambda b, c: (b * nc + c, 0)),
                   pl.BlockSpec((chunk, M_WIDTH), lambda b, c: (b * nc + nc - 1 - c, 0))],
        out_shape=[jax.ShapeDtypeStruct((tokens, M_WIDTH), F32)] * 2,
        scratch_shapes=[pltpu.VMEM((2 * M_HEADS, M_HEAD_DIM, STATE_COLS), F32),
                        pltpu.VMEM((2 * M_HEADS, 8, LANES), F32)],
        compiler_params=pltpu.CompilerParams(dimension_semantics=("parallel", "arbitrary"),
                                             vmem_limit_bytes=VMEM_LIMIT_BYTES),
        name="mlstm",
    )(m, m, m, gates, m, m, m, gates, m_meta, gates_meta)


def _back_kernel(h_ref, ao_ref, hf_ref, hb_ref, mo_ref, merge_ref, mgain_ref, wab_ref, wmb_ref, wout_ref,
                 g2_ref, w1_ref, w3_ref, w2_ref, y_ref):
    d_model = h_ref.shape[1]
    a_out = jnp.dot(ao_ref[...], wab_ref[...], preferred_element_type=F32)

    hm = hf_ref[...] + hb_ref[...]
    o_gate = jax.nn.sigmoid(mo_ref[...].astype(F32))
    gated = []
    for hd in range(M_HEADS):
        sl = slice(hd * M_HEAD_DIM, (hd + 1) * M_HEAD_DIM)
        gated.append((_rmsnorm(hm[:, sl], mgain_ref[:, sl]) * o_gate[:, sl]).astype(BF16))
    m_out = jnp.dot(jnp.concatenate(gated, axis=1), wmb_ref[...], preferred_element_type=F32)

    merged = (merge_ref[:, :d_model].astype(F32) * a_out + merge_ref[:, d_model:].astype(F32) * m_out)
    h2 = h_ref[...] + jnp.dot(merged.astype(BF16), wout_ref[...], preferred_element_type=F32)
    y_ref[...] = _swiglu_half_residual(h2, g2_ref[...], w1_ref, w3_ref, w2_ref)


def _back_call(h, attn_o, hf, hb, m, merge, wts, rows):
    tokens, d_model = h.shape

    def row_spec(width, col=0):
        return pl.BlockSpec((rows, width), lambda i: (i, col))

    consts = (wts["mgain"], wts["wab"], wts["wmb"], wts["wout"], wts["g2"], wts["w1b"], wts["w3b"], wts["w2b"])
    return pl.pallas_call(
        _back_kernel,
        grid=(tokens // rows,),
        in_specs=[row_spec(d_model), row_spec(Q_PAD), row_spec(M_WIDTH), row_spec(M_WIDTH),
                  row_spec(M_WIDTH, 3), row_spec(2 * d_model)] + [_const_spec(c.shape) for c in consts],
        out_specs=row_spec(d_model),
        out_shape=jax.ShapeDtypeStruct((tokens, d_model), F32),
        compiler_params=pltpu.CompilerParams(dimension_semantics=("parallel",),
                                             vmem_limit_bytes=VMEM_LIMIT_BYTES),
        name="back",
    )(h, attn_o, hf, hb, m, merge, *consts)


def _pad_heads(w, n_heads):
    lead = w.shape[:-1]
    w = w.reshape(*lead, n_heads, HEAD_DIM)
    w = jnp.pad(w, [(0, 0)] * len(lead) + [(0, 0), (0, LANES - HEAD_DIM)])
    return w.reshape(*lead, n_heads * LANES)


def _prepare_weights(g_ffn1, w1_ffn1, w3_ffn1, w2_ffn1, g_mix, w_in, b_i, b_f, q_gain, k_gain, m_gain,
                     w_attn_br, w_mlstm_br, w_out, g_ffn2, w1_ffn2, w3_ffn2, w2_ffn2):
    d_model = w_in.shape[1]
    attn_w = N_Q_HEADS * HEAD_DIM
    kv_w = N_KV_HEADS * HEAD_DIM
    splits = (attn_w, kv_w, kv_w, M_WIDTH, M_WIDTH, M_WIDTH, M_WIDTH, 2 * M_HEADS, 2 * M_HEADS, d_model, d_model)
    offs = np.cumsum((0,) + splits)
    assert offs[-1] == w_in.shape[2]
    cols = [w_in[0, :, offs[i]:offs[i + 1]] for i in range(len(splits))]
    aq, ak, av, mq, mk, mv, mo, gi, gf, ga, gm = cols
    gate_w = jnp.pad(jnp.concatenate([gi, gf], axis=1), ((0, 0), (0, LANES - 4 * M_HEADS)))
    gate_b = jnp.pad(jnp.concatenate([b_i[0], b_f[0]]), (0, LANES - 4 * M_HEADS))[None]
    row = lambda g: g.astype(F32)[None]
    pad_gain = lambda g: jnp.pad(g[0].astype(F32), (0, LANES - HEAD_DIM))[None]
    wab = _pad_heads(w_attn_br[0].T, N_Q_HEADS).T
    return {
        "g1": row(g_ffn1[0]), "w1a": w1_ffn1[0].astype(BF16), "w3a": w3_ffn1[0].astype(BF16),
        "w2a": w2_ffn1[0].astype(BF16),
        "gmix": row(g_mix[0]),
        "wq": _pad_heads(aq, N_Q_HEADS).astype(BF16), "wk": _pad_heads(ak, N_KV_HEADS).astype(BF16),
        "wv": _pad_heads(av, N_KV_HEADS).astype(BF16),
        "wm": jnp.concatenate([mq, mk, mv, mo], axis=1).astype(BF16),
        "wg": gate_w.astype(BF16), "gbias": gate_b.astype(F32),
        "wmerge": jnp.concatenate([ga, gm], axis=1).astype(BF16),
        "qgain": pad_gain(q_gain), "kgain": pad_gain(k_gain),
        "mgain": m_gain[0].astype(F32).reshape(1, M_WIDTH),
        "wab": wab.astype(BF16), "wmb": w_mlstm_br[0].astype(BF16), "wout": w_out[0].astype(BF16),
        "g2": row(g_ffn2[0]), "w1b": w1_ffn2[0].astype(BF16), "w3b": w3_ffn2[0].astype(BF16),
        "w2b": w2_ffn2[0].astype(BF16),
    }


def _rope_tables(n_tok):
    t = jnp.arange(n_tok)
    pos = jnp.stack([(t // GRID_W).astype(F32), (t % GRID_W).astype(F32)], axis=-1)
    inv_freq = ROPE_THETA ** (-2.0 * jnp.arange(ROPE_FREQS, dtype=F32) / ROPE_AXIS_DIM)
    ang = pos[:, :, None] * inv_freq
    cos, sin = jnp.cos(ang), jnp.sin(ang)
    cos = jnp.stack([cos, cos], axis=2).reshape(n_tok, HEAD_DIM)
    sin = jnp.stack([-sin, sin], axis=2).reshape(n_tok, HEAD_DIM)
    cos = jnp.pad(cos, ((0, 0), (0, LANES - HEAD_DIM)), constant_values=1.0)
    sin = jnp.pad(sin, ((0, 0), (0, LANES - HEAD_DIM)))
    return cos, sin


def _trunk(x, meta_parts, wts):
    batch, n_tok, d_model = x.shape
    k_meta, v_meta, m_meta, gates_meta = meta_parts
    cos, sin = _rope_tables(n_tok)
    h, q, k, v, m, gates, merge = _front_call(x.reshape(batch * n_tok, d_model), cos, sin, wts, FRONT_ROWS)
    attn_o = _attn_call(q, k, v, k_meta, v_meta, batch, min(ATTN_Q_ROWS, n_tok), min(ATTN_K_ROWS, n_tok))
    hf, hb = _mlstm_call(m, gates, m_meta, gates_meta, batch, min(MLSTM_CHUNK, n_tok))
    y = _back_call(h, attn_o, hf, hb, m, merge, wts, FRONT_ROWS)
    return y.reshape(batch, n_tok, d_model)


def kernel(x_prompt, x_sample, meta, g_ffn1, w1_ffn1, w3_ffn1, w2_ffn1, g_mix, w_in, b_i, b_f, q_gain, k_gain,
           m_gain, w_attn_br, w_mlstm_br, w_out, g_ffn2, w1_ffn2, w3_ffn2, w2_ffn2):
    assert w_in.shape[0] == 1, "single-layer trunk: the meta rows' mixer outputs are never consumed"
    assert meta.shape[0] == N_META
    wts = _prepare_weights(g_ffn1, w1_ffn1, w3_ffn1, w2_ffn1, g_mix, w_in, b_i, b_f, q_gain, k_gain, m_gain,
                           w_attn_br, w_mlstm_br, w_out, g_ffn2, w1_ffn2, w3_ffn2, w2_ffn2)
    ones = jnp.ones((META_ROWS, LANES), F32)
    meta_rows = jnp.pad(meta.astype(F32), ((0, META_ROWS - N_META), (0, 0)))
    _, _, k_meta, v_meta, m_meta, gates_meta, _ = _front_call(meta_rows, ones, jnp.zeros_like(ones), wts, META_ROWS)
    meta_parts = (k_meta, v_meta, m_meta, gates_meta)
    return (_trunk(x_prompt, meta_parts, wts), _trunk(x_sample, meta_parts, wts))
```

```python
import functools

import jax
import jax.numpy as jnp
import numpy as np
from jax import lax
from jax.experimental import pallas as pl
from jax.experimental.pallas import tpu as pltpu

F32 = jnp.float32
BF16 = jnp.bfloat16

N_META = 16
GRID_W = 64
EPS = 1e-6
NEG = -1e30
N_Q_HEADS = 8
N_KV_HEADS = 2
Q_PER_KV = N_Q_HEADS // N_KV_HEADS
HEAD_DIM = 64
ROPE_AXIS_DIM = HEAD_DIM // 2
ROPE_FREQS = ROPE_AXIS_DIM // 2
ROPE_THETA = 10000.0
M_HEADS = 4
M_HEAD_DIM = 128
M_WIDTH = M_HEADS * M_HEAD_DIM

LANES = 128
VMEM_LIMIT_BYTES = 60 * 1024 * 1024

FRONT_ROWS = 256
ATTN_Q_ROWS = 512
ATTN_K_ROWS = 1024
MLSTM_CHUNK = 256

Q_PAD = N_Q_HEADS * LANES
KV_PAD = N_KV_HEADS * LANES
Q_SCALE = HEAD_DIM ** -0.5 * 1.4426950408889634
ATTN_SAFE_LOG2 = 90.0
ATTN_V_ROWS = 80
META_ROWS = LANES
STATE_COLS = 2 * M_HEAD_DIM


def _const_spec(shape):
    zeros = (0,) * len(shape)
    return pl.BlockSpec(shape, lambda *_: zeros, pipeline_mode=pl.Buffered(1))


def _rmsnorm(x, g):
    return x * lax.rsqrt(jnp.mean(x * x, axis=-1, keepdims=True) + EPS) * g


def _swiglu_half_residual(x, g, w1_ref, w3_ref, w2_ref):
    u = _rmsnorm(x, g).astype(BF16)
    a = jnp.dot(u, w1_ref[...], preferred_element_type=F32)
    b = jnp.dot(u, w3_ref[...], preferred_element_type=F32)
    act = (a * jax.nn.sigmoid(a) * b).astype(BF16)
    return x + 0.5 * jnp.dot(act, w2_ref[...], preferred_element_type=F32)


def _log_sigmoid(x):
    return jnp.minimum(x, 0.0) - jnp.log1p(jnp.exp(-jnp.abs(x)))


def _head_norm_rope(x, gain, cos, sin_signed, first_half):
    ms = jnp.sum(x * x, axis=-1, keepdims=True) * (1.0 / HEAD_DIM)
    y = x * lax.rsqrt(ms + EPS) * gain
    partner = jnp.where(first_half, pltpu.roll(y, LANES - ROPE_FREQS, 1), pltpu.roll(y, ROPE_FREQS, 1))
    return y * cos + partner * sin_signed


def _front_kernel(x_ref, cos_ref, sin_ref, g1_ref, w1_ref, w3_ref, w2_ref, gmix_ref, wq_ref, wk_ref, wv_ref,
                  wm_ref, wg_ref, wmerge_ref, qgain_ref, kgain_ref, gbias_ref,
                  h_ref, q_ref, k_ref, v_ref, m_ref, gate_ref, merge_ref, norm_ref):
    h = _swiglu_half_residual(x_ref[...], g1_ref[...], w1_ref, w3_ref, w2_ref)
    h_ref[...] = h
    u = _rmsnorm(h, gmix_ref[...]).astype(BF16)

    rows = h.shape[0]
    lane = lax.broadcasted_iota(jnp.int32, (rows, LANES), 1)
    first_half = (lane % ROPE_AXIS_DIM) < ROPE_FREQS
    cos = cos_ref[...]
    sin = sin_ref[...]

    def sq_norm(y_bf16):
        y = y_bf16.astype(F32)
        return jnp.sum(y * y, axis=-1, keepdims=True)

    zq = jnp.dot(u, wq_ref[...], preferred_element_type=F32)
    q_sq = jnp.zeros((rows, 1), F32)
    for hd in range(N_Q_HEADS):
        sl = slice(hd * LANES, (hd + 1) * LANES)
        y = _head_norm_rope(zq[:, sl], qgain_ref[...], cos, sin, first_half) * Q_SCALE
        q_ref[sl, :] = y.T.astype(BF16)
        q_sq = jnp.maximum(q_sq, sq_norm(y.astype(BF16)))

    zk = jnp.dot(u, wk_ref[...], preferred_element_type=F32)
    zv = jnp.dot(u, wv_ref[...], preferred_element_type=F32)
    k_sq = jnp.zeros((rows, 1), F32)
    for hd in range(N_KV_HEADS):
        sl = slice(hd * LANES, (hd + 1) * LANES)
        y = _head_norm_rope(zk[:, sl], kgain_ref[...], cos, sin, first_half).astype(BF16)
        k_ref[:, sl] = y
        k_sq = jnp.maximum(k_sq, sq_norm(y))
        v_ref[sl, :] = jnp.where(lane == HEAD_DIM, 1.0, zv[:, sl]).T.astype(BF16)

    tile_row = lax.broadcasted_iota(jnp.int32, norm_ref.shape, 0)
    norm_ref[...] = jnp.where(tile_row == 0, jnp.max(q_sq, axis=0, keepdims=True),
                              jnp.max(k_sq, axis=0, keepdims=True))

    zm = jnp.dot(u, wm_ref[...], preferred_element_type=F32)
    m_ref[:, :M_WIDTH] = zm[:, :M_WIDTH].astype(BF16)
    m_ref[:, M_WIDTH:2 * M_WIDTH] = (zm[:, M_WIDTH:2 * M_WIDTH] * (M_HEAD_DIM ** -0.5)).astype(BF16)
    m_ref[:, 2 * M_WIDTH:] = zm[:, 2 * M_WIDTH:].astype(BF16)

    zg = jnp.dot(u, wg_ref[...], preferred_element_type=F32) + gbias_ref[...]
    gate_ref[...] = jnp.where(lane < 2 * M_HEADS, zg, _log_sigmoid(zg))

    zmerge = jnp.dot(u, wmerge_ref[...], preferred_element_type=F32)
    merge_ref[...] = jax.nn.sigmoid(zmerge).astype(BF16)


def _front_call(x, cos, sin, wts, rows):
    tokens, d_model = x.shape
    steps = tokens // rows
    table_steps = cos.shape[0] // rows

    def row_spec(width):
        return pl.BlockSpec((rows, width), lambda i: (i, 0))

    table_spec = pl.BlockSpec((rows, LANES), lambda i: (i % table_steps, 0))
    consts = (wts["g1"], wts["w1a"], wts["w3a"], wts["w2a"], wts["gmix"], wts["wq"], wts["wk"], wts["wv"],
              wts["wm"], wts["wg"], wts["wmerge"], wts["qgain"], wts["kgain"], wts["gbias"])

    def col_spec(height):
        return pl.BlockSpec((height, rows), lambda i: (0, i))

    out_specs = [row_spec(d_model), col_spec(Q_PAD), row_spec(KV_PAD), col_spec(KV_PAD), row_spec(4 * M_WIDTH),
                 row_spec(LANES), row_spec(2 * d_model), pl.BlockSpec((None, 8, LANES), lambda i: (i, 0, 0))]
    out_shape = [jax.ShapeDtypeStruct((tokens, d_model), F32), jax.ShapeDtypeStruct((Q_PAD, tokens), BF16),
                 jax.ShapeDtypeStruct((tokens, KV_PAD), BF16), jax.ShapeDtypeStruct((KV_PAD, tokens), BF16),
                 jax.ShapeDtypeStruct((tokens, 4 * M_WIDTH), BF16), jax.ShapeDtypeStruct((tokens, LANES), F32),
                 jax.ShapeDtypeStruct((tokens, 2 * d_model), BF16), jax.ShapeDtypeStruct((steps, 8, LANES), F32)]
    return pl.pallas_call(
        _front_kernel,
        grid=(steps,),
        in_specs=[row_spec(d_model), table_spec, table_spec] + [_const_spec(c.shape) for c in consts],
        out_specs=out_specs,
        out_shape=out_shape,
        compiler_params=pltpu.CompilerParams(dimension_semantics=("parallel",),
                                             vmem_limit_bytes=VMEM_LIMIT_BYTES),
        name="front",
    )(x, cos, sin, *consts)


def _attn_kernel(safe_ref, floor_ref, qt_ref, k_ref, vt_ref, kmeta_ref, vtmeta_ref, o_ref, m_scr, acc_scr):
    ki = pl.program_id(2)
    q_cols = qt_ref.shape[1]
    step = (pl.program_id(0) * pl.num_programs(1) + pl.program_id(1)) * pl.num_programs(2) + ki
    safe = safe_ref[step]

    def scores_t(hd, keys_ref):
        g = hd // Q_PER_KV
        return jnp.dot(keys_ref[:, g * LANES:(g + 1) * LANES], qt_ref[hd * LANES:(hd + 1) * LANES, :],
                       preferred_element_type=F32)

    def values_t(hd, values_ref):
        g = hd // Q_PER_KV
        return values_ref[g * LANES:g * LANES + ATTN_V_ROWS, :]

    def meta_scores_t(hd):
        real = lax.broadcasted_iota(jnp.int32, (kmeta_ref.shape[0], q_cols), 0) < N_META
        return jnp.where(real, scores_t(hd, kmeta_ref), NEG)

    @pl.when((ki == 0) & (safe == 1))
    def _seed_from_meta_keys_against_floor():
        floor = jnp.full((1, q_cols), floor_ref[pl.program_id(0) * pl.num_programs(1) + pl.program_id(1)], F32)
        scores = [meta_scores_t(hd) for hd in range(N_Q_HEADS)]
        for hd, s in enumerate(scores):
            p = jnp.exp2(s - floor).astype(BF16)
            acc = jnp.dot(values_t(hd, vtmeta_ref), p, preferred_element_type=F32)
            m = jnp.maximum(floor, jnp.max(s, axis=0, keepdims=True))
            acc_scr[hd] = jnp.exp2(floor - m) * acc
            m_scr[hd] = jnp.broadcast_to(m, m_scr.shape[1:])

    @pl.when((ki == 0) & (safe != 1))
    def _seed_from_meta_keys():
        for hd in range(N_Q_HEADS):
            s = meta_scores_t(hd)
            m = jnp.max(s, axis=0, keepdims=True)
            p = jnp.exp2(s - m).astype(BF16)
            acc_scr[hd] = jnp.dot(values_t(hd, vtmeta_ref), p, preferred_element_type=F32)
            m_scr[hd] = jnp.broadcast_to(m, m_scr.shape[1:])

    @pl.when(safe == 1)
    def _block_against_previous_maximum():
        for hd in range(N_Q_HEADS):
            m_old = m_scr[hd][:1]
            s = scores_t(hd, k_ref)
            p = jnp.exp2(s - m_old).astype(BF16)
            pv = jnp.dot(values_t(hd, vt_ref), p, preferred_element_type=F32)
            m_new = jnp.maximum(m_old, jnp.max(s, axis=0, keepdims=True))
            acc_scr[hd] = jnp.exp2(m_old - m_new) * (acc_scr[hd] + pv)
            m_scr[hd] = jnp.broadcast_to(m_new, m_scr.shape[1:])

    @pl.when(safe != 1)
    def _block_against_own_maximum():
        for hd in range(N_Q_HEADS):
            m_old = m_scr[hd][:1]
            s = scores_t(hd, k_ref)
            m_new = jnp.maximum(m_old, jnp.max(s, axis=0, keepdims=True))
            p = jnp.exp2(s - m_new).astype(BF16)
            pv = jnp.dot(values_t(hd, vt_ref), p, preferred_element_type=F32)
            acc_scr[hd] = jnp.exp2(m_old - m_new) * acc_scr[hd] + pv
            m_scr[hd] = jnp.broadcast_to(m_new, m_scr.shape[1:])

    @pl.when(ki == pl.num_programs(2) - 1)
    def _normalise():
        for hd in range(N_Q_HEADS):
            acc = acc_scr[hd]
            out_t = acc / acc[HEAD_DIM:HEAD_DIM + 1]
            out_t = jnp.concatenate([out_t, jnp.zeros((LANES - ATTN_V_ROWS, q_cols), F32)], axis=0)
            o_ref[:, hd * LANES:(hd + 1) * LANES] = out_t.T.astype(BF16)


def _attn_safe_blocks(norms, norms_meta, batch, nq, nk):
    def block_max(sq, blocks):
        return jnp.sqrt(jnp.max(sq.reshape(blocks, -1), axis=1))

    q_norm = block_max(norms[:, 0, 0], batch * nq).reshape(batch, nq, 1)
    k_norm = block_max(norms[:, 1, 0], batch * nk).reshape(batch, 1, nk)
    k_meta_norm = jnp.sqrt(jnp.max(norms_meta[:, 1, 0]))
    gap = q_norm * (jnp.maximum(k_norm, k_meta_norm) + k_meta_norm)
    safe = (gap <= ATTN_SAFE_LOG2).astype(jnp.int32).reshape(-1)
    floor = (-q_norm * k_meta_norm).reshape(-1)
    return safe, floor


def _attn_call(qt, k, vt, k_meta, vt_meta, safe, floor, batch, q_rows, k_rows):
    tokens = k.shape[0]
    n_tok = tokens // batch
    nq = n_tok // q_rows
    nk = n_tok // k_rows
    grid_spec = pltpu.PrefetchScalarGridSpec(
        num_scalar_prefetch=2,
        grid=(batch, nq, nk),
        in_specs=[
            pl.BlockSpec((Q_PAD, q_rows), lambda b, i, j, *_: (0, b * nq + i)),
            pl.BlockSpec((k_rows, KV_PAD), lambda b, i, j, *_: (b * nk + j, 0)),
            pl.BlockSpec((KV_PAD, k_rows), lambda b, i, j, *_: (0, b * nk + j)),
            pl.BlockSpec(k_meta.shape, lambda b, i, j, *_: (0, 0)),
            pl.BlockSpec(vt_meta.shape, lambda b, i, j, *_: (0, 0)),
        ],
        out_specs=pl.BlockSpec((q_rows, Q_PAD), lambda b, i, j, *_: (b * nq + i, 0)),
        scratch_shapes=[pltpu.VMEM((N_Q_HEADS, 8, q_rows), F32),
                        pltpu.VMEM((N_Q_HEADS, ATTN_V_ROWS, q_rows), F32)],
    )
    return pl.pallas_call(
        _attn_kernel,
        grid_spec=grid_spec,
        out_shape=jax.ShapeDtypeStruct((tokens, Q_PAD), BF16),
        compiler_params=pltpu.CompilerParams(dimension_semantics=("parallel", "parallel", "arbitrary"),
                                             vmem_limit_bytes=VMEM_LIMIT_BYTES),
        name="attn",
    )(safe, floor, qt, k, vt, k_meta, vt_meta)


def _split_hi_lo(x):
    hi = x.astype(BF16)
    lo = (x - hi.astype(F32)).astype(BF16)
    return hi, lo


def _gate_cumsum(gates, causal):
    t = gates.shape[0]
    row = lax.broadcasted_iota(jnp.int32, (t, t), 0)
    col = lax.broadcasted_iota(jnp.int32, (t, t), 1)
    mask = (col <= row) if causal else (col >= row)
    tri = jnp.where(mask, 1.0, 0.0).astype(BF16)
    hi, lo = _split_hi_lo(gates)
    sums = jnp.dot(tri, hi, preferred_element_type=F32) + jnp.dot(tri, lo, preferred_element_type=F32)
    return sums, mask


def _value_with_ones(v):
    lane = lax.broadcasted_iota(jnp.int32, v.shape, 1)
    return jnp.concatenate([v, jnp.where(lane == 0, 1.0, 0.0).astype(v.dtype)], axis=1)


def _state_update(state, m_in, k, vaug, li_col, b_col, b_tot):
    w_end = b_tot - b_col + li_col
    m_new = jnp.maximum(b_tot + m_in, jnp.max(w_end, axis=0, keepdims=True))
    kw = (k.astype(F32) * jnp.exp(w_end - m_new)).astype(BF16)
    local = lax.dot_general(kw, vaug, (((0,), (0,)), ((), ())), preferred_element_type=F32)
    return jnp.exp(b_tot + m_in - m_new) * state + local, m_new


def _chain_chunk(q, k, v, gates, gates_t, sums, sums_t, mask, chain, causal, state, m_in):
    t = q.shape[0]
    i_lane = chain
    f_lane = 2 * M_HEADS + chain
    li_col = gates[:, i_lane:i_lane + 1]
    li_row = gates_t[i_lane:i_lane + 1, :]
    b_col = sums[:, f_lane:f_lane + 1]
    b_row = sums_t[f_lane:f_lane + 1, :]
    last = t - 1 if causal else 0
    b_tot = sums[last:last + 1, f_lane:f_lane + 1]

    vaug = _value_with_ones(v)
    d = jnp.where(mask, b_col - b_row + li_row, NEG)
    inter = b_col + m_in
    m_t = jnp.maximum(inter, jnp.max(d, axis=-1, keepdims=True))
    qk = lax.dot_general(q, k, (((1,), (1,)), ((), ())), preferred_element_type=F32)
    s = (qk * jnp.exp(d - m_t)).astype(BF16)
    tot = (jnp.dot(s, vaug, preferred_element_type=F32)
           + jnp.exp(inter - m_t) * jnp.dot(q, state.astype(BF16), preferred_element_type=F32))
    den = jnp.maximum(jnp.abs(tot[:, M_HEAD_DIM:M_HEAD_DIM + 1]), jnp.exp(-m_t))
    out = tot[:, :M_HEAD_DIM] / den

    new_state, m_new = _state_update(state, m_in, k, vaug, li_col, b_col, b_tot)
    return out, new_state, m_new


def _mlstm_kernel(qf_ref, kf_ref, vf_ref, gf_ref, qb_ref, kb_ref, vb_ref, gb_ref, mmeta_ref, gmeta_ref,
                  hf_ref, hb_ref, state_scr, m_scr):
    c = pl.program_id(1)

    @pl.when(c == 0)
    def _reset_and_absorb_meta():
        state_scr[...] = jnp.zeros_like(state_scr)
        m_scr[...] = jnp.zeros_like(m_scr)
        gates = gmeta_ref[:N_META, :]
        sums, _ = _gate_cumsum(gates, causal=True)
        for hd in range(M_HEADS):
            f_lane = 2 * M_HEADS + hd
            k = mmeta_ref[:N_META, M_WIDTH + hd * M_HEAD_DIM:M_WIDTH + (hd + 1) * M_HEAD_DIM]
            v = mmeta_ref[:N_META, 2 * M_WIDTH + hd * M_HEAD_DIM:2 * M_WIDTH + (hd + 1) * M_HEAD_DIM]
            new_state, m_new = _state_update(
                state_scr[hd], m_scr[hd][:1, :1], k, _value_with_ones(v),
                gates[:, hd:hd + 1], sums[:, f_lane:f_lane + 1], sums[N_META - 1:N_META, f_lane:f_lane + 1])
            state_scr[hd] = new_state
            m_scr[hd] = jnp.broadcast_to(m_new, m_scr.shape[1:])

    for causal, q_ref, k_ref, v_ref, g_ref, o_ref in ((True, qf_ref, kf_ref, vf_ref, gf_ref, hf_ref),
                                                      (False, qb_ref, kb_ref, vb_ref, gb_ref, hb_ref)):
        gates = g_ref[...]
        sums, mask = _gate_cumsum(gates, causal)
        gates_t = gates.T
        sums_t = sums.T
        for hd in range(M_HEADS):
            chain = hd if causal else M_HEADS + hd
            sl = slice(hd * M_HEAD_DIM, (hd + 1) * M_HEAD_DIM)
            out, new_state, m_new = _chain_chunk(
                q_ref[:, sl], k_ref[:, sl], v_ref[:, sl], gates, gates_t, sums, sums_t, mask, chain, causal,
                state_scr[chain], m_scr[chain][:1, :1])
            o_ref[:, sl] = out
            state_scr[chain] = new_state
            m_scr[chain] = jnp.broadcast_to(m_new, m_scr.shape[1:])


def _mlstm_call(m, gates, m_meta, gates_meta, batch, chunk):
    tokens = m.shape[0]
    nc = tokens // batch // chunk

    def fwd(col):
        return pl.BlockSpec((chunk, M_WIDTH), lambda b, c: (b * nc + c, col))

    def bwd(col):
        return pl.BlockSpec((chunk, M_WIDTH), lambda b, c: (b * nc + nc - 1 - c, col))

    gate_fwd = pl.BlockSpec((chunk, LANES), lambda b, c: (b * nc + c, 0))
    gate_bwd = pl.BlockSpec((chunk, LANES), lambda b, c: (b * nc + nc - 1 - c, 0))
    return pl.pallas_call(
        _mlstm_kernel,
        grid=(batch, nc),
        in_specs=[fwd(0), fwd(1), fwd(2), gate_fwd, bwd(0), bwd(1), bwd(2), gate_bwd,
                  pl.BlockSpec(m_meta.shape, lambda b, c: (0, 0)),
                  pl.BlockSpec(gates_meta.shape, lambda b, c: (0, 0))],
        out_specs=[pl.BlockSpec((chunk, M_WIDTH), lambda b, c: (b * nc + c, 0)),
                   pl.BlockSpec((chunk, M_WIDTH), lambda b, c: (b * nc + nc - 1 - c, 0))],
        out_shape=[jax.ShapeDtypeStruct((tokens, M_WIDTH), F32)] * 2,
        scratch_shapes=[pltpu.VMEM((2 * M_HEADS, M_HEAD_DIM, STATE_COLS), F32),
                        pltpu.VMEM((2 * M_HEADS, 8, LANES), F32)],
        compiler_params=pltpu.CompilerParams(dimension_semantics=("parallel", "arbitrary"),
                                             vmem_limit_bytes=VMEM_LIMIT_BYTES),
        name="mlstm",
    )(m, m, m, gates, m, m, m, gates, m_meta, gates_meta)


def _back_kernel(h_ref, ao_ref, hf_ref, hb_ref, mo_ref, merge_ref, mgain_ref, wab_ref, wmb_ref, wout_ref,
                 g2_ref, w1_ref, w3_ref, w2_ref, y_ref):
    d_model = h_ref.shape[1]
    a_out = jnp.dot(ao_ref[...], wab_ref[...], preferred_element_type=F32)

    hm = hf_ref[...] + hb_ref[...]
    o_gate = jax.nn.sigmoid(mo_ref[...].astype(F32))
    gated = []
    for hd in range(M_HEADS):
        sl = slice(hd * M_HEAD_DIM, (hd + 1) * M_HEAD_DIM)
        gated.append((_rmsnorm(hm[:, sl], mgain_ref[:, sl]) * o_gate[:, sl]).astype(BF16))
    m_out = jnp.dot(jnp.concatenate(gated, axis=1), wmb_ref[...], preferred_element_type=F32)

    merged = (merge_ref[:, :d_model].astype(F32) * a_out + merge_ref[:, d_model:].astype(F32) * m_out)
    h2 = h_ref[...] + jnp.dot(merged.astype(BF16), wout_ref[...], preferred_element_type=F32)
    y_ref[...] = _swiglu_half_residual(h2, g2_ref[...], w1_ref, w3_ref, w2_ref)


def _back_call(h, attn_o, hf, hb, m, merge, wts, rows):
    tokens, d_model = h.shape

    def row_spec(width, col=0):
        return pl.BlockSpec((rows, width), lambda i: (i, col))

    consts = (wts["mgain"], wts["wab"], wts["wmb"], wts["wout"], wts["g2"], wts["w1b"], wts["w3b"], wts["w2b"])
    return pl.pallas_call(
        _back_kernel,
        grid=(tokens // rows,),
        in_specs=[row_spec(d_model), row_spec(Q_PAD), row_spec(M_WIDTH), row_spec(M_WIDTH),
                  row_spec(M_WIDTH, 3), row_spec(2 * d_model)] + [_const_spec(c.shape) for c in consts],
        out_specs=row_spec(d_model),
        out_shape=jax.ShapeDtypeStruct((tokens, d_model), F32),
        compiler_params=pltpu.CompilerParams(dimension_semantics=("parallel",),
                                             vmem_limit_bytes=VMEM_LIMIT_BYTES),
        name="back",
    )(h, attn_o, hf, hb, m, merge, *consts)


def _pad_heads(w, n_heads):
    lead = w.shape[:-1]
    w = w.reshape(*lead, n_heads, HEAD_DIM)
    w = jnp.pad(w, [(0, 0)] * len(lead) + [(0, 0), (0, LANES - HEAD_DIM)])
    return w.reshape(*lead, n_heads * LANES)


def _prepare_weights(g_ffn1, w1_ffn1, w3_ffn1, w2_ffn1, g_mix, w_in, b_i, b_f, q_gain, k_gain, m_gain,
                     w_attn_br, w_mlstm_br, w_out, g_ffn2, w1_ffn2, w3_ffn2, w2_ffn2):
    d_model = w_in.shape[1]
    attn_w = N_Q_HEADS * HEAD_DIM
    kv_w = N_KV_HEADS * HEAD_DIM
    splits = (attn_w, kv_w, kv_w, M_WIDTH, M_WIDTH, M_WIDTH, M_WIDTH, 2 * M_HEADS, 2 * M_HEADS, d_model, d_model)
    offs = np.cumsum((0,) + splits)
    assert offs[-1] == w_in.shape[2]
    cols = [w_in[0, :, offs[i]:offs[i + 1]] for i in range(len(splits))]
    aq, ak, av, mq, mk, mv, mo, gi, gf, ga, gm = cols
    gate_w = jnp.pad(jnp.concatenate([gi, gf], axis=1), ((0, 0), (0, LANES - 4 * M_HEADS)))
    gate_b = jnp.pad(jnp.concatenate([b_i[0], b_f[0]]), (0, LANES - 4 * M_HEADS))[None]
    row = lambda g: g.astype(F32)[None]
    pad_gain = lambda g: jnp.pad(g[0].astype(F32), (0, LANES - HEAD_DIM))[None]
    wab = _pad_heads(w_attn_br[0].T, N_Q_HEADS).T
    return {
        "g1": row(g_ffn1[0]), "w1a": w1_ffn1[0].astype(BF16), "w3a": w3_ffn1[0].astype(BF16),
        "w2a": w2_ffn1[0].astype(BF16),
        "gmix": row(g_mix[0]),
        "wq": _pad_heads(aq, N_Q_HEADS).astype(BF16), "wk": _pad_heads(ak, N_KV_HEADS).astype(BF16),
        "wv": _pad_heads(av, N_KV_HEADS).astype(BF16),
        "wm": jnp.concatenate([mq, mk, mv, mo], axis=1).astype(BF16),
        "wg": gate_w.astype(BF16), "gbias": gate_b.astype(F32),
        "wmerge": jnp.concatenate([ga, gm], axis=1).astype(BF16),
        "qgain": pad_gain(q_gain), "kgain": pad_gain(k_gain),
        "mgain": m_gain[0].astype(F32).reshape(1, M_WIDTH),
        "wab": wab.astype(BF16), "wmb": w_mlstm_br[0].astype(BF16), "wout": w_out[0].astype(BF16),
        "g2": row(g_ffn2[0]), "w1b": w1_ffn2[0].astype(BF16), "w3b": w3_ffn2[0].astype(BF16),
        "w2b": w2_ffn2[0].astype(BF16),
    }


def _rope_tables(n_tok):
    t = jnp.arange(n_tok)
    pos = jnp.stack([(t // GRID_W).astype(F32), (t % GRID_W).astype(F32)], axis=-1)
    inv_freq = ROPE_THETA ** (-2.0 * jnp.arange(ROPE_FREQS, dtype=F32) / ROPE_AXIS_DIM)
    ang = pos[:, :, None] * inv_freq
    cos, sin = jnp.cos(ang), jnp.sin(ang)
    cos = jnp.stack([cos, cos], axis=2).reshape(n_tok, HEAD_DIM)
    sin = jnp.stack([-sin, sin], axis=2).reshape(n_tok, HEAD_DIM)
    cos = jnp.pad(cos, ((0, 0), (0, LANES - HEAD_DIM)), constant_values=1.0)
    sin = jnp.pad(sin, ((0, 0), (0, LANES - HEAD_DIM)))
    return cos, sin


def _trunk(x, meta_parts, wts):
    batch, n_tok, d_model = x.shape
    k_meta, v_meta, m_meta, gates_meta, norms_meta = meta_parts
    cos, sin = _rope_tables(n_tok)
    h, q, k, v, m, gates, merge, norms = _front_call(x.reshape(batch * n_tok, d_model), cos, sin, wts, FRONT_ROWS)
    q_rows, k_rows = min(ATTN_Q_ROWS, n_tok), min(ATTN_K_ROWS, n_tok)
    safe, floor = _attn_safe_blocks(norms, norms_meta, batch, n_tok // q_rows, n_tok // k_rows)
    attn_o = _attn_call(q, k, v, k_meta, v_meta, safe, floor, batch, q_rows, k_rows)
    hf, hb = _mlstm_call(m, gates, m_meta, gates_meta, batch, min(MLSTM_CHUNK, n_tok))
    y = _back_call(h, attn_o, hf, hb, m, merge, wts, FRONT_ROWS)
    return y.reshape(batch, n_tok, d_model)


def kernel(x_prompt, x_sample, meta, g_ffn1, w1_ffn1, w3_ffn1, w2_ffn1, g_mix, w_in, b_i, b_f, q_gain, k_gain,
           m_gain, w_attn_br, w_mlstm_br, w_out, g_ffn2, w1_ffn2, w3_ffn2, w2_ffn2):
    assert w_in.shape[0] == 1, "single-layer trunk: the meta rows' mixer outputs are never consumed"
    assert meta.shape[0] == N_META
    wts = _prepare_weights(g_ffn1, w1_ffn1, w3_ffn1, w2_ffn1, g_mix, w_in, b_i, b_f, q_gain, k_gain, m_gain,
                           w_attn_br, w_mlstm_br, w_out, g_ffn2, w1_ffn2, w3_ffn2, w2_ffn2)
    ones = jnp.ones((META_ROWS, LANES), F32)
    meta_rows = jnp.pad(meta.astype(F32), ((0, META_ROWS - N_META), (0, 0)))
    _, _, k_meta, v_meta, m_meta, gates_meta, _, norms_meta = _front_call(
        meta_rows, ones, jnp.zeros_like(ones), wts, META_ROWS)
    meta_parts = (k_meta, v_meta, m_meta, gates_meta, norms_meta)
    return (_trunk(x_prompt, meta_parts, wts), _trunk(x_sample, meta_parts, wts))
```

```python
import functools

import jax
import jax.numpy as jnp
import numpy as np
from jax import lax
from jax.experimental import pallas as pl
from jax.experimental.pallas import tpu as pltpu

F32 = jnp.float32
BF16 = jnp.bfloat16

N_META = 16
GRID_W = 64
EPS = 1e-6
NEG = -1e30
N_Q_HEADS = 8
N_KV_HEADS = 2
Q_PER_KV = N_Q_HEADS // N_KV_HEADS
HEAD_DIM = 64
ROPE_AXIS_DIM = HEAD_DIM // 2
ROPE_FREQS = ROPE_AXIS_DIM // 2
ROPE_THETA = 10000.0
M_HEADS = 4
M_HEAD_DIM = 128
M_WIDTH = M_HEADS * M_HEAD_DIM

LANES = 128
VMEM_LIMIT_BYTES = 60 * 1024 * 1024

FRONT_ROWS = 256
ATTN_Q_ROWS = 512
ATTN_K_ROWS = 2048
MLSTM_CHUNK = 256

Q_PAD = N_Q_HEADS * LANES
KV_PAD = N_KV_HEADS * LANES
Q_SCALE = HEAD_DIM ** -0.5 * 1.4426950408889634
ATTN_SAFE_LOG2 = 90.0
ATTN_V_ROWS = 80
META_ROWS = LANES
STATE_ROWS = M_HEAD_DIM + 16


def _const_spec(shape):
    zeros = (0,) * len(shape)
    return pl.BlockSpec(shape, lambda *_: zeros, pipeline_mode=pl.Buffered(1))


def _rmsnorm(x, g):
    return x * lax.rsqrt(jnp.mean(x * x, axis=-1, keepdims=True) + EPS) * g


def _swiglu_half_residual(x, g, w1_ref, w3_ref, w2_ref):
    u = _rmsnorm(x, g).astype(BF16)
    a = jnp.dot(u, w1_ref[...], preferred_element_type=F32)
    b = jnp.dot(u, w3_ref[...], preferred_element_type=F32)
    act = (a * jax.nn.sigmoid(a) * b).astype(BF16)
    return x + 0.5 * jnp.dot(act, w2_ref[...], preferred_element_type=F32)


def _log_sigmoid(x):
    return jnp.minimum(x, 0.0) - jnp.log1p(jnp.exp(-jnp.abs(x)))


def _head_norm_rope(x, gain, cos, sin_signed, first_half):
    ms = jnp.sum(x * x, axis=-1, keepdims=True) * (1.0 / HEAD_DIM)
    y = x * lax.rsqrt(ms + EPS) * gain
    partner = jnp.where(first_half, pltpu.roll(y, LANES - ROPE_FREQS, 1), pltpu.roll(y, ROPE_FREQS, 1))
    return y * cos + partner * sin_signed


def _front_kernel(x_ref, cos_ref, sin_ref, g1_ref, w1_ref, w3_ref, w2_ref, gmix_ref, wq_ref, wk_ref, wv_ref,
                  wm_ref, wg_ref, wmerge_ref, qgain_ref, kgain_ref, gbias_ref,
                  h_ref, q_ref, k_ref, v_ref, mt_ref, m_ref, gate_ref, merge_ref, norm_ref):
    h = _swiglu_half_residual(x_ref[...], g1_ref[...], w1_ref, w3_ref, w2_ref)
    h_ref[...] = h
    u = _rmsnorm(h, gmix_ref[...]).astype(BF16)

    rows = h.shape[0]
    lane = lax.broadcasted_iota(jnp.int32, (rows, LANES), 1)
    first_half = (lane % ROPE_AXIS_DIM) < ROPE_FREQS
    cos = cos_ref[...]
    sin = sin_ref[...]

    def sq_norm(y_bf16):
        y = y_bf16.astype(F32)
        return jnp.sum(y * y, axis=-1, keepdims=True)

    zq = jnp.dot(u, wq_ref[...], preferred_element_type=F32)
    q_sq = jnp.zeros((rows, 1), F32)
    for hd in range(N_Q_HEADS):
        sl = slice(hd * LANES, (hd + 1) * LANES)
        y = _head_norm_rope(zq[:, sl], qgain_ref[...], cos, sin, first_half) * Q_SCALE
        q_ref[sl, :] = y.T.astype(BF16)
        q_sq = jnp.maximum(q_sq, sq_norm(y.astype(BF16)))

    zk = jnp.dot(u, wk_ref[...], preferred_element_type=F32)
    zv = jnp.dot(u, wv_ref[...], preferred_element_type=F32)
    k_sq = jnp.zeros((rows, 1), F32)
    for hd in range(N_KV_HEADS):
        sl = slice(hd * LANES, (hd + 1) * LANES)
        y = _head_norm_rope(zk[:, sl], kgain_ref[...], cos, sin, first_half).astype(BF16)
        k_ref[:, sl] = y
        k_sq = jnp.maximum(k_sq, sq_norm(y))
        v_ref[sl, :] = jnp.where(lane == HEAD_DIM, 1.0, zv[:, sl]).T.astype(BF16)

    tile_row = lax.broadcasted_iota(jnp.int32, norm_ref.shape, 0)
    norm_ref[...] = jnp.where(tile_row == 0, jnp.max(q_sq, axis=0, keepdims=True),
                              jnp.max(k_sq, axis=0, keepdims=True))

    zm = jnp.dot(u, wm_ref[...], preferred_element_type=F32)
    mt_ref[:M_WIDTH, :] = zm[:, :M_WIDTH].T.astype(BF16)
    mt_ref[M_WIDTH:, :] = zm[:, 2 * M_WIDTH:3 * M_WIDTH].T.astype(BF16)
    m_ref[:, :M_WIDTH] = (zm[:, M_WIDTH:2 * M_WIDTH] * (M_HEAD_DIM ** -0.5)).astype(BF16)
    m_ref[:, M_WIDTH:] = zm[:, 3 * M_WIDTH:].astype(BF16)

    zg = jnp.dot(u, wg_ref[...], preferred_element_type=F32) + gbias_ref[...]
    gate_ref[...] = jnp.where(lane < 2 * M_HEADS, zg, _log_sigmoid(zg))

    zmerge = jnp.dot(u, wmerge_ref[...], preferred_element_type=F32)
    merge_ref[...] = jax.nn.sigmoid(zmerge).astype(BF16)


def _front_call(x, cos, sin, wts, rows):
    tokens, d_model = x.shape
    steps = tokens // rows
    table_steps = cos.shape[0] // rows

    def row_spec(width):
        return pl.BlockSpec((rows, width), lambda i: (i, 0))

    table_spec = pl.BlockSpec((rows, LANES), lambda i: (i % table_steps, 0))
    consts = (wts["g1"], wts["w1a"], wts["w3a"], wts["w2a"], wts["gmix"], wts["wq"], wts["wk"], wts["wv"],
              wts["wm"], wts["wg"], wts["wmerge"], wts["qgain"], wts["kgain"], wts["gbias"])

    def col_spec(height):
        return pl.BlockSpec((height, rows), lambda i: (0, i))

    out_specs = [row_spec(d_model), col_spec(Q_PAD), row_spec(KV_PAD), col_spec(KV_PAD), col_spec(2 * M_WIDTH),
                 row_spec(2 * M_WIDTH), row_spec(LANES), row_spec(2 * d_model),
                 pl.BlockSpec((None, 8, LANES), lambda i: (i, 0, 0))]
    out_shape = [jax.ShapeDtypeStruct((tokens, d_model), F32), jax.ShapeDtypeStruct((Q_PAD, tokens), BF16),
                 jax.ShapeDtypeStruct((tokens, KV_PAD), BF16), jax.ShapeDtypeStruct((KV_PAD, tokens), BF16),
                 jax.ShapeDtypeStruct((2 * M_WIDTH, tokens), BF16),
                 jax.ShapeDtypeStruct((tokens, 2 * M_WIDTH), BF16), jax.ShapeDtypeStruct((tokens, LANES), F32),
                 jax.ShapeDtypeStruct((tokens, 2 * d_model), BF16), jax.ShapeDtypeStruct((steps, 8, LANES), F32)]
    return pl.pallas_call(
        _front_kernel,
        grid=(steps,),
        in_specs=[row_spec(d_model), table_spec, table_spec] + [_const_spec(c.shape) for c in consts],
        out_specs=out_specs,
        out_shape=out_shape,
        compiler_params=pltpu.CompilerParams(dimension_semantics=("parallel",),
                                             vmem_limit_bytes=VMEM_LIMIT_BYTES),
        name="front",
    )(x, cos, sin, *consts)


def _attn_kernel(safe_ref, floor_ref, qt_ref, k_ref, vt_ref, kmeta_ref, vtmeta_ref, o_ref, m_scr, acc_scr):
    ki = pl.program_id(2)
    q_cols = qt_ref.shape[1]
    step = (pl.program_id(0) * pl.num_programs(1) + pl.program_id(1)) * pl.num_programs(2) + ki
    safe = safe_ref[step]

    def scores_t(hd, keys_ref):
        g = hd // Q_PER_KV
        return jnp.dot(keys_ref[:, g * LANES:(g + 1) * LANES], qt_ref[hd * LANES:(hd + 1) * LANES, :],
                       preferred_element_type=F32)

    def values_t(hd, values_ref):
        g = hd // Q_PER_KV
        return values_ref[g * LANES:g * LANES + ATTN_V_ROWS, :]

    def meta_scores_t(hd):
        real = lax.broadcasted_iota(jnp.int32, (kmeta_ref.shape[0], q_cols), 0) < N_META
        return jnp.where(real, scores_t(hd, kmeta_ref), NEG)

    @pl.when((ki == 0) & (safe == 1))
    def _seed_from_meta_keys_against_floor():
        floor = jnp.full((1, q_cols), floor_ref[pl.program_id(0) * pl.num_programs(1) + pl.program_id(1)], F32)
        scores = [meta_scores_t(hd) for hd in range(N_Q_HEADS)]
        for hd, s in enumerate(scores):
            p = jnp.exp2(s - floor).astype(BF16)
            acc = jnp.dot(values_t(hd, vtmeta_ref), p, preferred_element_type=F32)
            m = jnp.maximum(floor, jnp.max(s, axis=0, keepdims=True))
            acc_scr[hd] = jnp.exp2(floor - m) * acc
            m_scr[hd] = jnp.broadcast_to(m, m_scr.shape[1:])

    @pl.when((ki == 0) & (safe != 1))
    def _seed_from_meta_keys():
        for hd in range(N_Q_HEADS):
            s = meta_scores_t(hd)
            m = jnp.max(s, axis=0, keepdims=True)
            p = jnp.exp2(s - m).astype(BF16)
            acc_scr[hd] = jnp.dot(values_t(hd, vtmeta_ref), p, preferred_element_type=F32)
            m_scr[hd] = jnp.broadcast_to(m, m_scr.shape[1:])

    @pl.when(safe == 1)
    def _block_against_previous_maximum():
        s_next = scores_t(0, k_ref)
        for hd in range(N_Q_HEADS):
            m_old = m_scr[hd][:1]
            s = s_next
            if hd + 1 < N_Q_HEADS:
                s_next = scores_t(hd + 1, k_ref)
            p = jnp.exp2(s - m_old).astype(BF16)
            pv = jnp.dot(values_t(hd, vt_ref), p, preferred_element_type=F32)
            m_new = jnp.maximum(m_old, jnp.max(s, axis=0, keepdims=True))
            acc_scr[hd] = jnp.exp2(m_old - m_new) * (acc_scr[hd] + pv)
            m_scr[hd] = jnp.broadcast_to(m_new, m_scr.shape[1:])

    @pl.when(safe != 1)
    def _block_against_own_maximum():
        for hd in range(N_Q_HEADS):
            m_old = m_scr[hd][:1]
            s = scores_t(hd, k_ref)
            m_new = jnp.maximum(m_old, jnp.max(s, axis=0, keepdims=True))
            p = jnp.exp2(s - m_new).astype(BF16)
            pv = jnp.dot(values_t(hd, vt_ref), p, preferred_element_type=F32)
            acc_scr[hd] = jnp.exp2(m_old - m_new) * acc_scr[hd] + pv
            m_scr[hd] = jnp.broadcast_to(m_new, m_scr.shape[1:])

    @pl.when(ki == pl.num_programs(2) - 1)
    def _normalise():
        for hd in range(N_Q_HEADS):
            acc = acc_scr[hd]
            out_t = acc / acc[HEAD_DIM:HEAD_DIM + 1]
            out_t = jnp.concatenate([out_t, jnp.zeros((LANES - ATTN_V_ROWS, q_cols), F32)], axis=0)
            o_ref[:, hd * LANES:(hd + 1) * LANES] = out_t.T.astype(BF16)


def _attn_safe_blocks(norms, norms_meta, batch, nq, nk):
    def block_max(sq, blocks):
        return jnp.sqrt(jnp.max(sq.reshape(blocks, -1), axis=1))

    q_norm = block_max(norms[:, 0, 0], batch * nq).reshape(batch, nq, 1)
    k_norm = block_max(norms[:, 1, 0], batch * nk).reshape(batch, 1, nk)
    k_meta_norm = jnp.sqrt(jnp.max(norms_meta[:, 1, 0]))
    gap = q_norm * (jnp.maximum(k_norm, k_meta_norm) + k_meta_norm)
    safe = (gap <= ATTN_SAFE_LOG2).astype(jnp.int32).reshape(-1)
    floor = (-q_norm * k_meta_norm).reshape(-1)
    return safe, floor


def _attn_call(qt, k, vt, k_meta, vt_meta, safe, floor, batch, q_rows, k_rows):
    tokens = k.shape[0]
    n_tok = tokens // batch
    nq = n_tok // q_rows
    nk = n_tok // k_rows
    grid_spec = pltpu.PrefetchScalarGridSpec(
        num_scalar_prefetch=2,
        grid=(batch, nq, nk),
        in_specs=[
            pl.BlockSpec((Q_PAD, q_rows), lambda b, i, j, *_: (0, b * nq + i)),
            pl.BlockSpec((k_rows, KV_PAD), lambda b, i, j, *_: (b * nk + j, 0)),
            pl.BlockSpec((KV_PAD, k_rows), lambda b, i, j, *_: (0, b * nk + j)),
            pl.BlockSpec(k_meta.shape, lambda b, i, j, *_: (0, 0)),
            pl.BlockSpec(vt_meta.shape, lambda b, i, j, *_: (0, 0)),
        ],
        out_specs=pl.BlockSpec((q_rows, Q_PAD), lambda b, i, j, *_: (b * nq + i, 0)),
        scratch_shapes=[pltpu.VMEM((N_Q_HEADS, 8, q_rows), F32),
                        pltpu.VMEM((N_Q_HEADS, ATTN_V_ROWS, q_rows), F32)],
    )
    return pl.pallas_call(
        _attn_kernel,
        grid_spec=grid_spec,
        out_shape=jax.ShapeDtypeStruct((tokens, Q_PAD), BF16),
        compiler_params=pltpu.CompilerParams(dimension_semantics=("parallel", "parallel", "arbitrary"),
                                             vmem_limit_bytes=VMEM_LIMIT_BYTES),
        name="attn",
    )(safe, floor, qt, k, vt, k_meta, vt_meta)


def _split_hi_lo(x):
    hi = x.astype(BF16)
    lo = (x - hi.astype(F32)).astype(BF16)
    return hi, lo


def _gate_cumsum(gates, causal):
    t = gates.shape[0]
    row = lax.broadcasted_iota(jnp.int32, (t, t), 0)
    col = lax.broadcasted_iota(jnp.int32, (t, t), 1)
    mask = (col <= row) if causal else (col >= row)
    tri = jnp.where(mask, 1.0, 0.0).astype(BF16)
    hi, lo = _split_hi_lo(gates)
    sums = jnp.dot(tri, hi, preferred_element_type=F32) + jnp.dot(tri, lo, preferred_element_type=F32)
    return sums, mask


def _values_t_with_ones(vt):
    row = lax.broadcasted_iota(jnp.int32, (STATE_ROWS - M_HEAD_DIM, vt.shape[1]), 0)
    return jnp.concatenate([vt, jnp.where(row == 0, 1.0, 0.0).astype(vt.dtype)], axis=0)


def _state_update(state_t, m_in, k, vaug_t, li_row, b_row, b_tot):
    w_end = b_tot - b_row + li_row
    m_new = jnp.maximum(b_tot + m_in, jnp.max(w_end, axis=1, keepdims=True))
    vw = (vaug_t.astype(F32) * jnp.exp(w_end - m_new)).astype(BF16)
    return jnp.exp(b_tot + m_in - m_new) * state_t + jnp.dot(vw, k, preferred_element_type=F32), m_new


def _mlstm_kernel(qtf_ref, vtf_ref, kf_ref, gf_ref, qtb_ref, vtb_ref, kb_ref, gb_ref, kmeta_ref, vtmeta_ref,
                  gmeta_ref, hf_ref, hb_ref, state_scr, m_scr):
    c = pl.program_id(1)

    @pl.when(c == 0)
    def _reset_and_absorb_meta():
        state_scr[...] = jnp.zeros_like(state_scr)
        m_scr[...] = jnp.zeros_like(m_scr)
        gates = gmeta_ref[...]
        row = lax.broadcasted_iota(jnp.int32, gates.shape, 0)
        lane = lax.broadcasted_iota(jnp.int32, gates.shape, 1)
        gates = jnp.where(row < N_META, gates, jnp.where(lane < 2 * M_HEADS, NEG, 0.0))
        sums, _ = _gate_cumsum(gates, causal=True)
        gates_t = gates.T
        sums_t = sums.T
        last = gates.shape[0] - 1
        for hd in range(M_HEADS):
            f_lane = 2 * M_HEADS + hd
            sl = slice(hd * M_HEAD_DIM, (hd + 1) * M_HEAD_DIM)
            vt = vtmeta_ref[M_WIDTH + hd * M_HEAD_DIM:M_WIDTH + (hd + 1) * M_HEAD_DIM, :]
            new_state, m_new = _state_update(
                state_scr[hd], m_scr[hd][:1, :1], kmeta_ref[:, sl], _values_t_with_ones(vt),
                gates_t[hd:hd + 1, :], sums_t[f_lane:f_lane + 1, :], sums[last:last + 1, f_lane:f_lane + 1])
            state_scr[hd] = new_state
            m_scr[hd] = jnp.broadcast_to(m_new, m_scr.shape[1:])

    chains = []
    for causal, qt_ref, vt_ref, k_ref, g_ref, o_ref in ((True, qtf_ref, vtf_ref, kf_ref, gf_ref, hf_ref),
                                                        (False, qtb_ref, vtb_ref, kb_ref, gb_ref, hb_ref)):
        gates = g_ref[...]
        t = gates.shape[0]
        sums, _ = _gate_cumsum(gates, causal)
        gates_t = gates.T
        sums_t = sums.T
        src = lax.broadcasted_iota(jnp.int32, (t, t), 0)
        tgt = lax.broadcasted_iota(jnp.int32, (t, t), 1)
        feeds = (src <= tgt) if causal else (src >= tgt)
        last = t - 1 if causal else 0
        for hd in range(M_HEADS):
            chain = hd if causal else M_HEADS + hd
            i_lane, f_lane = chain, 2 * M_HEADS + chain
            chains.append(dict(
                chain=chain, feeds=feeds, o_ref=o_ref, qt_ref=qt_ref, vt_ref=vt_ref, k_ref=k_ref,
                sl=slice(hd * M_HEAD_DIM, (hd + 1) * M_HEAD_DIM), t=t,
                r_col=gates[:, i_lane:i_lane + 1] - sums[:, f_lane:f_lane + 1],
                li_row=gates_t[i_lane:i_lane + 1, :], b_row=sums_t[f_lane:f_lane + 1, :],
                b_tot=sums[last:last + 1, f_lane:f_lane + 1], m_in=m_scr[chain][:1, :1]))

    for ch in chains:
        lhs = jnp.concatenate([ch["k_ref"][:, ch["sl"]], state_scr[ch["chain"]].astype(BF16)], axis=0)
        both = jnp.dot(lhs, ch["qt_ref"][ch["sl"], :], preferred_element_type=F32)
        ch["kq"], ch["inter"] = both[:ch["t"]], both[ch["t"]:]
    for ch in chains:
        ch["r"] = jnp.where(ch["feeds"], ch["r_col"], NEG)
        ch["g"] = jnp.maximum(jnp.max(ch["r"], axis=0, keepdims=True), ch["m_in"])
    for ch in chains:
        ch["s"] = (ch["kq"] * jnp.exp(ch["r"] - ch["g"])).astype(BF16)
        ch["vaug"] = _values_t_with_ones(ch["vt_ref"][ch["sl"], :])
        w_end = ch["b_tot"] - ch["b_row"] + ch["li_row"]
        ch["m_new"] = jnp.maximum(ch["b_tot"] + ch["m_in"], jnp.max(w_end, axis=1, keepdims=True))
        ch["vw"] = (ch["vaug"].astype(F32) * jnp.exp(w_end - ch["m_new"])).astype(BF16)
    for ch in chains:
        ch["tot"] = jnp.dot(ch["vaug"], ch["s"], preferred_element_type=F32)
        ch["local"] = jnp.dot(ch["vw"], ch["k_ref"][:, ch["sl"]], preferred_element_type=F32)
    for ch in chains:
        tot = ch["tot"] + jnp.exp(ch["m_in"] - ch["g"]) * ch["inter"]
        den = jnp.maximum(jnp.abs(tot[M_HEAD_DIM:M_HEAD_DIM + 1]), jnp.exp(-(ch["b_row"] + ch["g"])))
        ch["o_ref"][:, ch["sl"]] = (tot[:M_HEAD_DIM] / den).T
        chain = ch["chain"]
        state_scr[chain] = jnp.exp(ch["b_tot"] + ch["m_in"] - ch["m_new"]) * state_scr[chain] + ch["local"]
        m_scr[chain] = jnp.broadcast_to(ch["m_new"], m_scr.shape[1:])


def _mlstm_call(mt, m, gates, mt_meta, m_meta, gates_meta, batch, chunk):
    tokens = m.shape[0]
    nc = tokens // batch // chunk

    def chunk_of(direction):
        return (lambda b, c: b * nc + c) if direction == "fwd" else (lambda b, c: b * nc + nc - 1 - c)

    def specs(direction):
        at = chunk_of(direction)
        return [pl.BlockSpec((M_WIDTH, chunk), lambda b, c: (0, at(b, c))),
                pl.BlockSpec((M_WIDTH, chunk), lambda b, c: (1, at(b, c))),
                pl.BlockSpec((chunk, M_WIDTH), lambda b, c: (at(b, c), 0)),
                pl.BlockSpec((chunk, LANES), lambda b, c: (at(b, c), 0))]

    def out_spec(direction):
        at = chunk_of(direction)
        return pl.BlockSpec((chunk, M_WIDTH), lambda b, c: (at(b, c), 0))

    return pl.pallas_call(
        _mlstm_kernel,
        grid=(batch, nc),
        in_specs=specs("fwd") + specs("bwd") + [pl.BlockSpec(m_meta.shape, lambda b, c: (0, 0)),
                                                pl.BlockSpec(mt_meta.shape, lambda b, c: (0, 0)),
                                                pl.BlockSpec(gates_meta.shape, lambda b, c: (0, 0))],
        out_specs=[out_spec("fwd"), out_spec("bwd")],
        out_shape=[jax.ShapeDtypeStruct((tokens, M_WIDTH), F32)] * 2,
        scratch_shapes=[pltpu.VMEM((2 * M_HEADS, STATE_ROWS, M_HEAD_DIM), F32),
                        pltpu.VMEM((2 * M_HEADS, 8, LANES), F32)],
        compiler_params=pltpu.CompilerParams(dimension_semantics=("parallel", "arbitrary"),
                                             vmem_limit_bytes=VMEM_LIMIT_BYTES),
        name="mlstm",
    )(mt, mt, m, gates, mt, mt, m, gates, m_meta, mt_meta, gates_meta)


def _back_kernel(h_ref, ao_ref, hf_ref, hb_ref, mo_ref, merge_ref, mgain_ref, wab_ref, wmb_ref, wout_ref,
                 g2_ref, w1_ref, w3_ref, w2_ref, y_ref):
    d_model = h_ref.shape[1]
    a_out = jnp.dot(ao_ref[...], wab_ref[...], preferred_element_type=F32)

    hm = hf_ref[...] + hb_ref[...]
    o_gate = jax.nn.sigmoid(mo_ref[...].astype(F32))
    gated = []
    for hd in range(M_HEADS):
        sl = slice(hd * M_HEAD_DIM, (hd + 1) * M_HEAD_DIM)
        gated.append((_rmsnorm(hm[:, sl], mgain_ref[:, sl]) * o_gate[:, sl]).astype(BF16))
    m_out = jnp.dot(jnp.concatenate(gated, axis=1), wmb_ref[...], preferred_element_type=F32)

    merged = (merge_ref[:, :d_model].astype(F32) * a_out + merge_ref[:, d_model:].astype(F32) * m_out)
    h2 = h_ref[...] + jnp.dot(merged.astype(BF16), wout_ref[...], preferred_element_type=F32)
    y_ref[...] = _swiglu_half_residual(h2, g2_ref[...], w1_ref, w3_ref, w2_ref)


def _back_call(h, attn_o, hf, hb, m, merge, wts, rows):
    tokens, d_model = h.shape

    def row_spec(width, col=0):
        return pl.BlockSpec((rows, width), lambda i: (i, col))

    consts = (wts["mgain"], wts["wab"], wts["wmb"], wts["wout"], wts["g2"], wts["w1b"], wts["w3b"], wts["w2b"])
    return pl.pallas_call(
        _back_kernel,
        grid=(tokens // rows,),
        in_specs=[row_spec(d_model), row_spec(Q_PAD), row_spec(M_WIDTH), row_spec(M_WIDTH),
                  row_spec(M_WIDTH, 1), row_spec(2 * d_model)] + [_const_spec(c.shape) for c in consts],
        out_specs=row_spec(d_model),
        out_shape=jax.ShapeDtypeStruct((tokens, d_model), F32),
        compiler_params=pltpu.CompilerParams(dimension_semantics=("parallel",),
                                             vmem_limit_bytes=VMEM_LIMIT_BYTES),
        name="back",
    )(h, attn_o, hf, hb, m, merge, *consts)


def _pad_heads(w, n_heads):
    lead = w.shape[:-1]
    w = w.reshape(*lead, n_heads, HEAD_DIM)
    w = jnp.pad(w, [(0, 0)] * len(lead) + [(0, 0), (0, LANES - HEAD_DIM)])
    return w.reshape(*lead, n_heads * LANES)


def _prepare_weights(g_ffn1, w1_ffn1, w3_ffn1, w2_ffn1, g_mix, w_in, b_i, b_f, q_gain, k_gain, m_gain,
                     w_attn_br, w_mlstm_br, w_out, g_ffn2, w1_ffn2, w3_ffn2, w2_ffn2):
    d_model = w_in.shape[1]
    attn_w = N_Q_HEADS * HEAD_DIM
    kv_w = N_KV_HEADS * HEAD_DIM
    splits = (attn_w, kv_w, kv_w, M_WIDTH, M_WIDTH, M_WIDTH, M_WIDTH, 2 * M_HEADS, 2 * M_HEADS, d_model, d_model)
    offs = np.cumsum((0,) + splits)
    assert offs[-1] == w_in.shape[2]
    cols = [w_in[0, :, offs[i]:offs[i + 1]] for i in range(len(splits))]
    aq, ak, av, mq, mk, mv, mo, gi, gf, ga, gm = cols
    gate_w = jnp.pad(jnp.concatenate([gi, gf], axis=1), ((0, 0), (0, LANES - 4 * M_HEADS)))
    gate_b = jnp.pad(jnp.concatenate([b_i[0], b_f[0]]), (0, LANES - 4 * M_HEADS))[None]
    row = lambda g: g.astype(F32)[None]
    pad_gain = lambda g: jnp.pad(g[0].astype(F32), (0, LANES - HEAD_DIM))[None]
    wab = _pad_heads(w_attn_br[0].T, N_Q_HEADS).T
    return {
        "g1": row(g_ffn1[0]), "w1a": w1_ffn1[0].astype(BF16), "w3a": w3_ffn1[0].astype(BF16),
        "w2a": w2_ffn1[0].astype(BF16),
        "gmix": row(g_mix[0]),
        "wq": _pad_heads(aq, N_Q_HEADS).astype(BF16), "wk": _pad_heads(ak, N_KV_HEADS).astype(BF16),
        "wv": _pad_heads(av, N_KV_HEADS).astype(BF16),
        "wm": jnp.concatenate([mq, mk, mv, mo], axis=1).astype(BF16),
        "wg": gate_w.astype(BF16), "gbias": gate_b.astype(F32),
        "wmerge": jnp.concatenate([ga, gm], axis=1).astype(BF16),
        "qgain": pad_gain(q_gain), "kgain": pad_gain(k_gain),
        "mgain": m_gain[0].astype(F32).reshape(1, M_WIDTH),
        "wab": wab.astype(BF16), "wmb": w_mlstm_br[0].astype(BF16), "wout": w_out[0].astype(BF16),
        "g2": row(g_ffn2[0]), "w1b": w1_ffn2[0].astype(BF16), "w3b": w3_ffn2[0].astype(BF16),
        "w2b": w2_ffn2[0].astype(BF16),
    }


def _rope_tables(n_tok):
    t = jnp.arange(n_tok)
    pos = jnp.stack([(t // GRID_W).astype(F32), (t % GRID_W).astype(F32)], axis=-1)
    inv_freq = ROPE_THETA ** (-2.0 * jnp.arange(ROPE_FREQS, dtype=F32) / ROPE_AXIS_DIM)
    ang = pos[:, :, None] * inv_freq
    cos, sin = jnp.cos(ang), jnp.sin(ang)
    cos = jnp.stack([cos, cos], axis=2).reshape(n_tok, HEAD_DIM)
    sin = jnp.stack([-sin, sin], axis=2).reshape(n_tok, HEAD_DIM)
    cos = jnp.pad(cos, ((0, 0), (0, LANES - HEAD_DIM)), constant_values=1.0)
    sin = jnp.pad(sin, ((0, 0), (0, LANES - HEAD_DIM)))
    return cos, sin


def _trunk(x, meta_parts, wts):
    batch, n_tok, d_model = x.shape
    k_meta, v_meta, mt_meta, m_meta, gates_meta, norms_meta = meta_parts
    cos, sin = _rope_tables(n_tok)
    h, q, k, v, mt, m, gates, merge, norms = _front_call(
        x.reshape(batch * n_tok, d_model), cos, sin, wts, FRONT_ROWS)
    q_rows, k_rows = min(ATTN_Q_ROWS, n_tok), min(ATTN_K_ROWS, n_tok)
    safe, floor = _attn_safe_blocks(norms, norms_meta, batch, n_tok // q_rows, n_tok // k_rows)
    attn_o = _attn_call(q, k, v, k_meta, v_meta, safe, floor, batch, q_rows, k_rows)
    hf, hb = _mlstm_call(mt, m, gates, mt_meta, m_meta, gates_meta, batch, min(MLSTM_CHUNK, n_tok))
    y = _back_call(h, attn_o, hf, hb, m, merge, wts, FRONT_ROWS)
    return y.reshape(batch, n_tok, d_model)


def kernel(x_prompt, x_sample, meta, g_ffn1, w1_ffn1, w3_ffn1, w2_ffn1, g_mix, w_in, b_i, b_f, q_gain, k_gain,
           m_gain, w_attn_br, w_mlstm_br, w_out, g_ffn2, w1_ffn2, w3_ffn2, w2_ffn2):
    assert w_in.shape[0] == 1, "single-layer trunk: the meta rows' mixer outputs are never consumed"
    assert meta.shape[0] == N_META
    wts = _prepare_weights(g_ffn1, w1_ffn1, w3_ffn1, w2_ffn1, g_mix, w_in, b_i, b_f, q_gain, k_gain, m_gain,
                           w_attn_br, w_mlstm_br, w_out, g_ffn2, w1_ffn2, w3_ffn2, w2_ffn2)
    ones = jnp.ones((META_ROWS, LANES), F32)
    meta_rows = jnp.pad(meta.astype(F32), ((0, META_ROWS - N_META), (0, 0)))
    _, _, k_meta, v_meta, mt_meta, m_meta, gates_meta, _, norms_meta = _front_call(
        meta_rows, ones, jnp.zeros_like(ones), wts, META_ROWS)
    meta_parts = (k_meta, v_meta, mt_meta, m_meta, gates_meta, norms_meta)
    return (_trunk(x_prompt, meta_parts, wts), _trunk(x_sample, meta_parts, wts))
```

```python
import functools

import jax
import jax.numpy as jnp
import numpy as np
from jax import lax
from jax.experimental import pallas as pl
from jax.experimental.pallas import tpu as pltpu

F32 = jnp.float32
BF16 = jnp.bfloat16

N_META = 16
GRID_W = 64
EPS = 1e-6
NEG = -1e30
N_Q_HEADS = 8
N_KV_HEADS = 2
Q_PER_KV = N_Q_HEADS // N_KV_HEADS
HEAD_DIM = 64
ROPE_AXIS_DIM = HEAD_DIM // 2
ROPE_FREQS = ROPE_AXIS_DIM // 2
ROPE_THETA = 10000.0
M_HEADS = 4
M_HEAD_DIM = 128
M_WIDTH = M_HEADS * M_HEAD_DIM

LANES = 128
VMEM_LIMIT_BYTES = 60 * 1024 * 1024

FRONT_ROWS = 256
ATTN_Q_ROWS = 512
ATTN_K_ROWS = 2048
ATTN_SUB_ROWS = 512
MLSTM_CHUNK = 256

Q_PAD = N_Q_HEADS * LANES
ATTN_WIDTH = N_Q_HEADS * HEAD_DIM
KV_WIDTH = N_KV_HEADS * HEAD_DIM
assert KV_WIDTH == LANES and HEAD_DIM * 2 == LANES
Q_SCALE = HEAD_DIM ** -0.5 * 1.4426950408889634
ATTN_SAFE_LOG2 = 90.0
ATTN_V_ROWS = 80
META_ROWS = LANES
STATE_ROWS = M_HEAD_DIM + 16


def _const_spec(shape):
    zeros = (0,) * len(shape)
    return pl.BlockSpec(shape, lambda *_: zeros, pipeline_mode=pl.Buffered(1))


def _rmsnorm(x, g):
    return x * lax.rsqrt(jnp.mean(x * x, axis=-1, keepdims=True) + EPS) * g


def _swiglu_half_residual(x, g, w1_ref, w3_ref, w2_ref):
    u = _rmsnorm(x, g).astype(BF16)
    a = jnp.dot(u, w1_ref[...], preferred_element_type=F32)
    b = jnp.dot(u, w3_ref[...], preferred_element_type=F32)
    act = (a * jax.nn.sigmoid(a) * b).astype(BF16)
    return x + 0.5 * jnp.dot(act, w2_ref[...], preferred_element_type=F32)


def _log_sigmoid(x):
    return jnp.minimum(x, 0.0) - jnp.log1p(jnp.exp(-jnp.abs(x)))


def _half_sums(x, low_half):
    low = jnp.sum(jnp.where(low_half, x, 0.0), axis=-1, keepdims=True)
    high = jnp.sum(jnp.where(low_half, 0.0, x), axis=-1, keepdims=True)
    return low, high


def _head_pair_norm_rope(x, gain, cos, sin_signed, low_half, first_half):
    ms_low, ms_high = _half_sums(x * x, low_half)
    inv = jnp.where(low_half, lax.rsqrt(ms_low * (1.0 / HEAD_DIM) + EPS), lax.rsqrt(ms_high * (1.0 / HEAD_DIM) + EPS))
    y = x * inv * gain
    partner = jnp.where(first_half, pltpu.roll(y, LANES - ROPE_FREQS, 1), pltpu.roll(y, ROPE_FREQS, 1))
    return y * cos + partner * sin_signed


def _front_kernel(x_ref, cos_ref, sin_ref, g1_ref, w1_ref, w3_ref, w2_ref, gmix_ref, wq_ref, wkv_ref,
                  wm_ref, wg_ref, wmerge_ref, qgain_ref, kgain_ref, gbias_ref,
                  h_ref, q_ref, k_ref, v_ref, mt_ref, m_ref, gate_ref, merge_ref, norm_ref):
    h = _swiglu_half_residual(x_ref[...], g1_ref[...], w1_ref, w3_ref, w2_ref)
    h_ref[...] = h
    u = _rmsnorm(h, gmix_ref[...]).astype(BF16)

    rows = h.shape[0]
    lane = lax.broadcasted_iota(jnp.int32, (rows, LANES), 1)
    low_half = lane < HEAD_DIM
    first_half = (lane % ROPE_AXIS_DIM) < ROPE_FREQS
    cos = cos_ref[...]
    sin = sin_ref[...]

    def sq_norms(y_bf16):
        y = y_bf16.astype(F32)
        return jnp.maximum(*_half_sums(y * y, low_half))

    zq = jnp.dot(u, wq_ref[...], preferred_element_type=F32)
    q_sq = jnp.zeros((rows, 1), F32)
    for pair in range(Q_PER_KV):
        y = _head_pair_norm_rope(zq[:, pair * LANES:(pair + 1) * LANES], qgain_ref[...], cos, sin,
                                 low_half, first_half) * Q_SCALE
        q_sq = jnp.maximum(q_sq, sq_norms(y.astype(BF16)))
        for group, keep in enumerate((low_half, ~low_half)):
            hd = group * Q_PER_KV + pair
            q_ref[hd * LANES:(hd + 1) * LANES, :] = jnp.where(keep, y, 0.0).T.astype(BF16)

    zkv = jnp.dot(u, wkv_ref[...], preferred_element_type=F32)
    k = _head_pair_norm_rope(zkv[:, :LANES], kgain_ref[...], cos, sin, low_half, first_half).astype(BF16)
    k_ref[...] = k
    k_sq = sq_norms(k)
    v_ref[...] = zkv[:, LANES:].T.astype(BF16)

    tile_row = lax.broadcasted_iota(jnp.int32, norm_ref.shape, 0)
    norm_ref[...] = jnp.where(tile_row == 0, jnp.max(q_sq, axis=0, keepdims=True),
                              jnp.max(k_sq, axis=0, keepdims=True))

    zm = jnp.dot(u, wm_ref[...], preferred_element_type=F32)
    mt_ref[:M_WIDTH, :] = zm[:, :M_WIDTH].T.astype(BF16)
    mt_ref[M_WIDTH:, :] = zm[:, 2 * M_WIDTH:3 * M_WIDTH].T.astype(BF16)
    m_ref[:, :M_WIDTH] = (zm[:, M_WIDTH:2 * M_WIDTH] * (M_HEAD_DIM ** -0.5)).astype(BF16)
    m_ref[:, M_WIDTH:] = zm[:, 3 * M_WIDTH:].astype(BF16)

    zg = jnp.dot(u, wg_ref[...], preferred_element_type=F32) + gbias_ref[...]
    gate_ref[...] = jnp.where(lane < 2 * M_HEADS, zg, _log_sigmoid(zg))

    zmerge = jnp.dot(u, wmerge_ref[...], preferred_element_type=F32)
    merge_ref[...] = jax.nn.sigmoid(zmerge).astype(BF16)


def _front_call(x, cos, sin, wts, rows):
    tokens, d_model = x.shape
    steps = tokens // rows
    table_steps = cos.shape[0] // rows

    def row_spec(width):
        return pl.BlockSpec((rows, width), lambda i: (i, 0))

    table_spec = pl.BlockSpec((rows, LANES), lambda i: (i % table_steps, 0))
    consts = (wts["g1"], wts["w1a"], wts["w3a"], wts["w2a"], wts["gmix"], wts["wq"], wts["wkv"],
              wts["wm"], wts["wg"], wts["wmerge"], wts["qgain"], wts["kgain"], wts["gbias"])

    def col_spec(height):
        return pl.BlockSpec((height, rows), lambda i: (0, i))

    out_specs = [row_spec(d_model), col_spec(Q_PAD), row_spec(KV_WIDTH), col_spec(KV_WIDTH), col_spec(2 * M_WIDTH),
                 row_spec(2 * M_WIDTH), row_spec(LANES), row_spec(2 * d_model),
                 pl.BlockSpec((None, 8, LANES), lambda i: (i, 0, 0))]
    out_shape = [jax.ShapeDtypeStruct((tokens, d_model), F32), jax.ShapeDtypeStruct((Q_PAD, tokens), BF16),
                 jax.ShapeDtypeStruct((tokens, KV_WIDTH), BF16), jax.ShapeDtypeStruct((KV_WIDTH, tokens), BF16),
                 jax.ShapeDtypeStruct((2 * M_WIDTH, tokens), BF16),
                 jax.ShapeDtypeStruct((tokens, 2 * M_WIDTH), BF16), jax.ShapeDtypeStruct((tokens, LANES), F32),
                 jax.ShapeDtypeStruct((tokens, 2 * d_model), BF16), jax.ShapeDtypeStruct((steps, 8, LANES), F32)]
    return pl.pallas_call(
        _front_kernel,
        grid=(steps,),
        in_specs=[row_spec(d_model), table_spec, table_spec] + [_const_spec(c.shape) for c in consts],
        out_specs=out_specs,
        out_shape=out_shape,
        compiler_params=pltpu.CompilerParams(dimension_semantics=("parallel",),
                                             vmem_limit_bytes=VMEM_LIMIT_BYTES),
        name="front",
    )(x, cos, sin, *consts)


def _attn_kernel(safe_ref, floor_ref, qt_ref, k_ref, vt_ref, kmeta_ref, vtmeta_ref, o_ref, m_scr, acc_scr):
    ki = pl.program_id(2)
    q_cols = qt_ref.shape[1]
    step = (pl.program_id(0) * pl.num_programs(1) + pl.program_id(1)) * pl.num_programs(2) + ki
    safe = safe_ref[step]

    def scores_t(hd, keys_ref):
        return jnp.dot(keys_ref[...], qt_ref[hd * LANES:(hd + 1) * LANES, :], preferred_element_type=F32)

    def values_t(hd, values_ref):
        g = hd // Q_PER_KV
        row = lax.broadcasted_iota(jnp.int32, (ATTN_V_ROWS - HEAD_DIM, values_ref.shape[1]), 0)
        ones = jnp.where(row == 0, 1.0, 0.0).astype(BF16)
        return jnp.concatenate([values_ref[g * HEAD_DIM:(g + 1) * HEAD_DIM, :], ones], axis=0)

    def meta_scores_t(hd):
        real = lax.broadcasted_iota(jnp.int32, (kmeta_ref.shape[0], q_cols), 0) < N_META
        return jnp.where(real, scores_t(hd, kmeta_ref), NEG)

    @pl.when((ki == 0) & (safe == 1))
    def _seed_from_meta_keys_against_floor():
        floor = jnp.full((1, q_cols), floor_ref[pl.program_id(0) * pl.num_programs(1) + pl.program_id(1)], F32)
        scores = [meta_scores_t(hd) for hd in range(N_Q_HEADS)]
        for hd, s in enumerate(scores):
            p = jnp.exp2(s - floor).astype(BF16)
            acc = jnp.dot(values_t(hd, vtmeta_ref), p, preferred_element_type=F32)
            m = jnp.maximum(floor, jnp.max(s, axis=0, keepdims=True))
            acc_scr[hd] = jnp.exp2(floor - m) * acc
            m_scr[hd] = jnp.broadcast_to(m, m_scr.shape[1:])

    @pl.when((ki == 0) & (safe != 1))
    def _seed_from_meta_keys():
        for hd in range(N_Q_HEADS):
            s = meta_scores_t(hd)
            m = jnp.max(s, axis=0, keepdims=True)
            p = jnp.exp2(s - m).astype(BF16)
            acc_scr[hd] = jnp.dot(values_t(hd, vtmeta_ref), p, preferred_element_type=F32)
            m_scr[hd] = jnp.broadcast_to(m, m_scr.shape[1:])

    @pl.when(safe == 1)
    def _block_against_previous_maximum():
        n_sub = k_ref.shape[0] // ATTN_SUB_ROWS

        def sub_scores(hd, i):
            return jnp.dot(k_ref[i * ATTN_SUB_ROWS:(i + 1) * ATTN_SUB_ROWS, :],
                           qt_ref[hd * LANES:(hd + 1) * LANES, :], preferred_element_type=F32)

        s_next = [sub_scores(0, i) for i in range(n_sub)]
        for hd in range(N_Q_HEADS):
            m_old = m_scr[hd][:1]
            s_cur, s_next = s_next, []
            vt = values_t(hd, vt_ref)
            pv = None
            m_new = m_old
            for i in range(n_sub):
                if hd + 1 < N_Q_HEADS:
                    s_next.append(sub_scores(hd + 1, i))
                p = jnp.exp2(s_cur[i] - m_old).astype(BF16)
                part = jnp.dot(vt[:, i * ATTN_SUB_ROWS:(i + 1) * ATTN_SUB_ROWS], p, preferred_element_type=F32)
                pv = part if pv is None else pv + part
                m_new = jnp.maximum(m_new, jnp.max(s_cur[i], axis=0, keepdims=True))
            acc_scr[hd] = jnp.exp2(m_old - m_new) * (acc_scr[hd] + pv)
            m_scr[hd] = jnp.broadcast_to(m_new, m_scr.shape[1:])

    @pl.when(safe != 1)
    def _block_against_own_maximum():
        for hd in range(N_Q_HEADS):
            m_old = m_scr[hd][:1]
            s = scores_t(hd, k_ref)
            m_new = jnp.maximum(m_old, jnp.max(s, axis=0, keepdims=True))
            p = jnp.exp2(s - m_new).astype(BF16)
            pv = jnp.dot(values_t(hd, vt_ref), p, preferred_element_type=F32)
            acc_scr[hd] = jnp.exp2(m_old - m_new) * acc_scr[hd] + pv
            m_scr[hd] = jnp.broadcast_to(m_new, m_scr.shape[1:])

    @pl.when(ki == pl.num_programs(2) - 1)
    def _normalise():
        def normalised(hd):
            acc = acc_scr[hd]
            return acc[:HEAD_DIM] / acc[HEAD_DIM:HEAD_DIM + 1]

        for pair in range(Q_PER_KV):
            both_t = jnp.concatenate([normalised(pair), normalised(Q_PER_KV + pair)], axis=0)
            o_ref[:, pair * LANES:(pair + 1) * LANES] = both_t.T.astype(BF16)


def _attn_safe_blocks(norms, norms_meta, batch, nq, nk):
    def block_max(sq, blocks):
        return jnp.sqrt(jnp.max(sq.reshape(blocks, -1), axis=1))

    q_norm = block_max(norms[:, 0, 0], batch * nq).reshape(batch, nq, 1)
    k_norm = block_max(norms[:, 1, 0], batch * nk).reshape(batch, 1, nk)
    k_meta_norm = jnp.sqrt(jnp.max(norms_meta[:, 1, 0]))
    gap = q_norm * (jnp.maximum(k_norm, k_meta_norm) + k_meta_norm)
    safe = (gap <= ATTN_SAFE_LOG2).astype(jnp.int32).reshape(-1)
    floor = (-q_norm * k_meta_norm).reshape(-1)
    return safe, floor


def _attn_call(qt, k, vt, k_meta, vt_meta, safe, floor, batch, q_rows, k_rows):
    tokens = k.shape[0]
    n_tok = tokens // batch
    nq = n_tok // q_rows
    nk = n_tok // k_rows
    grid_spec = pltpu.PrefetchScalarGridSpec(
        num_scalar_prefetch=2,
        grid=(batch, nq, nk),
        in_specs=[
            pl.BlockSpec((Q_PAD, q_rows), lambda b, i, j, *_: (0, b * nq + i)),
            pl.BlockSpec((k_rows, KV_WIDTH), lambda b, i, j, *_: (b * nk + j, 0)),
            pl.BlockSpec((KV_WIDTH, k_rows), lambda b, i, j, *_: (0, b * nk + j)),
            pl.BlockSpec(k_meta.shape, lambda b, i, j, *_: (0, 0)),
            pl.BlockSpec(vt_meta.shape, lambda b, i, j, *_: (0, 0)),
        ],
        out_specs=pl.BlockSpec((q_rows, ATTN_WIDTH), lambda b, i, j, *_: (b * nq + i, 0)),
        scratch_shapes=[pltpu.VMEM((N_Q_HEADS, 8, q_rows), F32),
                        pltpu.VMEM((N_Q_HEADS, ATTN_V_ROWS, q_rows), F32)],
    )
    return pl.pallas_call(
        _attn_kernel,
        grid_spec=grid_spec,
        out_shape=jax.ShapeDtypeStruct((tokens, ATTN_WIDTH), BF16),
        compiler_params=pltpu.CompilerParams(dimension_semantics=("parallel", "parallel", "arbitrary"),
                                             vmem_limit_bytes=VMEM_LIMIT_BYTES),
        name="attn",
    )(safe, floor, qt, k, vt, k_meta, vt_meta)


def _split_hi_lo(x):
    hi = x.astype(BF16)
    lo = (x - hi.astype(F32)).astype(BF16)
    return hi, lo


def _gate_cumsum(gates, causal):
    t = gates.shape[0]
    row = lax.broadcasted_iota(jnp.int32, (t, t), 0)
    col = lax.broadcasted_iota(jnp.int32, (t, t), 1)
    mask = (col <= row) if causal else (col >= row)
    tri = jnp.where(mask, 1.0, 0.0).astype(BF16)
    hi, lo = _split_hi_lo(gates)
    sums = jnp.dot(tri, hi, preferred_element_type=F32) + jnp.dot(tri, lo, preferred_element_type=F32)
    return sums, mask


def _values_t_with_ones(vt):
    row = lax.broadcasted_iota(jnp.int32, (STATE_ROWS - M_HEAD_DIM, vt.shape[1]), 0)
    return jnp.concatenate([vt, jnp.where(row == 0, 1.0, 0.0).astype(vt.dtype)], axis=0)


def _state_update(state_t, m_in, k, vaug_t, li_row, b_row, b_tot):
    w_end = b_tot - b_row + li_row
    m_new = jnp.maximum(b_tot + m_in, jnp.max(w_end, axis=1, keepdims=True))
    vw = (vaug_t.astype(F32) * jnp.exp(w_end - m_new)).astype(BF16)
    return jnp.exp(b_tot + m_in - m_new) * state_t + jnp.dot(vw, k, preferred_element_type=F32), m_new


def _mlstm_kernel(qtf_ref, vtf_ref, kf_ref, gf_ref, qtb_ref, vtb_ref, kb_ref, gb_ref, kmeta_ref, vtmeta_ref,
                  gmeta_ref, hf_ref, hb_ref, state_scr, m_scr):
    c = pl.program_id(1)

    @pl.when(c == 0)
    def _reset_and_absorb_meta():
        state_scr[...] = jnp.zeros_like(state_scr)
        m_scr[...] = jnp.zeros_like(m_scr)
        gates = gmeta_ref[...]
        row = lax.broadcasted_iota(jnp.int32, gates.shape, 0)
        lane = lax.broadcasted_iota(jnp.int32, gates.shape, 1)
        gates = jnp.where(row < N_META, gates, jnp.where(lane < 2 * M_HEADS, NEG, 0.0))
        sums, _ = _gate_cumsum(gates, causal=True)
        gates_t = gates.T
        sums_t = sums.T
        last = gates.shape[0] - 1
        for hd in range(M_HEADS):
            f_lane = 2 * M_HEADS + hd
            sl = slice(hd * M_HEAD_DIM, (hd + 1) * M_HEAD_DIM)
            vt = vtmeta_ref[M_WIDTH + hd * M_HEAD_DIM:M_WIDTH + (hd + 1) * M_HEAD_DIM, :]
            new_state, m_new = _state_update(
                state_scr[hd], m_scr[hd][:1, :1], kmeta_ref[:, sl], _values_t_with_ones(vt),
                gates_t[hd:hd + 1, :], sums_t[f_lane:f_lane + 1, :], sums[last:last + 1, f_lane:f_lane + 1])
            state_scr[hd] = new_state
            m_scr[hd] = jnp.broadcast_to(m_new, m_scr.shape[1:])

    chains = []
    for causal, qt_ref, vt_ref, k_ref, g_ref, o_ref in ((True, qtf_ref, vtf_ref, kf_ref, gf_ref, hf_ref),
                                                        (False, qtb_ref, vtb_ref, kb_ref, gb_ref, hb_ref)):
        gates = g_ref[...]
        t = gates.shape[0]
        sums, _ = _gate_cumsum(gates, causal)
        gates_t = gates.T
        sums_t = sums.T
        src = lax.broadcasted_iota(jnp.int32, (t, t), 0)
        tgt = lax.broadcasted_iota(jnp.int32, (t, t), 1)
        feeds = (src <= tgt) if causal else (src >= tgt)
        last = t - 1 if causal else 0
        for hd in range(M_HEADS):
            chain = hd if causal else M_HEADS + hd
            i_lane, f_lane = chain, 2 * M_HEADS + chain
            chains.append(dict(
                chain=chain, feeds=feeds, o_ref=o_ref, qt_ref=qt_ref, vt_ref=vt_ref, k_ref=k_ref,
                sl=slice(hd * M_HEAD_DIM, (hd + 1) * M_HEAD_DIM), t=t,
                r_col=gates[:, i_lane:i_lane + 1] - sums[:, f_lane:f_lane + 1],
                li_row=gates_t[i_lane:i_lane + 1, :], b_row=sums_t[f_lane:f_lane + 1, :],
                b_tot=sums[last:last + 1, f_lane:f_lane + 1], m_in=m_scr[chain][:1, :1]))

    for ch in chains:
        lhs = jnp.concatenate([ch["k_ref"][:, ch["sl"]], state_scr[ch["chain"]].astype(BF16)], axis=0)
        both = jnp.dot(lhs, ch["qt_ref"][ch["sl"], :], preferred_element_type=F32)
        ch["kq"], ch["inter"] = both[:ch["t"]], both[ch["t"]:]
    for ch in chains:
        ch["r"] = jnp.where(ch["feeds"], ch["r_col"], NEG)
        ch["g"] = jnp.maximum(jnp.max(ch["r"], axis=0, keepdims=True), ch["m_in"])
    for ch in chains:
        ch["s"] = (ch["kq"] * jnp.exp(ch["r"] - ch["g"])).astype(BF16)
        ch["vaug"] = _values_t_with_ones(ch["vt_ref"][ch["sl"], :])
        w_end = ch["b_tot"] - ch["b_row"] + ch["li_row"]
        ch["m_new"] = jnp.maximum(ch["b_tot"] + ch["m_in"], jnp.max(w_end, axis=1, keepdims=True))
        ch["vw"] = (ch["vaug"].astype(F32) * jnp.exp(w_end - ch["m_new"])).astype(BF16)
    for ch in chains:
        ch["tot"] = jnp.dot(ch["vaug"], ch["s"], preferred_element_type=F32)
        ch["local"] = jnp.dot(ch["vw"], ch["k_ref"][:, ch["sl"]], preferred_element_type=F32)
    for ch in chains:
        tot = ch["tot"] + jnp.exp(ch["m_in"] - ch["g"]) * ch["inter"]
        den = jnp.maximum(jnp.abs(tot[M_HEAD_DIM:M_HEAD_DIM + 1]), jnp.exp(-(ch["b_row"] + ch["g"])))
        ch["o_ref"][:, ch["sl"]] = (tot[:M_HEAD_DIM] / den).T
        chain = ch["chain"]
        state_scr[chain] = jnp.exp(ch["b_tot"] + ch["m_in"] - ch["m_new"]) * state_scr[chain] + ch["local"]
        m_scr[chain] = jnp.broadcast_to(ch["m_new"], m_scr.shape[1:])


def _mlstm_call(mt, m, gates, mt_meta, m_meta, gates_meta, batch, chunk):
    tokens = m.shape[0]
    nc = tokens // batch // chunk

    def chunk_of(direction):
        return (lambda b, c: b * nc + c) if direction == "fwd" else (lambda b, c: b * nc + nc - 1 - c)

    def specs(direction):
        at = chunk_of(direction)
        return [pl.BlockSpec((M_WIDTH, chunk), lambda b, c: (0, at(b, c))),
                pl.BlockSpec((M_WIDTH, chunk), lambda b, c: (1, at(b, c))),
                pl.BlockSpec((chunk, M_WIDTH), lambda b, c: (at(b, c), 0)),
                pl.BlockSpec((chunk, LANES), lambda b, c: (at(b, c), 0))]

    def out_spec(direction):
        at = chunk_of(direction)
        return pl.BlockSpec((chunk, M_WIDTH), lambda b, c: (at(b, c), 0))

    return pl.pallas_call(
        _mlstm_kernel,
        grid=(batch, nc),
        in_specs=specs("fwd") + specs("bwd") + [pl.BlockSpec(m_meta.shape, lambda b, c: (0, 0)),
                                                pl.BlockSpec(mt_meta.shape, lambda b, c: (0, 0)),
                                                pl.BlockSpec(gates_meta.shape, lambda b, c: (0, 0))],
        out_specs=[out_spec("fwd"), out_spec("bwd")],
        out_shape=[jax.ShapeDtypeStruct((tokens, M_WIDTH), F32)] * 2,
        scratch_shapes=[pltpu.VMEM((2 * M_HEADS, STATE_ROWS, M_HEAD_DIM), F32),
                        pltpu.VMEM((2 * M_HEADS, 8, LANES), F32)],
        compiler_params=pltpu.CompilerParams(dimension_semantics=("parallel", "arbitrary"),
                                             vmem_limit_bytes=VMEM_LIMIT_BYTES),
        name="mlstm",
    )(mt, mt, m, gates, mt, mt, m, gates, m_meta, mt_meta, gates_meta)


def _back_kernel(h_ref, ao_ref, hf_ref, hb_ref, mo_ref, merge_ref, mgain_ref, wab_ref, wmb_ref, wout_ref,
                 g2_ref, w1_ref, w3_ref, w2_ref, y_ref):
    d_model = h_ref.shape[1]
    a_out = jnp.dot(ao_ref[...], wab_ref[...], preferred_element_type=F32)

    hm = hf_ref[...] + hb_ref[...]
    o_gate = jax.nn.sigmoid(mo_ref[...].astype(F32))
    gated = []
    for hd in range(M_HEADS):
        sl = slice(hd * M_HEAD_DIM, (hd + 1) * M_HEAD_DIM)
        gated.append((_rmsnorm(hm[:, sl], mgain_ref[:, sl]) * o_gate[:, sl]).astype(BF16))
    m_out = jnp.dot(jnp.concatenate(gated, axis=1), wmb_ref[...], preferred_element_type=F32)

    merged = (merge_ref[:, :d_model].astype(F32) * a_out + merge_ref[:, d_model:].astype(F32) * m_out)
    h2 = h_ref[...] + jnp.dot(merged.astype(BF16), wout_ref[...], preferred_element_type=F32)
    y_ref[...] = _swiglu_half_residual(h2, g2_ref[...], w1_ref, w3_ref, w2_ref)


def _back_call(h, attn_o, hf, hb, m, merge, wts, rows):
    tokens, d_model = h.shape

    def row_spec(width, col=0):
        return pl.BlockSpec((rows, width), lambda i: (i, col))

    consts = (wts["mgain"], wts["wab"], wts["wmb"], wts["wout"], wts["g2"], wts["w1b"], wts["w3b"], wts["w2b"])
    return pl.pallas_call(
        _back_kernel,
        grid=(tokens // rows,),
        in_specs=[row_spec(d_model), row_spec(ATTN_WIDTH), row_spec(M_WIDTH), row_spec(M_WIDTH),
                  row_spec(M_WIDTH, 1), row_spec(2 * d_model)] + [_const_spec(c.shape) for c in consts],
        out_specs=row_spec(d_model),
        out_shape=jax.ShapeDtypeStruct((tokens, d_model), F32),
        compiler_params=pltpu.CompilerParams(dimension_semantics=("parallel",),
                                             vmem_limit_bytes=VMEM_LIMIT_BYTES),
        name="back",
    )(h, attn_o, hf, hb, m, merge, *consts)


def _pair_heads(w):
    lead = w.shape[:-1]
    w = w.reshape(*lead, N_KV_HEADS, Q_PER_KV, HEAD_DIM)
    return jnp.swapaxes(w, -3, -2).reshape(*lead, ATTN_WIDTH)


def _prepare_weights(g_ffn1, w1_ffn1, w3_ffn1, w2_ffn1, g_mix, w_in, b_i, b_f, q_gain, k_gain, m_gain,
                     w_attn_br, w_mlstm_br, w_out, g_ffn2, w1_ffn2, w3_ffn2, w2_ffn2):
    d_model = w_in.shape[1]
    attn_w = N_Q_HEADS * HEAD_DIM
    kv_w = N_KV_HEADS * HEAD_DIM
    splits = (attn_w, kv_w, kv_w, M_WIDTH, M_WIDTH, M_WIDTH, M_WIDTH, 2 * M_HEADS, 2 * M_HEADS, d_model, d_model)
    offs = np.cumsum((0,) + splits)
    assert offs[-1] == w_in.shape[2]
    cols = [w_in[0, :, offs[i]:offs[i + 1]] for i in range(len(splits))]
    aq, ak, av, mq, mk, mv, mo, gi, gf, ga, gm = cols
    gate_w = jnp.pad(jnp.concatenate([gi, gf], axis=1), ((0, 0), (0, LANES - 4 * M_HEADS)))
    gate_b = jnp.pad(jnp.concatenate([b_i[0], b_f[0]]), (0, LANES - 4 * M_HEADS))[None]
    row = lambda g: g.astype(F32)[None]
    both_halves = lambda g: jnp.tile(g[0].astype(F32), LANES // HEAD_DIM)[None]
    return {
        "g1": row(g_ffn1[0]), "w1a": w1_ffn1[0].astype(BF16), "w3a": w3_ffn1[0].astype(BF16),
        "w2a": w2_ffn1[0].astype(BF16),
        "gmix": row(g_mix[0]),
        "wq": _pair_heads(aq).astype(BF16), "wkv": jnp.concatenate([ak, av], axis=1).astype(BF16),
        "wm": jnp.concatenate([mq, mk, mv, mo], axis=1).astype(BF16),
        "wg": gate_w.astype(BF16), "gbias": gate_b.astype(F32),
        "wmerge": jnp.concatenate([ga, gm], axis=1).astype(BF16),
        "qgain": both_halves(q_gain), "kgain": both_halves(k_gain),
        "mgain": m_gain[0].astype(F32).reshape(1, M_WIDTH),
        "wab": _pair_heads(w_attn_br[0].T).T.astype(BF16), "wmb": w_mlstm_br[0].astype(BF16), "wout": w_out[0].astype(BF16),
        "g2": row(g_ffn2[0]), "w1b": w1_ffn2[0].astype(BF16), "w3b": w3_ffn2[0].astype(BF16),
        "w2b": w2_ffn2[0].astype(BF16),
    }


def _rope_tables(n_tok):
    t = jnp.arange(n_tok)
    pos = jnp.stack([(t // GRID_W).astype(F32), (t % GRID_W).astype(F32)], axis=-1)
    inv_freq = ROPE_THETA ** (-2.0 * jnp.arange(ROPE_FREQS, dtype=F32) / ROPE_AXIS_DIM)
    ang = pos[:, :, None] * inv_freq
    cos, sin = jnp.cos(ang), jnp.sin(ang)
    cos = jnp.stack([cos, cos], axis=2).reshape(n_tok, HEAD_DIM)
    sin = jnp.stack([-sin, sin], axis=2).reshape(n_tok, HEAD_DIM)
    reps = (1, LANES // HEAD_DIM)
    return jnp.tile(cos, reps), jnp.tile(sin, reps)


def _trunk(x, meta_parts, wts):
    batch, n_tok, d_model = x.shape
    k_meta, v_meta, mt_meta, m_meta, gates_meta, norms_meta = meta_parts
    cos, sin = _rope_tables(n_tok)
    h, q, k, v, mt, m, gates, merge, norms = _front_call(
        x.reshape(batch * n_tok, d_model), cos, sin, wts, FRONT_ROWS)
    q_rows, k_rows = min(ATTN_Q_ROWS, n_tok), min(ATTN_K_ROWS, n_tok)
    safe, floor = _attn_safe_blocks(norms, norms_meta, batch, n_tok // q_rows, n_tok // k_rows)
    attn_o = _attn_call(q, k, v, k_meta, v_meta, safe, floor, batch, q_rows, k_rows)
    hf, hb = _mlstm_call(mt, m, gates, mt_meta, m_meta, gates_meta, batch, min(MLSTM_CHUNK, n_tok))
    y = _back_call(h, attn_o, hf, hb, m, merge, wts, FRONT_ROWS)
    return y.reshape(batch, n_tok, d_model)


def kernel(x_prompt, x_sample, meta, g_ffn1, w1_ffn1, w3_ffn1, w2_ffn1, g_mix, w_in, b_i, b_f, q_gain, k_gain,
           m_gain, w_attn_br, w_mlstm_br, w_out, g_ffn2, w1_ffn2, w3_ffn2, w2_ffn2):
    assert w_in.shape[0] == 1, "single-layer trunk: the meta rows' mixer outputs are never consumed"
    assert meta.shape[0] == N_META
    wts = _prepare_weights(g_ffn1, w1_ffn1, w3_ffn1, w2_ffn1, g_mix, w_in, b_i, b_f, q_gain, k_gain, m_gain,
                           w_attn_br, w_mlstm_br, w_out, g_ffn2, w1_ffn2, w3_ffn2, w2_ffn2)
    ones = jnp.ones((META_ROWS, LANES), F32)
    meta_rows = jnp.pad(meta.astype(F32), ((0, META_ROWS - N_META), (0, 0)))
    _, _, k_meta, v_meta, mt_meta, m_meta, gates_meta, _, norms_meta = _front_call(
        meta_rows, ones, jnp.zeros_like(ones), wts, META_ROWS)
    meta_parts = (k_meta, v_meta, mt_meta, m_meta, gates_meta, norms_meta)
    return (_trunk(x_prompt, meta_parts, wts), _trunk(x_sample, meta_parts, wts))
```

```python
import functools

import jax
import jax.numpy as jnp
import numpy as np
from jax import lax
from jax.experimental import pallas as pl
from jax.experimental.pallas import tpu as pltpu

F32 = jnp.float32
BF16 = jnp.bfloat16

N_META = 16
GRID_W = 64
EPS = 1e-6
NEG = -1e30
N_Q_HEADS = 8
N_KV_HEADS = 2
Q_PER_KV = N_Q_HEADS // N_KV_HEADS
HEAD_DIM = 64
ROPE_AXIS_DIM = HEAD_DIM // 2
ROPE_FREQS = ROPE_AXIS_DIM // 2
ROPE_THETA = 10000.0
M_HEADS = 4
M_HEAD_DIM = 128
M_WIDTH = M_HEADS * M_HEAD_DIM

LANES = 128
VMEM_LIMIT_BYTES = 60 * 1024 * 1024

FRONT_ROWS = 256
ATTN_Q_ROWS = 512
ATTN_K_ROWS = 2048
MLSTM_CHUNK = 256

Q_PAD = N_Q_HEADS * LANES
ATTN_WIDTH = N_Q_HEADS * HEAD_DIM
KV_WIDTH = N_KV_HEADS * HEAD_DIM
assert KV_WIDTH == LANES and HEAD_DIM * 2 == LANES
Q_SCALE = HEAD_DIM ** -0.5 * 1.4426950408889634
ATTN_SAFE_LOG2 = 90.0
ATTN_V_ROWS = 80
META_ROWS = LANES
STATE_ROWS = M_HEAD_DIM + 16


def _const_spec(shape):
    zeros = (0,) * len(shape)
    return pl.BlockSpec(shape, lambda *_: zeros, pipeline_mode=pl.Buffered(1))


def _rmsnorm(x, g):
    return x * lax.rsqrt(jnp.mean(x * x, axis=-1, keepdims=True) + EPS) * g


def _swiglu_half_residual(x, g, w1_ref, w3_ref, w2_ref):
    u = _rmsnorm(x, g).astype(BF16)
    a = jnp.dot(u, w1_ref[...], preferred_element_type=F32)
    b = jnp.dot(u, w3_ref[...], preferred_element_type=F32)
    act = (a * jax.nn.sigmoid(a) * b).astype(BF16)
    return x + 0.5 * jnp.dot(act, w2_ref[...], preferred_element_type=F32)


def _log_sigmoid(x):
    return jnp.minimum(x, 0.0) - jnp.log1p(jnp.exp(-jnp.abs(x)))


def _half_sums(x, low_half):
    low = jnp.sum(jnp.where(low_half, x, 0.0), axis=-1, keepdims=True)
    high = jnp.sum(jnp.where(low_half, 0.0, x), axis=-1, keepdims=True)
    return low, high


def _head_pair_norm_rope(x, gain, cos, sin_signed, low_half, first_half):
    ms_low, ms_high = _half_sums(x * x, low_half)
    inv = jnp.where(low_half, lax.rsqrt(ms_low * (1.0 / HEAD_DIM) + EPS), lax.rsqrt(ms_high * (1.0 / HEAD_DIM) + EPS))
    y = x * inv * gain
    partner = jnp.where(first_half, pltpu.roll(y, LANES - ROPE_FREQS, 1), pltpu.roll(y, ROPE_FREQS, 1))
    return y * cos + partner * sin_signed


def _front_kernel(x_ref, cos_ref, sin_ref, g1_ref, w1_ref, w3_ref, w2_ref, gmix_ref, wq_ref, wkv_ref,
                  wm_ref, wg_ref, wmerge_ref, qgain_ref, kgain_ref, gbias_ref,
                  h_ref, q_ref, k_ref, v_ref, mt_ref, m_ref, gate_ref, merge_ref, norm_ref):
    h = _swiglu_half_residual(x_ref[...], g1_ref[...], w1_ref, w3_ref, w2_ref)
    h_ref[...] = h
    u = _rmsnorm(h, gmix_ref[...]).astype(BF16)

    rows = h.shape[0]
    lane = lax.broadcasted_iota(jnp.int32, (rows, LANES), 1)
    low_half = lane < HEAD_DIM
    first_half = (lane % ROPE_AXIS_DIM) < ROPE_FREQS
    cos = cos_ref[...]
    sin = sin_ref[...]

    def sq_norms(y_bf16):
        y = y_bf16.astype(F32)
        return jnp.maximum(*_half_sums(y * y, low_half))

    zq = jnp.dot(u, wq_ref[...], preferred_element_type=F32)
    q_sq = jnp.zeros((rows, 1), F32)
    for pair in range(Q_PER_KV):
        y = _head_pair_norm_rope(zq[:, pair * LANES:(pair + 1) * LANES], qgain_ref[...], cos, sin,
                                 low_half, first_half) * Q_SCALE
        q_sq = jnp.maximum(q_sq, sq_norms(y.astype(BF16)))
        for group, keep in enumerate((low_half, ~low_half)):
            hd = group * Q_PER_KV + pair
            q_ref[hd * LANES:(hd + 1) * LANES, :] = jnp.where(keep, y, 0.0).T.astype(BF16)

    zkv = jnp.dot(u, wkv_ref[...], preferred_element_type=F32)
    k = _head_pair_norm_rope(zkv[:, :LANES], kgain_ref[...], cos, sin, low_half, first_half).astype(BF16)
    k_ref[...] = k
    k_sq = sq_norms(k)
    v_ref[...] = zkv[:, LANES:].T.astype(BF16)

    tile_row = lax.broadcasted_iota(jnp.int32, norm_ref.shape, 0)
    norm_ref[...] = jnp.where(tile_row == 0, jnp.max(q_sq, axis=0, keepdims=True),
                              jnp.max(k_sq, axis=0, keepdims=True))

    zm = jnp.dot(u, wm_ref[...], preferred_element_type=F32)
    mt_ref[:M_WIDTH, :] = zm[:, :M_WIDTH].T.astype(BF16)
    mt_ref[M_WIDTH:, :] = zm[:, 2 * M_WIDTH:3 * M_WIDTH].T.astype(BF16)
    m_ref[:, :M_WIDTH] = (zm[:, M_WIDTH:2 * M_WIDTH] * (M_HEAD_DIM ** -0.5)).astype(BF16)
    m_ref[:, M_WIDTH:] = zm[:, 3 * M_WIDTH:].astype(BF16)

    zg = jnp.dot(u, wg_ref[...], preferred_element_type=F32) + gbias_ref[...]
    gate_ref[...] = jnp.where(lane < 2 * M_HEADS, zg, _log_sigmoid(zg))

    zmerge = jnp.dot(u, wmerge_ref[...], preferred_element_type=F32)
    merge_ref[...] = jax.nn.sigmoid(zmerge).astype(BF16)


def _front_call(x, cos, sin, wts, rows):
    tokens, d_model = x.shape
    steps = tokens // rows
    table_steps = cos.shape[0] // rows

    def row_spec(width):
        return pl.BlockSpec((rows, width), lambda i: (i, 0))

    table_spec = pl.BlockSpec((rows, LANES), lambda i: (i % table_steps, 0))
    consts = (wts["g1"], wts["w1a"], wts["w3a"], wts["w2a"], wts["gmix"], wts["wq"], wts["wkv"],
              wts["wm"], wts["wg"], wts["wmerge"], wts["qgain"], wts["kgain"], wts["gbias"])

    def col_spec(height):
        return pl.BlockSpec((height, rows), lambda i: (0, i))

    out_specs = [row_spec(d_model), col_spec(Q_PAD), row_spec(KV_WIDTH), col_spec(KV_WIDTH), col_spec(2 * M_WIDTH),
                 row_spec(2 * M_WIDTH), row_spec(LANES), row_spec(2 * d_model),
                 pl.BlockSpec((None, 8, LANES), lambda i: (i, 0, 0))]
    out_shape = [jax.ShapeDtypeStruct((tokens, d_model), F32), jax.ShapeDtypeStruct((Q_PAD, tokens), BF16),
                 jax.ShapeDtypeStruct((tokens, KV_WIDTH), BF16), jax.ShapeDtypeStruct((KV_WIDTH, tokens), BF16),
                 jax.ShapeDtypeStruct((2 * M_WIDTH, tokens), BF16),
                 jax.ShapeDtypeStruct((tokens, 2 * M_WIDTH), BF16), jax.ShapeDtypeStruct((tokens, LANES), F32),
                 jax.ShapeDtypeStruct((tokens, 2 * d_model), BF16), jax.ShapeDtypeStruct((steps, 8, LANES), F32)]
    return pl.pallas_call(
        _front_kernel,
        grid=(steps,),
        in_specs=[row_spec(d_model), table_spec, table_spec] + [_const_spec(c.shape) for c in consts],
        out_specs=out_specs,
        out_shape=out_shape,
        compiler_params=pltpu.CompilerParams(dimension_semantics=("parallel",),
                                             vmem_limit_bytes=VMEM_LIMIT_BYTES),
        name="front",
    )(x, cos, sin, *consts)


def _attn_kernel(safe_ref, floor_ref, qt_ref, k_ref, vt_ref, kmeta_ref, vtmeta_ref, o_ref, m_scr, acc_scr):
    ki = pl.program_id(2)
    q_cols = qt_ref.shape[1]
    step = (pl.program_id(0) * pl.num_programs(1) + pl.program_id(1)) * pl.num_programs(2) + ki
    safe = safe_ref[step]

    def scores_t(hd, keys_ref):
        return jnp.dot(keys_ref[...], qt_ref[hd * LANES:(hd + 1) * LANES, :], preferred_element_type=F32)

    def values_t(hd, values_ref):
        g = hd // Q_PER_KV
        row = lax.broadcasted_iota(jnp.int32, (ATTN_V_ROWS - HEAD_DIM, values_ref.shape[1]), 0)
        ones = jnp.where(row == 0, 1.0, 0.0).astype(BF16)
        return jnp.concatenate([values_ref[g * HEAD_DIM:(g + 1) * HEAD_DIM, :], ones], axis=0)

    def meta_scores_t(hd):
        real = lax.broadcasted_iota(jnp.int32, (kmeta_ref.shape[0], q_cols), 0) < N_META
        return jnp.where(real, scores_t(hd, kmeta_ref), NEG)

    @pl.when((ki == 0) & (safe == 1))
    def _seed_from_meta_keys_against_floor():
        floor = jnp.full((1, q_cols), floor_ref[pl.program_id(0) * pl.num_programs(1) + pl.program_id(1)], F32)
        scores = [meta_scores_t(hd) for hd in range(N_Q_HEADS)]
        for hd, s in enumerate(scores):
            p = jnp.exp2(s - floor).astype(BF16)
            acc = jnp.dot(values_t(hd, vtmeta_ref), p, preferred_element_type=F32)
            m = jnp.maximum(floor, jnp.max(s, axis=0, keepdims=True))
            acc_scr[hd] = jnp.exp2(floor - m) * acc
            m_scr[hd] = jnp.broadcast_to(m, m_scr.shape[1:])

    @pl.when((ki == 0) & (safe != 1))
    def _seed_from_meta_keys():
        for hd in range(N_Q_HEADS):
            s = meta_scores_t(hd)
            m = jnp.max(s, axis=0, keepdims=True)
            p = jnp.exp2(s - m).astype(BF16)
            acc_scr[hd] = jnp.dot(values_t(hd, vtmeta_ref), p, preferred_element_type=F32)
            m_scr[hd] = jnp.broadcast_to(m, m_scr.shape[1:])

    @pl.when(safe == 1)
    def _block_against_previous_maximum():
        for hd in range(N_Q_HEADS):
            m_old = m_scr[hd][:1]
            s = scores_t(hd, k_ref)
            p = jnp.exp2(s - m_old).astype(BF16)
            pv = jnp.dot(values_t(hd, vt_ref), p, preferred_element_type=F32)
            m_new = jnp.maximum(m_old, jnp.max(s, axis=0, keepdims=True))
            acc_scr[hd] = jnp.exp2(m_old - m_new) * (acc_scr[hd] + pv)
            m_scr[hd] = jnp.broadcast_to(m_new, m_scr.shape[1:])

    @pl.when(safe != 1)
    def _block_against_own_maximum():
        for hd in range(N_Q_HEADS):
            m_old = m_scr[hd][:1]
            s = scores_t(hd, k_ref)
            m_new = jnp.maximum(m_old, jnp.max(s, axis=0, keepdims=True))
            p = jnp.exp2(s - m_new).astype(BF16)
            pv = jnp.dot(values_t(hd, vt_ref), p, preferred_element_type=F32)
            acc_scr[hd] = jnp.exp2(m_old - m_new) * acc_scr[hd] + pv
            m_scr[hd] = jnp.broadcast_to(m_new, m_scr.shape[1:])

    @pl.when(ki == pl.num_programs(2) - 1)
    def _normalise():
        def normalised(hd):
            acc = acc_scr[hd]
            return acc[:HEAD_DIM] / acc[HEAD_DIM:HEAD_DIM + 1]

        for pair in range(Q_PER_KV):
            both_t = jnp.concatenate([normalised(pair), normalised(Q_PER_KV + pair)], axis=0)
            o_ref[:, pair * LANES:(pair + 1) * LANES] = both_t.T.astype(BF16)


def _attn_safe_blocks(norms, norms_meta, batch, nq, nk):
    def block_max(sq, blocks):
        return jnp.sqrt(jnp.max(sq.reshape(blocks, -1), axis=1))

    q_norm = block_max(norms[:, 0, 0], batch * nq).reshape(batch, nq, 1)
    k_norm = block_max(norms[:, 1, 0], batch * nk).reshape(batch, 1, nk)
    k_meta_norm = jnp.sqrt(jnp.max(norms_meta[:, 1, 0]))
    gap = q_norm * (jnp.maximum(k_norm, k_meta_norm) + k_meta_norm)
    safe = (gap <= ATTN_SAFE_LOG2).astype(jnp.int32).reshape(-1)
    floor = (-q_norm * k_meta_norm).reshape(-1)
    return safe, floor


def _attn_call(qt, k, vt, k_meta, vt_meta, safe, floor, batch, q_rows, k_rows):
    tokens = k.shape[0]
    n_tok = tokens // batch
    nq = n_tok // q_rows
    nk = n_tok // k_rows
    grid_spec = pltpu.PrefetchScalarGridSpec(
        num_scalar_prefetch=2,
        grid=(batch, nq, nk),
        in_specs=[
            pl.BlockSpec((Q_PAD, q_rows), lambda b, i, j, *_: (0, b * nq + i)),
            pl.BlockSpec((k_rows, KV_WIDTH), lambda b, i, j, *_: (b * nk + j, 0)),
            pl.BlockSpec((KV_WIDTH, k_rows), lambda b, i, j, *_: (0, b * nk + j)),
            pl.BlockSpec(k_meta.shape, lambda b, i, j, *_: (0, 0)),
            pl.BlockSpec(vt_meta.shape, lambda b, i, j, *_: (0, 0)),
        ],
        out_specs=pl.BlockSpec((q_rows, ATTN_WIDTH), lambda b, i, j, *_: (b * nq + i, 0)),
        scratch_shapes=[pltpu.VMEM((N_Q_HEADS, 8, q_rows), F32),
                        pltpu.VMEM((N_Q_HEADS, ATTN_V_ROWS, q_rows), F32)],
    )
    return pl.pallas_call(
        _attn_kernel,
        grid_spec=grid_spec,
        out_shape=jax.ShapeDtypeStruct((tokens, ATTN_WIDTH), BF16),
        compiler_params=pltpu.CompilerParams(dimension_semantics=("parallel", "parallel", "arbitrary"),
                                             vmem_limit_bytes=VMEM_LIMIT_BYTES),
        name="attn",
    )(safe, floor, qt, k, vt, k_meta, vt_meta)


def _split_hi_lo(x):
    hi = x.astype(BF16)
    lo = (x - hi.astype(F32)).astype(BF16)
    return hi, lo


def _gate_cumsum(gates, causal):
    t = gates.shape[0]
    row = lax.broadcasted_iota(jnp.int32, (t, t), 0)
    col = lax.broadcasted_iota(jnp.int32, (t, t), 1)
    mask = (col <= row) if causal else (col >= row)
    tri = jnp.where(mask, 1.0, 0.0).astype(BF16)
    hi, lo = _split_hi_lo(gates)
    sums = jnp.dot(tri, hi, preferred_element_type=F32) + jnp.dot(tri, lo, preferred_element_type=F32)
    return sums, mask


def _values_t_with_ones(vt):
    row = lax.broadcasted_iota(jnp.int32, (STATE_ROWS - M_HEAD_DIM, vt.shape[1]), 0)
    return jnp.concatenate([vt, jnp.where(row == 0, 1.0, 0.0).astype(vt.dtype)], axis=0)


def _state_update(state_t, m_in, k, vaug_t, li_row, b_row, b_tot):
    w_end = b_tot - b_row + li_row
    m_new = jnp.maximum(b_tot + m_in, jnp.max(w_end, axis=1, keepdims=True))
    vw = (vaug_t.astype(F32) * jnp.exp(w_end - m_new)).astype(BF16)
    return jnp.exp(b_tot + m_in - m_new) * state_t + jnp.dot(vw, k, preferred_element_type=F32), m_new


def _mlstm_kernel(qtf_ref, vtf_ref, kf_ref, gf_ref, qtb_ref, vtb_ref, kb_ref, gb_ref, kmeta_ref, vtmeta_ref,
                  gmeta_ref, hf_ref, hb_ref, state_scr, m_scr):
    c = pl.program_id(1)

    @pl.when(c == 0)
    def _reset_and_absorb_meta():
        state_scr[...] = jnp.zeros_like(state_scr)
        m_scr[...] = jnp.zeros_like(m_scr)
        gates = gmeta_ref[...]
        row = lax.broadcasted_iota(jnp.int32, gates.shape, 0)
        lane = lax.broadcasted_iota(jnp.int32, gates.shape, 1)
        gates = jnp.where(row < N_META, gates, jnp.where(lane < 2 * M_HEADS, NEG, 0.0))
        sums, _ = _gate_cumsum(gates, causal=True)
        gates_t = gates.T
        sums_t = sums.T
        last = gates.shape[0] - 1
        for hd in range(M_HEADS):
            f_lane = 2 * M_HEADS + hd
            sl = slice(hd * M_HEAD_DIM, (hd + 1) * M_HEAD_DIM)
            vt = vtmeta_ref[M_WIDTH + hd * M_HEAD_DIM:M_WIDTH + (hd + 1) * M_HEAD_DIM, :]
            new_state, m_new = _state_update(
                state_scr[hd], m_scr[hd][:1, :1], kmeta_ref[:, sl], _values_t_with_ones(vt),
                gates_t[hd:hd + 1, :], sums_t[f_lane:f_lane + 1, :], sums[last:last + 1, f_lane:f_lane + 1])
            state_scr[hd] = new_state
            m_scr[hd] = jnp.broadcast_to(m_new, m_scr.shape[1:])

    chains = []
    for causal, qt_ref, vt_ref, k_ref, g_ref, o_ref in ((True, qtf_ref, vtf_ref, kf_ref, gf_ref, hf_ref),
                                                        (False, qtb_ref, vtb_ref, kb_ref, gb_ref, hb_ref)):
        gates = g_ref[...]
        t = gates.shape[0]
        sums, _ = _gate_cumsum(gates, causal)
        gates_t = gates.T
        sums_t = sums.T
        src = lax.broadcasted_iota(jnp.int32, (t, t), 0)
        tgt = lax.broadcasted_iota(jnp.int32, (t, t), 1)
        feeds = (src <= tgt) if causal else (src >= tgt)
        last = t - 1 if causal else 0
        for hd in range(M_HEADS):
            chain = hd if causal else M_HEADS + hd
            i_lane, f_lane = chain, 2 * M_HEADS + chain
            chains.append(dict(
                chain=chain, feeds=feeds, o_ref=o_ref, qt_ref=qt_ref, vt_ref=vt_ref, k_ref=k_ref,
                sl=slice(hd * M_HEAD_DIM, (hd + 1) * M_HEAD_DIM), t=t,
                r_col=gates[:, i_lane:i_lane + 1] - sums[:, f_lane:f_lane + 1],
                li_row=gates_t[i_lane:i_lane + 1, :], b_row=sums_t[f_lane:f_lane + 1, :],
                b_tot=sums[last:last + 1, f_lane:f_lane + 1], m_in=m_scr[chain][:1, :1]))

    for ch in chains:
        lhs = jnp.concatenate([ch["k_ref"][:, ch["sl"]], state_scr[ch["chain"]].astype(BF16)], axis=0)
        both = jnp.dot(lhs, ch["qt_ref"][ch["sl"], :], preferred_element_type=F32)
        ch["kq"], ch["inter"] = both[:ch["t"]], both[ch["t"]:]
    for ch in chains:
        ch["r"] = jnp.where(ch["feeds"], ch["r_col"], NEG)
        ch["g"] = jnp.maximum(jnp.max(ch["r"], axis=0, keepdims=True), ch["m_in"])
    for ch in chains:
        ch["s"] = (ch["kq"] * jnp.exp(ch["r"] - ch["g"])).astype(BF16)
        ch["vaug"] = _values_t_with_ones(ch["vt_ref"][ch["sl"], :])
        w_end = ch["b_tot"] - ch["b_row"] + ch["li_row"]
        ch["m_new"] = jnp.maximum(ch["b_tot"] + ch["m_in"], jnp.max(w_end, axis=1, keepdims=True))
        ch["vw"] = (ch["vaug"].astype(F32) * jnp.exp(w_end - ch["m_new"])).astype(BF16)
    for ch in chains:
        ch["tot"] = jnp.dot(ch["vaug"], ch["s"], preferred_element_type=F32)
        ch["local"] = jnp.dot(ch["vw"], ch["k_ref"][:, ch["sl"]], preferred_element_type=F32)
    for ch in chains:
        tot = ch["tot"] + jnp.exp(ch["m_in"] - ch["g"]) * ch["inter"]
        den = jnp.maximum(jnp.abs(tot[M_HEAD_DIM:M_HEAD_DIM + 1]), jnp.exp(-(ch["b_row"] + ch["g"])))
        ch["o_ref"][ch["sl"], :] = tot[:M_HEAD_DIM] / den
        chain = ch["chain"]
        state_scr[chain] = jnp.exp(ch["b_tot"] + ch["m_in"] - ch["m_new"]) * state_scr[chain] + ch["local"]
        m_scr[chain] = jnp.broadcast_to(ch["m_new"], m_scr.shape[1:])


def _mlstm_call(mt, m, gates, mt_meta, m_meta, gates_meta, batch, chunk):
    tokens = m.shape[0]
    nc = tokens // batch // chunk

    def chunk_of(direction):
        return (lambda b, c: b * nc + c) if direction == "fwd" else (lambda b, c: b * nc + nc - 1 - c)

    def specs(direction):
        at = chunk_of(direction)
        return [pl.BlockSpec((M_WIDTH, chunk), lambda b, c: (0, at(b, c))),
                pl.BlockSpec((M_WIDTH, chunk), lambda b, c: (1, at(b, c))),
                pl.BlockSpec((chunk, M_WIDTH), lambda b, c: (at(b, c), 0)),
                pl.BlockSpec((chunk, LANES), lambda b, c: (at(b, c), 0))]

    def out_spec(direction):
        at = chunk_of(direction)
        return pl.BlockSpec((M_WIDTH, chunk), lambda b, c: (0, at(b, c)))

    return pl.pallas_call(
        _mlstm_kernel,
        grid=(batch, nc),
        in_specs=specs("fwd") + specs("bwd") + [pl.BlockSpec(m_meta.shape, lambda b, c: (0, 0)),
                                                pl.BlockSpec(mt_meta.shape, lambda b, c: (0, 0)),
                                                pl.BlockSpec(gates_meta.shape, lambda b, c: (0, 0))],
        out_specs=[out_spec("fwd"), out_spec("bwd")],
        out_shape=[jax.ShapeDtypeStruct((M_WIDTH, tokens), F32)] * 2,
        scratch_shapes=[pltpu.VMEM((2 * M_HEADS, STATE_ROWS, M_HEAD_DIM), F32),
                        pltpu.VMEM((2 * M_HEADS, 8, LANES), F32)],
        compiler_params=pltpu.CompilerParams(dimension_semantics=("parallel", "arbitrary"),
                                             vmem_limit_bytes=VMEM_LIMIT_BYTES),
        name="mlstm",
    )(mt, mt, m, gates, mt, mt, m, gates, m_meta, mt_meta, gates_meta)


def _back_kernel(h_ref, ao_ref, hf_ref, hb_ref, mo_ref, merge_ref, mgain_ref, wab_ref, wmb_ref, wout_ref,
                 g2_ref, w1_ref, w3_ref, w2_ref, y_ref):
    d_model = h_ref.shape[1]
    a_out = jnp.dot(ao_ref[...], wab_ref[...], preferred_element_type=F32)

    hm = (hf_ref[...] + hb_ref[...]).T
    o_gate = jax.nn.sigmoid(mo_ref[...].astype(F32))
    gated = []
    for hd in range(M_HEADS):
        sl = slice(hd * M_HEAD_DIM, (hd + 1) * M_HEAD_DIM)
        gated.append((_rmsnorm(hm[:, sl], mgain_ref[:, sl]) * o_gate[:, sl]).astype(BF16))
    m_out = jnp.dot(jnp.concatenate(gated, axis=1), wmb_ref[...], preferred_element_type=F32)

    merged = (merge_ref[:, :d_model].astype(F32) * a_out + merge_ref[:, d_model:].astype(F32) * m_out)
    h2 = h_ref[...] + jnp.dot(merged.astype(BF16), wout_ref[...], preferred_element_type=F32)
    y_ref[...] = _swiglu_half_residual(h2, g2_ref[...], w1_ref, w3_ref, w2_ref)


def _back_call(h, attn_o, hf, hb, m, merge, wts, rows):
    tokens, d_model = h.shape

    def row_spec(width, col=0):
        return pl.BlockSpec((rows, width), lambda i: (i, col))

    col_spec = pl.BlockSpec((M_WIDTH, rows), lambda i: (0, i))
    consts = (wts["mgain"], wts["wab"], wts["wmb"], wts["wout"], wts["g2"], wts["w1b"], wts["w3b"], wts["w2b"])
    return pl.pallas_call(
        _back_kernel,
        grid=(tokens // rows,),
        in_specs=[row_spec(d_model), row_spec(ATTN_WIDTH), col_spec, col_spec,
                  row_spec(M_WIDTH, 1), row_spec(2 * d_model)] + [_const_spec(c.shape) for c in consts],
        out_specs=row_spec(d_model),
        out_shape=jax.ShapeDtypeStruct((tokens, d_model), F32),
        compiler_params=pltpu.CompilerParams(dimension_semantics=("parallel",),
                                             vmem_limit_bytes=VMEM_LIMIT_BYTES),
        name="back",
    )(h, attn_o, hf, hb, m, merge, *consts)


def _pair_heads(w):
    lead = w.shape[:-1]
    w = w.reshape(*lead, N_KV_HEADS, Q_PER_KV, HEAD_DIM)
    return jnp.swapaxes(w, -3, -2).reshape(*lead, ATTN_WIDTH)


def _prepare_weights(g_ffn1, w1_ffn1, w3_ffn1, w2_ffn1, g_mix, w_in, b_i, b_f, q_gain, k_gain, m_gain,
                     w_attn_br, w_mlstm_br, w_out, g_ffn2, w1_ffn2, w3_ffn2, w2_ffn2):
    d_model = w_in.shape[1]
    attn_w = N_Q_HEADS * HEAD_DIM
    kv_w = N_KV_HEADS * HEAD_DIM
    splits = (attn_w, kv_w, kv_w, M_WIDTH, M_WIDTH, M_WIDTH, M_WIDTH, 2 * M_HEADS, 2 * M_HEADS, d_model, d_model)
    offs = np.cumsum((0,) + splits)
    assert offs[-1] == w_in.shape[2]
    cols = [w_in[0, :, offs[i]:offs[i + 1]] for i in range(len(splits))]
    aq, ak, av, mq, mk, mv, mo, gi, gf, ga, gm = cols
    gate_w = jnp.pad(jnp.concatenate([gi, gf], axis=1), ((0, 0), (0, LANES - 4 * M_HEADS)))
    gate_b = jnp.pad(jnp.concatenate([b_i[0], b_f[0]]), (0, LANES - 4 * M_HEADS))[None]
    row = lambda g: g.astype(F32)[None]
    both_halves = lambda g: jnp.tile(g[0].astype(F32), LANES // HEAD_DIM)[None]
    return {
        "g1": row(g_ffn1[0]), "w1a": w1_ffn1[0].astype(BF16), "w3a": w3_ffn1[0].astype(BF16),
        "w2a": w2_ffn1[0].astype(BF16),
        "gmix": row(g_mix[0]),
        "wq": _pair_heads(aq).astype(BF16), "wkv": jnp.concatenate([ak, av], axis=1).astype(BF16),
        "wm": jnp.concatenate([mq, mk, mv, mo], axis=1).astype(BF16),
        "wg": gate_w.astype(BF16), "gbias": gate_b.astype(F32),
        "wmerge": jnp.concatenate([ga, gm], axis=1).astype(BF16),
        "qgain": both_halves(q_gain), "kgain": both_halves(k_gain),
        "mgain": m_gain[0].astype(F32).reshape(1, M_WIDTH),
        "wab": _pair_heads(w_attn_br[0].T).T.astype(BF16), "wmb": w_mlstm_br[0].astype(BF16), "wout": w_out[0].astype(BF16),
        "g2": row(g_ffn2[0]), "w1b": w1_ffn2[0].astype(BF16), "w3b": w3_ffn2[0].astype(BF16),
        "w2b": w2_ffn2[0].astype(BF16),
    }


def _rope_tables(n_tok):
    t = jnp.arange(n_tok)
    pos = jnp.stack([(t // GRID_W).astype(F32), (t % GRID_W).astype(F32)], axis=-1)
    inv_freq = ROPE_THETA ** (-2.0 * jnp.arange(ROPE_FREQS, dtype=F32) / ROPE_AXIS_DIM)
    ang = pos[:, :, None] * inv_freq
    cos, sin = jnp.cos(ang), jnp.sin(ang)
    cos = jnp.stack([cos, cos], axis=2).reshape(n_tok, HEAD_DIM)
    sin = jnp.stack([-sin, sin], axis=2).reshape(n_tok, HEAD_DIM)
    reps = (1, LANES // HEAD_DIM)
    return jnp.tile(cos, reps), jnp.tile(sin, reps)


def _trunk(x, meta_parts, wts):
    batch, n_tok, d_model = x.shape
    k_meta, v_meta, mt_meta, m_meta, gates_meta, norms_meta = meta_parts
    cos, sin = _rope_tables(n_tok)
    h, q, k, v, mt, m, gates, merge, norms = _front_call(
        x.reshape(batch * n_tok, d_model), cos, sin, wts, FRONT_ROWS)
    q_rows, k_rows = min(ATTN_Q_ROWS, n_tok), min(ATTN_K_ROWS, n_tok)
    safe, floor = _attn_safe_blocks(norms, norms_meta, batch, n_tok // q_rows, n_tok // k_rows)
    attn_o = _attn_call(q, k, v, k_meta, v_meta, safe, floor, batch, q_rows, k_rows)
    hf, hb = _mlstm_call(mt, m, gates, mt_meta, m_meta, gates_meta, batch, min(MLSTM_CHUNK, n_tok))
    y = _back_call(h, attn_o, hf, hb, m, merge, wts, FRONT_ROWS)
    return y.reshape(batch, n_tok, d_model)


def kernel(x_prompt, x_sample, meta, g_ffn1, w1_ffn1, w3_ffn1, w2_ffn1, g_mix, w_in, b_i, b_f, q_gain, k_gain,
           m_gain, w_attn_br, w_mlstm_br, w_out, g_ffn2, w1_ffn2, w3_ffn2, w2_ffn2):
    assert w_in.shape[0] == 1, "single-layer trunk: the meta rows' mixer outputs are never consumed"
    assert meta.shape[0] == N_META
    wts = _prepare_weights(g_ffn1, w1_ffn1, w3_ffn1, w2_ffn1, g_mix, w_in, b_i, b_f, q_gain, k_gain, m_gain,
                           w_attn_br, w_mlstm_br, w_out, g_ffn2, w1_ffn2, w3_ffn2, w2_ffn2)
    ones = jnp.ones((META_ROWS, LANES), F32)
    meta_rows = jnp.pad(meta.astype(F32), ((0, META_ROWS - N_META), (0, 0)))
    _, _, k_meta, v_meta, mt_meta, m_meta, gates_meta, _, norms_meta = _front_call(
        meta_rows, ones, jnp.zeros_like(ones), wts, META_ROWS)
    meta_parts = (k_meta, v_meta, mt_meta, m_meta, gates_meta, norms_meta)
    return (_trunk(x_prompt, meta_parts, wts), _trunk(x_sample, meta_parts, wts))
```

```python
import functools

import jax
import jax.numpy as jnp
import numpy as np
from jax import lax
from jax.experimental import pallas as pl
from jax.experimental.pallas import tpu as pltpu

F32 = jnp.float32
BF16 = jnp.bfloat16

N_META = 16
GRID_W = 64
EPS = 1e-6
NEG = -1e30
N_Q_HEADS = 8
N_KV_HEADS = 2
Q_PER_KV = N_Q_HEADS // N_KV_HEADS
HEAD_DIM = 64
ROPE_AXIS_DIM = HEAD_DIM // 2
ROPE_FREQS = ROPE_AXIS_DIM // 2
ROPE_THETA = 10000.0
M_HEADS = 4
M_HEAD_DIM = 128
M_WIDTH = M_HEADS * M_HEAD_DIM

LANES = 128
VMEM_LIMIT_BYTES = 60 * 1024 * 1024

FRONT_ROWS = 256
ATTN_Q_ROWS = 1024
ATTN_K_ROWS = 2048
MLSTM_CHUNK = 256

Q_PAD = N_Q_HEADS * LANES
ATTN_WIDTH = N_Q_HEADS * HEAD_DIM
KV_WIDTH = N_KV_HEADS * HEAD_DIM
assert KV_WIDTH == LANES and HEAD_DIM * 2 == LANES
Q_SCALE = HEAD_DIM ** -0.5 * 1.4426950408889634
ATTN_SAFE_LOG2 = 90.0
ATTN_V_ROWS = 80
META_ROWS = LANES
STATE_ROWS = M_HEAD_DIM + 16


def _const_spec(shape):
    zeros = (0,) * len(shape)
    return pl.BlockSpec(shape, lambda *_: zeros, pipeline_mode=pl.Buffered(1))


def _rmsnorm(x, g):
    return x * lax.rsqrt(jnp.mean(x * x, axis=-1, keepdims=True) + EPS) * g


def _swiglu_half_residual(x, g, w1_ref, w3_ref, w2_ref):
    u = _rmsnorm(x, g).astype(BF16)
    a = jnp.dot(u, w1_ref[...], preferred_element_type=F32)
    b = jnp.dot(u, w3_ref[...], preferred_element_type=F32)
    act = (a * jax.nn.sigmoid(a) * b).astype(BF16)
    return x + 0.5 * jnp.dot(act, w2_ref[...], preferred_element_type=F32)


def _log_sigmoid(x):
    return jnp.minimum(x, 0.0) - jnp.log1p(jnp.exp(-jnp.abs(x)))


def _half_sums(x, low_half):
    low = jnp.sum(jnp.where(low_half, x, 0.0), axis=-1, keepdims=True)
    high = jnp.sum(jnp.where(low_half, 0.0, x), axis=-1, keepdims=True)
    return low, high


def _head_pair_norm_rope(x, gain, cos, sin_signed, low_half, first_half):
    ms_low, ms_high = _half_sums(x * x, low_half)
    inv = jnp.where(low_half, lax.rsqrt(ms_low * (1.0 / HEAD_DIM) + EPS), lax.rsqrt(ms_high * (1.0 / HEAD_DIM) + EPS))
    y = x * inv * gain
    partner = jnp.where(first_half, pltpu.roll(y, LANES - ROPE_FREQS, 1), pltpu.roll(y, ROPE_FREQS, 1))
    return y * cos + partner * sin_signed


def _front_kernel(x_ref, cos_ref, sin_ref, g1_ref, w1_ref, w3_ref, w2_ref, gmix_ref, wq_ref, wkv_ref,
                  wm_ref, wg_ref, wmerge_ref, qgain_ref, kgain_ref, gbias_ref,
                  h_ref, q_ref, k_ref, v_ref, mt_ref, m_ref, gate_ref, merge_ref, norm_ref):
    h = _swiglu_half_residual(x_ref[...], g1_ref[...], w1_ref, w3_ref, w2_ref)
    h_ref[...] = h
    u = _rmsnorm(h, gmix_ref[...]).astype(BF16)

    rows = h.shape[0]
    lane = lax.broadcasted_iota(jnp.int32, (rows, LANES), 1)
    low_half = lane < HEAD_DIM
    first_half = (lane % ROPE_AXIS_DIM) < ROPE_FREQS
    cos = cos_ref[...]
    sin = sin_ref[...]

    def sq_norms(y_bf16):
        y = y_bf16.astype(F32)
        return jnp.maximum(*_half_sums(y * y, low_half))

    zq = jnp.dot(u, wq_ref[...], preferred_element_type=F32)
    q_sq = jnp.zeros((rows, 1), F32)
    for pair in range(Q_PER_KV):
        y = _head_pair_norm_rope(zq[:, pair * LANES:(pair + 1) * LANES], qgain_ref[...], cos, sin,
                                 low_half, first_half) * Q_SCALE
        q_sq = jnp.maximum(q_sq, sq_norms(y.astype(BF16)))
        for group, keep in enumerate((low_half, ~low_half)):
            hd = group * Q_PER_KV + pair
            q_ref[hd * LANES:(hd + 1) * LANES, :] = jnp.where(keep, y, 0.0).T.astype(BF16)

    zkv = jnp.dot(u, wkv_ref[...], preferred_element_type=F32)
    k = _head_pair_norm_rope(zkv[:, :LANES], kgain_ref[...], cos, sin, low_half, first_half).astype(BF16)
    k_ref[...] = k
    k_sq = sq_norms(k)
    v_ref[...] = zkv[:, LANES:].T.astype(BF16)

    tile_row = lax.broadcasted_iota(jnp.int32, norm_ref.shape, 0)
    norm_ref[...] = jnp.where(tile_row == 0, jnp.max(q_sq, axis=0, keepdims=True),
                              jnp.max(k_sq, axis=0, keepdims=True))

    zm = jnp.dot(u, wm_ref[...], preferred_element_type=F32)
    mt_ref[:M_WIDTH, :] = zm[:, :M_WIDTH].T.astype(BF16)
    mt_ref[M_WIDTH:, :] = zm[:, 2 * M_WIDTH:3 * M_WIDTH].T.astype(BF16)
    m_ref[:, :M_WIDTH] = (zm[:, M_WIDTH:2 * M_WIDTH] * (M_HEAD_DIM ** -0.5)).astype(BF16)
    m_ref[:, M_WIDTH:] = zm[:, 3 * M_WIDTH:].astype(BF16)

    zg = jnp.dot(u, wg_ref[...], preferred_element_type=F32) + gbias_ref[...]
    gate_ref[...] = jnp.where(lane < 2 * M_HEADS, zg, _log_sigmoid(zg))

    zmerge = jnp.dot(u, wmerge_ref[...], preferred_element_type=F32)
    merge_ref[...] = jax.nn.sigmoid(zmerge).astype(BF16)


def _front_call(x, cos, sin, wts, rows):
    tokens, d_model = x.shape
    steps = tokens // rows
    table_steps = cos.shape[0] // rows

    def row_spec(width):
        return pl.BlockSpec((rows, width), lambda i: (i, 0))

    table_spec = pl.BlockSpec((rows, LANES), lambda i: (i % table_steps, 0))
    consts = (wts["g1"], wts["w1a"], wts["w3a"], wts["w2a"], wts["gmix"], wts["wq"], wts["wkv"],
              wts["wm"], wts["wg"], wts["wmerge"], wts["qgain"], wts["kgain"], wts["gbias"])

    def col_spec(height):
        return pl.BlockSpec((height, rows), lambda i: (0, i))

    out_specs = [row_spec(d_model), col_spec(Q_PAD), row_spec(KV_WIDTH), col_spec(KV_WIDTH), col_spec(2 * M_WIDTH),
                 row_spec(2 * M_WIDTH), row_spec(LANES), row_spec(2 * d_model),
                 pl.BlockSpec((None, 8, LANES), lambda i: (i, 0, 0))]
    out_shape = [jax.ShapeDtypeStruct((tokens, d_model), F32), jax.ShapeDtypeStruct((Q_PAD, tokens), BF16),
                 jax.ShapeDtypeStruct((tokens, KV_WIDTH), BF16), jax.ShapeDtypeStruct((KV_WIDTH, tokens), BF16),
                 jax.ShapeDtypeStruct((2 * M_WIDTH, tokens), BF16),
                 jax.ShapeDtypeStruct((tokens, 2 * M_WIDTH), BF16), jax.ShapeDtypeStruct((tokens, LANES), F32),
                 jax.ShapeDtypeStruct((tokens, 2 * d_model), BF16), jax.ShapeDtypeStruct((steps, 8, LANES), F32)]
    return pl.pallas_call(
        _front_kernel,
        grid=(steps,),
        in_specs=[row_spec(d_model), table_spec, table_spec] + [_const_spec(c.shape) for c in consts],
        out_specs=out_specs,
        out_shape=out_shape,
        compiler_params=pltpu.CompilerParams(dimension_semantics=("parallel",),
                                             vmem_limit_bytes=VMEM_LIMIT_BYTES),
        name="front",
    )(x, cos, sin, *consts)


def _attn_kernel(safe_ref, floor_ref, qt_ref, k_ref, vt_ref, kmeta_ref, vtmeta_ref, o_ref, m_scr, acc_scr):
    ki = pl.program_id(2)
    q_cols = qt_ref.shape[1]
    step = (pl.program_id(0) * pl.num_programs(1) + pl.program_id(1)) * pl.num_programs(2) + ki
    safe = safe_ref[step]

    def scores_t(hd, keys_ref):
        return jnp.dot(keys_ref[...], qt_ref[hd * LANES:(hd + 1) * LANES, :], preferred_element_type=F32)

    def values_t(hd, values_ref):
        g = hd // Q_PER_KV
        row = lax.broadcasted_iota(jnp.int32, (ATTN_V_ROWS - HEAD_DIM, values_ref.shape[1]), 0)
        ones = jnp.where(row == 0, 1.0, 0.0).astype(BF16)
        return jnp.concatenate([values_ref[g * HEAD_DIM:(g + 1) * HEAD_DIM, :], ones], axis=0)

    def meta_scores_t(hd):
        real = lax.broadcasted_iota(jnp.int32, (kmeta_ref.shape[0], q_cols), 0) < N_META
        return jnp.where(real, scores_t(hd, kmeta_ref), NEG)

    @pl.when((ki == 0) & (safe == 1))
    def _seed_from_meta_keys_against_floor():
        floor = jnp.full((1, q_cols), floor_ref[pl.program_id(0) * pl.num_programs(1) + pl.program_id(1)], F32)
        scores = [meta_scores_t(hd) for hd in range(N_Q_HEADS)]
        for hd, s in enumerate(scores):
            p = jnp.exp2(s - floor).astype(BF16)
            acc = jnp.dot(values_t(hd, vtmeta_ref), p, preferred_element_type=F32)
            m = jnp.maximum(floor, jnp.max(s, axis=0, keepdims=True))
            acc_scr[hd] = jnp.exp2(floor - m) * acc
            m_scr[hd] = jnp.broadcast_to(m, m_scr.shape[1:])

    @pl.when((ki == 0) & (safe != 1))
    def _seed_from_meta_keys():
        for hd in range(N_Q_HEADS):
            s = meta_scores_t(hd)
            m = jnp.max(s, axis=0, keepdims=True)
            p = jnp.exp2(s - m).astype(BF16)
            acc_scr[hd] = jnp.dot(values_t(hd, vtmeta_ref), p, preferred_element_type=F32)
            m_scr[hd] = jnp.broadcast_to(m, m_scr.shape[1:])

    @pl.when(safe == 1)
    def _block_against_previous_maximum():
        s_next = scores_t(0, k_ref)
        for hd in range(N_Q_HEADS):
            m_old = m_scr[hd][:1]
            s = s_next
            if hd + 1 < N_Q_HEADS:
                s_next = scores_t(hd + 1, k_ref)
            p = jnp.exp2(s - m_old).astype(BF16)
            pv = jnp.dot(values_t(hd, vt_ref), p, preferred_element_type=F32)
            m_new = jnp.maximum(m_old, jnp.max(s, axis=0, keepdims=True))
            acc_scr[hd] = jnp.exp2(m_old - m_new) * (acc_scr[hd] + pv)
            m_scr[hd] = jnp.broadcast_to(m_new, m_scr.shape[1:])

    @pl.when(safe != 1)
    def _block_against_own_maximum():
        for hd in range(N_Q_HEADS):
            m_old = m_scr[hd][:1]
            s = scores_t(hd, k_ref)
            m_new = jnp.maximum(m_old, jnp.max(s, axis=0, keepdims=True))
            p = jnp.exp2(s - m_new).astype(BF16)
            pv = jnp.dot(values_t(hd, vt_ref), p, preferred_element_type=F32)
            acc_scr[hd] = jnp.exp2(m_old - m_new) * acc_scr[hd] + pv
            m_scr[hd] = jnp.broadcast_to(m_new, m_scr.shape[1:])

    @pl.when(ki == pl.num_programs(2) - 1)
    def _normalise():
        def normalised(hd):
            acc = acc_scr[hd]
            return acc[:HEAD_DIM] / acc[HEAD_DIM:HEAD_DIM + 1]

        for pair in range(Q_PER_KV):
            both_t = jnp.concatenate([normalised(pair), normalised(Q_PER_KV + pair)], axis=0)
            o_ref[:, pair * LANES:(pair + 1) * LANES] = both_t.T.astype(BF16)


def _attn_safe_blocks(norms, norms_meta, batch, nq, nk):
    def block_max(sq, blocks):
        return jnp.sqrt(jnp.max(sq.reshape(blocks, -1), axis=1))

    q_norm = block_max(norms[:, 0, 0], batch * nq).reshape(batch, nq, 1)
    k_norm = block_max(norms[:, 1, 0], batch * nk).reshape(batch, 1, nk)
    k_meta_norm = jnp.sqrt(jnp.max(norms_meta[:, 1, 0]))
    gap = q_norm * (jnp.maximum(k_norm, k_meta_norm) + k_meta_norm)
    safe = (gap <= ATTN_SAFE_LOG2).astype(jnp.int32).reshape(-1)
    floor = (-q_norm * k_meta_norm).reshape(-1)
    return safe, floor


def _attn_call(qt, k, vt, k_meta, vt_meta, safe, floor, batch, q_rows, k_rows):
    tokens = k.shape[0]
    n_tok = tokens // batch
    nq = n_tok // q_rows
    nk = n_tok // k_rows
    grid_spec = pltpu.PrefetchScalarGridSpec(
        num_scalar_prefetch=2,
        grid=(batch, nq, nk),
        in_specs=[
            pl.BlockSpec((Q_PAD, q_rows), lambda b, i, j, *_: (0, b * nq + i)),
            pl.BlockSpec((k_rows, KV_WIDTH), lambda b, i, j, *_: (b * nk + j, 0)),
            pl.BlockSpec((KV_WIDTH, k_rows), lambda b, i, j, *_: (0, b * nk + j)),
            pl.BlockSpec(k_meta.shape, lambda b, i, j, *_: (0, 0)),
            pl.BlockSpec(vt_meta.shape, lambda b, i, j, *_: (0, 0)),
        ],
        out_specs=pl.BlockSpec((q_rows, ATTN_WIDTH), lambda b, i, j, *_: (b * nq + i, 0)),
        scratch_shapes=[pltpu.VMEM((N_Q_HEADS, 8, q_rows), F32),
                        pltpu.VMEM((N_Q_HEADS, ATTN_V_ROWS, q_rows), F32)],
    )
    return pl.pallas_call(
        _attn_kernel,
        grid_spec=grid_spec,
        out_shape=jax.ShapeDtypeStruct((tokens, ATTN_WIDTH), BF16),
        compiler_params=pltpu.CompilerParams(dimension_semantics=("parallel", "parallel", "arbitrary"),
                                             vmem_limit_bytes=VMEM_LIMIT_BYTES),
        name="attn",
    )(safe, floor, qt, k, vt, k_meta, vt_meta)


def _split_hi_lo(x):
    hi = x.astype(BF16)
    lo = (x - hi.astype(F32)).astype(BF16)
    return hi, lo


def _gate_cumsum(gates, causal):
    t = gates.shape[0]
    row = lax.broadcasted_iota(jnp.int32, (t, t), 0)
    col = lax.broadcasted_iota(jnp.int32, (t, t), 1)
    mask = (col <= row) if causal else (col >= row)
    tri = jnp.where(mask, 1.0, 0.0).astype(BF16)
    hi, lo = _split_hi_lo(gates)
    sums = jnp.dot(tri, hi, preferred_element_type=F32) + jnp.dot(tri, lo, preferred_element_type=F32)
    return sums, mask


def _values_t_with_ones(vt):
    row = lax.broadcasted_iota(jnp.int32, (STATE_ROWS - M_HEAD_DIM, vt.shape[1]), 0)
    return jnp.concatenate([vt, jnp.where(row == 0, 1.0, 0.0).astype(vt.dtype)], axis=0)


def _state_update(state_t, m_in, k, vaug_t, li_row, b_row, b_tot):
    w_end = b_tot - b_row + li_row
    m_new = jnp.maximum(b_tot + m_in, jnp.max(w_end, axis=1, keepdims=True))
    vw = (vaug_t.astype(F32) * jnp.exp(w_end - m_new)).astype(BF16)
    return jnp.exp(b_tot + m_in - m_new) * state_t + jnp.dot(vw, k, preferred_element_type=F32), m_new


def _mlstm_kernel(qtf_ref, vtf_ref, kf_ref, gf_ref, qtb_ref, vtb_ref, kb_ref, gb_ref, kmeta_ref, vtmeta_ref,
                  gmeta_ref, hf_ref, hb_ref, state_scr, m_scr):
    c = pl.program_id(1)

    @pl.when(c == 0)
    def _reset_and_absorb_meta():
        state_scr[...] = jnp.zeros_like(state_scr)
        m_scr[...] = jnp.zeros_like(m_scr)
        gates = gmeta_ref[...]
        row = lax.broadcasted_iota(jnp.int32, gates.shape, 0)
        lane = lax.broadcasted_iota(jnp.int32, gates.shape, 1)
        gates = jnp.where(row < N_META, gates, jnp.where(lane < 2 * M_HEADS, NEG, 0.0))
        sums, _ = _gate_cumsum(gates, causal=True)
        gates_t = gates.T
        sums_t = sums.T
        last = gates.shape[0] - 1
        for hd in range(M_HEADS):
            f_lane = 2 * M_HEADS + hd
            sl = slice(hd * M_HEAD_DIM, (hd + 1) * M_HEAD_DIM)
            vt = vtmeta_ref[M_WIDTH + hd * M_HEAD_DIM:M_WIDTH + (hd + 1) * M_HEAD_DIM, :]
            new_state, m_new = _state_update(
                state_scr[hd], m_scr[hd][:1, :1], kmeta_ref[:, sl], _values_t_with_ones(vt),
                gates_t[hd:hd + 1, :], sums_t[f_lane:f_lane + 1, :], sums[last:last + 1, f_lane:f_lane + 1])
            state_scr[hd] = new_state
            m_scr[hd] = jnp.broadcast_to(m_new, m_scr.shape[1:])

    chains = []
    for causal, qt_ref, vt_ref, k_ref, g_ref, o_ref in ((True, qtf_ref, vtf_ref, kf_ref, gf_ref, hf_ref),
                                                        (False, qtb_ref, vtb_ref, kb_ref, gb_ref, hb_ref)):
        gates = g_ref[...]
        t = gates.shape[0]
        sums, _ = _gate_cumsum(gates, causal)
        gates_t = gates.T
        sums_t = sums.T
        src = lax.broadcasted_iota(jnp.int32, (t, t), 0)
        tgt = lax.broadcasted_iota(jnp.int32, (t, t), 1)
        feeds = (src <= tgt) if causal else (src >= tgt)
        last = t - 1 if causal else 0
        for hd in range(M_HEADS):
            chain = hd if causal else M_HEADS + hd
            i_lane, f_lane = chain, 2 * M_HEADS + chain
            chains.append(dict(
                chain=chain, feeds=feeds, o_ref=o_ref, qt_ref=qt_ref, vt_ref=vt_ref, k_ref=k_ref,
                sl=slice(hd * M_HEAD_DIM, (hd + 1) * M_HEAD_DIM), t=t,
                r_col=gates[:, i_lane:i_lane + 1] - sums[:, f_lane:f_lane + 1],
                li_row=gates_t[i_lane:i_lane + 1, :], b_row=sums_t[f_lane:f_lane + 1, :],
                b_tot=sums[last:last + 1, f_lane:f_lane + 1], m_in=m_scr[chain][:1, :1]))

    for ch in chains:
        lhs = jnp.concatenate([ch["k_ref"][:, ch["sl"]], state_scr[ch["chain"]].astype(BF16)], axis=0)
        both = jnp.dot(lhs, ch["qt_ref"][ch["sl"], :], preferred_element_type=F32)
        ch["kq"], ch["inter"] = both[:ch["t"]], both[ch["t"]:]
    for ch in chains:
        ch["r"] = jnp.where(ch["feeds"], ch["r_col"], NEG)
        ch["g"] = jnp.maximum(jnp.max(ch["r"], axis=0, keepdims=True), ch["m_in"])
    for ch in chains:
        ch["s"] = (ch["kq"] * jnp.exp(ch["r"] - ch["g"])).astype(BF16)
        ch["vaug"] = _values_t_with_ones(ch["vt_ref"][ch["sl"], :])
        w_end = ch["b_tot"] - ch["b_row"] + ch["li_row"]
        ch["m_new"] = jnp.maximum(ch["b_tot"] + ch["m_in"], jnp.max(w_end, axis=1, keepdims=True))
        ch["vw"] = (ch["vaug"].astype(F32) * jnp.exp(w_end - ch["m_new"])).astype(BF16)
    for ch in chains:
        ch["tot"] = jnp.dot(ch["vaug"], ch["s"], preferred_element_type=F32)
        ch["local"] = jnp.dot(ch["vw"], ch["k_ref"][:, ch["sl"]], preferred_element_type=F32)
    for ch in chains:
        tot = ch["tot"] + jnp.exp(ch["m_in"] - ch["g"]) * ch["inter"]
        den = jnp.maximum(jnp.abs(tot[M_HEAD_DIM:M_HEAD_DIM + 1]), jnp.exp(-(ch["b_row"] + ch["g"])))
        ch["o_ref"][ch["sl"], :] = tot[:M_HEAD_DIM] / den
        chain = ch["chain"]
        state_scr[chain] = jnp.exp(ch["b_tot"] + ch["m_in"] - ch["m_new"]) * state_scr[chain] + ch["local"]
        m_scr[chain] = jnp.broadcast_to(ch["m_new"], m_scr.shape[1:])


def _mlstm_call(mt, m, gates, mt_meta, m_meta, gates_meta, batch, chunk):
    tokens = m.shape[0]
    nc = tokens // batch // chunk

    def chunk_of(direction):
        return (lambda b, c: b * nc + c) if direction == "fwd" else (lambda b, c: b * nc + nc - 1 - c)

    def specs(direction):
        at = chunk_of(direction)
        return [pl.BlockSpec((M_WIDTH, chunk), lambda b, c: (0, at(b, c))),
                pl.BlockSpec((M_WIDTH, chunk), lambda b, c: (1, at(b, c))),
                pl.BlockSpec((chunk, M_WIDTH), lambda b, c: (at(b, c), 0)),
                pl.BlockSpec((chunk, LANES), lambda b, c: (at(b, c), 0))]

    def out_spec(direction):
        at = chunk_of(direction)
        return pl.BlockSpec((M_WIDTH, chunk), lambda b, c: (0, at(b, c)))

    return pl.pallas_call(
        _mlstm_kernel,
        grid=(batch, nc),
        in_specs=specs("fwd") + specs("bwd") + [pl.BlockSpec(m_meta.shape, lambda b, c: (0, 0)),
                                                pl.BlockSpec(mt_meta.shape, lambda b, c: (0, 0)),
                                                pl.BlockSpec(gates_meta.shape, lambda b, c: (0, 0))],
        out_specs=[out_spec("fwd"), out_spec("bwd")],
        out_shape=[jax.ShapeDtypeStruct((M_WIDTH, tokens), F32)] * 2,
        scratch_shapes=[pltpu.VMEM((2 * M_HEADS, STATE_ROWS, M_HEAD_DIM), F32),
                        pltpu.VMEM((2 * M_HEADS, 8, LANES), F32)],
        compiler_params=pltpu.CompilerParams(dimension_semantics=("parallel", "arbitrary"),
                                             vmem_limit_bytes=VMEM_LIMIT_BYTES),
        name="mlstm",
    )(mt, mt, m, gates, mt, mt, m, gates, m_meta, mt_meta, gates_meta)


def _back_kernel(h_ref, ao_ref, hf_ref, hb_ref, mo_ref, merge_ref, mgain_ref, wab_ref, wmb_ref, wout_ref,
                 g2_ref, w1_ref, w3_ref, w2_ref, y_ref):
    d_model = h_ref.shape[1]
    a_out = jnp.dot(ao_ref[...], wab_ref[...], preferred_element_type=F32)

    hm = (hf_ref[...] + hb_ref[...]).T
    o_gate = jax.nn.sigmoid(mo_ref[...].astype(F32))
    gated = []
    for hd in range(M_HEADS):
        sl = slice(hd * M_HEAD_DIM, (hd + 1) * M_HEAD_DIM)
        gated.append((_rmsnorm(hm[:, sl], mgain_ref[:, sl]) * o_gate[:, sl]).astype(BF16))
    m_out = jnp.dot(jnp.concatenate(gated, axis=1), wmb_ref[...], preferred_element_type=F32)

    merged = (merge_ref[:, :d_model].astype(F32) * a_out + merge_ref[:, d_model:].astype(F32) * m_out)
    h2 = h_ref[...] + jnp.dot(merged.astype(BF16), wout_ref[...], preferred_element_type=F32)
    y_ref[...] = _swiglu_half_residual(h2, g2_ref[...], w1_ref, w3_ref, w2_ref)


def _back_call(h, attn_o, hf, hb, m, merge, wts, rows):
    tokens, d_model = h.shape

    def row_spec(width, col=0):
        return pl.BlockSpec((rows, width), lambda i: (i, col))

    col_spec = pl.BlockSpec((M_WIDTH, rows), lambda i: (0, i))
    consts = (wts["mgain"], wts["wab"], wts["wmb"], wts["wout"], wts["g2"], wts["w1b"], wts["w3b"], wts["w2b"])
    return pl.pallas_call(
        _back_kernel,
        grid=(tokens // rows,),
        in_specs=[row_spec(d_model), row_spec(ATTN_WIDTH), col_spec, col_spec,
                  row_spec(M_WIDTH, 1), row_spec(2 * d_model)] + [_const_spec(c.shape) for c in consts],
        out_specs=row_spec(d_model),
        out_shape=jax.ShapeDtypeStruct((tokens, d_model), F32),
        compiler_params=pltpu.CompilerParams(dimension_semantics=("parallel",),
                                             vmem_limit_bytes=VMEM_LIMIT_BYTES),
        name="back",
    )(h, attn_o, hf, hb, m, merge, *consts)


def _pair_heads(w):
    lead = w.shape[:-1]
    w = w.reshape(*lead, N_KV_HEADS, Q_PER_KV, HEAD_DIM)
    return jnp.swapaxes(w, -3, -2).reshape(*lead, ATTN_WIDTH)


def _prepare_weights(g_ffn1, w1_ffn1, w3_ffn1, w2_ffn1, g_mix, w_in, b_i, b_f, q_gain, k_gain, m_gain,
                     w_attn_br, w_mlstm_br, w_out, g_ffn2, w1_ffn2, w3_ffn2, w2_ffn2):
    d_model = w_in.shape[1]
    attn_w = N_Q_HEADS * HEAD_DIM
    kv_w = N_KV_HEADS * HEAD_DIM
    splits = (attn_w, kv_w, kv_w, M_WIDTH, M_WIDTH, M_WIDTH, M_WIDTH, 2 * M_HEADS, 2 * M_HEADS, d_model, d_model)
    offs = np.cumsum((0,) + splits)
    assert offs[-1] == w_in.shape[2]
    cols = [w_in[0, :, offs[i]:offs[i + 1]] for i in range(len(splits))]
    aq, ak, av, mq, mk, mv, mo, gi, gf, ga, gm = cols
    gate_w = jnp.pad(jnp.concatenate([gi, gf], axis=1), ((0, 0), (0, LANES - 4 * M_HEADS)))
    gate_b = jnp.pad(jnp.concatenate([b_i[0], b_f[0]]), (0, LANES - 4 * M_HEADS))[None]
    row = lambda g: g.astype(F32)[None]
    both_halves = lambda g: jnp.tile(g[0].astype(F32), LANES // HEAD_DIM)[None]
    return {
        "g1": row(g_ffn1[0]), "w1a": w1_ffn1[0].astype(BF16), "w3a": w3_ffn1[0].astype(BF16),
        "w2a": w2_ffn1[0].astype(BF16),
        "gmix": row(g_mix[0]),
        "wq": _pair_heads(aq).astype(BF16), "wkv": jnp.concatenate([ak, av], axis=1).astype(BF16),
        "wm": jnp.concatenate([mq, mk, mv, mo], axis=1).astype(BF16),
        "wg": gate_w.astype(BF16), "gbias": gate_b.astype(F32),
        "wmerge": jnp.concatenate([ga, gm], axis=1).astype(BF16),
        "qgain": both_halves(q_gain), "kgain": both_halves(k_gain),
        "mgain": m_gain[0].astype(F32).reshape(1, M_WIDTH),
        "wab": _pair_heads(w_attn_br[0].T).T.astype(BF16), "wmb": w_mlstm_br[0].astype(BF16), "wout": w_out[0].astype(BF16),
        "g2": row(g_ffn2[0]), "w1b": w1_ffn2[0].astype(BF16), "w3b": w3_ffn2[0].astype(BF16),
        "w2b": w2_ffn2[0].astype(BF16),
    }


def _rope_tables(n_tok):
    t = jnp.arange(n_tok)
    pos = jnp.stack([(t // GRID_W).astype(F32), (t % GRID_W).astype(F32)], axis=-1)
    inv_freq = ROPE_THETA ** (-2.0 * jnp.arange(ROPE_FREQS, dtype=F32) / ROPE_AXIS_DIM)
    ang = pos[:, :, None] * inv_freq
    cos, sin = jnp.cos(ang), jnp.sin(ang)
    cos = jnp.stack([cos, cos], axis=2).reshape(n_tok, HEAD_DIM)
    sin = jnp.stack([-sin, sin], axis=2).reshape(n_tok, HEAD_DIM)
    reps = (1, LANES // HEAD_DIM)
    return jnp.tile(cos, reps), jnp.tile(sin, reps)


def _trunk(x, meta_parts, wts):
    batch, n_tok, d_model = x.shape
    k_meta, v_meta, mt_meta, m_meta, gates_meta, norms_meta = meta_parts
    cos, sin = _rope_tables(n_tok)
    h, q, k, v, mt, m, gates, merge, norms = _front_call(
        x.reshape(batch * n_tok, d_model), cos, sin, wts, FRONT_ROWS)
    q_rows, k_rows = min(ATTN_Q_ROWS, n_tok), min(ATTN_K_ROWS, n_tok)
    safe, floor = _attn_safe_blocks(norms, norms_meta, batch, n_tok // q_rows, n_tok // k_rows)
    attn_o = _attn_call(q, k, v, k_meta, v_meta, safe, floor, batch, q_rows, k_rows)
    hf, hb = _mlstm_call(mt, m, gates, mt_meta, m_meta, gates_meta, batch, min(MLSTM_CHUNK, n_tok))
    y = _back_call(h, attn_o, hf, hb, m, merge, wts, FRONT_ROWS)
    return y.reshape(batch, n_tok, d_model)


def kernel(x_prompt, x_sample, meta, g_ffn1, w1_ffn1, w3_ffn1, w2_ffn1, g_mix, w_in, b_i, b_f, q_gain, k_gain,
           m_gain, w_attn_br, w_mlstm_br, w_out, g_ffn2, w1_ffn2, w3_ffn2, w2_ffn2):
    assert w_in.shape[0] == 1, "single-layer trunk: the meta rows' mixer outputs are never consumed"
    assert meta.shape[0] == N_META
    wts = _prepare_weights(g_ffn1, w1_ffn1, w3_ffn1, w2_ffn1, g_mix, w_in, b_i, b_f, q_gain, k_gain, m_gain,
                           w_attn_br, w_mlstm_br, w_out, g_ffn2, w1_ffn2, w3_ffn2, w2_ffn2)
    ones = jnp.ones((META_ROWS, LANES), F32)
    meta_rows = jnp.pad(meta.astype(F32), ((0, META_ROWS - N_META), (0, 0)))
    _, _, k_meta, v_meta, mt_meta, m_meta, gates_meta, _, norms_meta = _front_call(
        meta_rows, ones, jnp.zeros_like(ones), wts, META_ROWS)
    meta_parts = (k_meta, v_meta, mt_meta, m_meta, gates_meta, norms_meta)
    return (_trunk(x_prompt, meta_parts, wts), _trunk(x_sample, meta_parts, wts))
```

```python
import functools

import jax
import jax.numpy as jnp
import numpy as np
from jax import lax
from jax.experimental import pallas as pl
from jax.experimental.pallas import tpu as pltpu

F32 = jnp.float32
BF16 = jnp.bfloat16

N_META = 16
GRID_W = 64
EPS = 1e-6
NEG = -1e30
N_Q_HEADS = 8
N_KV_HEADS = 2
Q_PER_KV = N_Q_HEADS // N_KV_HEADS
HEAD_DIM = 64
ROPE_AXIS_DIM = HEAD_DIM // 2
ROPE_FREQS = ROPE_AXIS_DIM // 2
ROPE_THETA = 10000.0
M_HEADS = 4
M_HEAD_DIM = 128
M_WIDTH = M_HEADS * M_HEAD_DIM

LANES = 128
VMEM_LIMIT_BYTES = 60 * 1024 * 1024

FRONT_ROWS = 512
BACK_ROWS = 512
SUB_ROWS = 256
ATTN_Q_ROWS = 1024
ATTN_K_ROWS = 2048
MLSTM_CHUNK = 256

Q_PAD = N_Q_HEADS * LANES
ATTN_WIDTH = N_Q_HEADS * HEAD_DIM
KV_WIDTH = N_KV_HEADS * HEAD_DIM
assert KV_WIDTH == LANES and HEAD_DIM * 2 == LANES
Q_SCALE = HEAD_DIM ** -0.5 * 1.4426950408889634
ATTN_SAFE_LOG2 = 90.0
ATTN_V_ROWS = 80
META_ROWS = LANES
STATE_ROWS = M_HEAD_DIM + 16


def _const_spec(shape):
    zeros = (0,) * len(shape)
    return pl.BlockSpec(shape, lambda *_: zeros, pipeline_mode=pl.Buffered(1))


def _rmsnorm(x, g):
    return x * lax.rsqrt(jnp.mean(x * x, axis=-1, keepdims=True) + EPS) * g


def _swiglu_half_residual(x, g, w1_ref, w3_ref, w2_ref):
    u = _rmsnorm(x, g).astype(BF16)
    a = jnp.dot(u, w1_ref[...], preferred_element_type=F32)
    b = jnp.dot(u, w3_ref[...], preferred_element_type=F32)
    act = (a * jax.nn.sigmoid(a) * b).astype(BF16)
    return x + 0.5 * jnp.dot(act, w2_ref[...], preferred_element_type=F32)


def _log_sigmoid(x):
    return jnp.minimum(x, 0.0) - jnp.log1p(jnp.exp(-jnp.abs(x)))


def _half_sums(x, low_half):
    low = jnp.sum(jnp.where(low_half, x, 0.0), axis=-1, keepdims=True)
    high = jnp.sum(jnp.where(low_half, 0.0, x), axis=-1, keepdims=True)
    return low, high


def _head_pair_norm_rope(x, gain, cos, sin_signed, low_half, first_half):
    ms_low, ms_high = _half_sums(x * x, low_half)
    inv = jnp.where(low_half, lax.rsqrt(ms_low * (1.0 / HEAD_DIM) + EPS), lax.rsqrt(ms_high * (1.0 / HEAD_DIM) + EPS))
    y = x * inv * gain
    partner = jnp.where(first_half, pltpu.roll(y, LANES - ROPE_FREQS, 1), pltpu.roll(y, ROPE_FREQS, 1))
    return y * cos + partner * sin_signed


def _front_kernel(x_ref, cos_ref, sin_ref, g1_ref, w1_ref, w3_ref, w2_ref, gmix_ref, wq_ref, wkv_ref,
                  wm_ref, wg_ref, wmerge_ref, qgain_ref, kgain_ref, gbias_ref,
                  h_ref, q_ref, k_ref, v_ref, mt_ref, m_ref, gate_ref, merge_ref, norm_ref):
    sub = min(x_ref.shape[0], SUB_ROWS)
    lane = lax.broadcasted_iota(jnp.int32, (sub, LANES), 1)
    low_half = lane < HEAD_DIM
    first_half = (lane % ROPE_AXIS_DIM) < ROPE_FREQS

    def sq_norms(y_bf16):
        y = y_bf16.astype(F32)
        return jnp.maximum(*_half_sums(y * y, low_half))

    q_sq = jnp.zeros((sub, 1), F32)
    k_sq = jnp.zeros((sub, 1), F32)
    for r in range(0, x_ref.shape[0], sub):
        rs = slice(r, r + sub)
        h = _swiglu_half_residual(x_ref[rs, :], g1_ref[...], w1_ref, w3_ref, w2_ref)
        h_ref[rs, :] = h
        u = _rmsnorm(h, gmix_ref[...]).astype(BF16)
        cos = cos_ref[rs, :]
        sin = sin_ref[rs, :]

        zq = jnp.dot(u, wq_ref[...], preferred_element_type=F32)
        for pair in range(Q_PER_KV):
            y = _head_pair_norm_rope(zq[:, pair * LANES:(pair + 1) * LANES], qgain_ref[...], cos, sin,
                                     low_half, first_half) * Q_SCALE
            q_sq = jnp.maximum(q_sq, sq_norms(y.astype(BF16)))
            for group, keep in enumerate((low_half, ~low_half)):
                hd = group * Q_PER_KV + pair
                q_ref[hd * LANES:(hd + 1) * LANES, rs] = jnp.where(keep, y, 0.0).T.astype(BF16)

        zkv = jnp.dot(u, wkv_ref[...], preferred_element_type=F32)
        k = _head_pair_norm_rope(zkv[:, :LANES], kgain_ref[...], cos, sin, low_half, first_half).astype(BF16)
        k_ref[rs, :] = k
        k_sq = jnp.maximum(k_sq, sq_norms(k))
        v_ref[:, rs] = zkv[:, LANES:].T.astype(BF16)

        zm = jnp.dot(u, wm_ref[...], preferred_element_type=F32)
        mt_ref[:M_WIDTH, rs] = zm[:, :M_WIDTH].T.astype(BF16)
        mt_ref[M_WIDTH:, rs] = zm[:, 2 * M_WIDTH:3 * M_WIDTH].T.astype(BF16)
        m_ref[rs, :M_WIDTH] = (zm[:, M_WIDTH:2 * M_WIDTH] * (M_HEAD_DIM ** -0.5)).astype(BF16)
        m_ref[rs, M_WIDTH:] = zm[:, 3 * M_WIDTH:].astype(BF16)

        zg = jnp.dot(u, wg_ref[...], preferred_element_type=F32) + gbias_ref[...]
        gate_ref[rs, :] = jnp.where(lane < 2 * M_HEADS, zg, _log_sigmoid(zg))

        zmerge = jnp.dot(u, wmerge_ref[...], preferred_element_type=F32)
        merge_ref[rs, :] = jax.nn.sigmoid(zmerge).astype(BF16)

    tile_row = lax.broadcasted_iota(jnp.int32, norm_ref.shape, 0)
    norm_ref[...] = jnp.where(tile_row == 0, jnp.max(q_sq, axis=0, keepdims=True),
                              jnp.max(k_sq, axis=0, keepdims=True))


def _front_call(x, cos, sin, wts, rows):
    tokens, d_model = x.shape
    steps = tokens // rows
    table_steps = cos.shape[0] // rows

    def row_spec(width):
        return pl.BlockSpec((rows, width), lambda i: (i, 0))

    table_spec = pl.BlockSpec((rows, LANES), lambda i: (i % table_steps, 0))
    consts = (wts["g1"], wts["w1a"], wts["w3a"], wts["w2a"], wts["gmix"], wts["wq"], wts["wkv"],
              wts["wm"], wts["wg"], wts["wmerge"], wts["qgain"], wts["kgain"], wts["gbias"])

    def col_spec(height):
        return pl.BlockSpec((height, rows), lambda i: (0, i))

    out_specs = [row_spec(d_model), col_spec(Q_PAD), row_spec(KV_WIDTH), col_spec(KV_WIDTH), col_spec(2 * M_WIDTH),
                 row_spec(2 * M_WIDTH), row_spec(LANES), row_spec(2 * d_model),
                 pl.BlockSpec((None, 8, LANES), lambda i: (i, 0, 0))]
    out_shape = [jax.ShapeDtypeStruct((tokens, d_model), F32), jax.ShapeDtypeStruct((Q_PAD, tokens), BF16),
                 jax.ShapeDtypeStruct((tokens, KV_WIDTH), BF16), jax.ShapeDtypeStruct((KV_WIDTH, tokens), BF16),
                 jax.ShapeDtypeStruct((2 * M_WIDTH, tokens), BF16),
                 jax.ShapeDtypeStruct((tokens, 2 * M_WIDTH), BF16), jax.ShapeDtypeStruct((tokens, LANES), F32),
                 jax.ShapeDtypeStruct((tokens, 2 * d_model), BF16), jax.ShapeDtypeStruct((steps, 8, LANES), F32)]
    return pl.pallas_call(
        _front_kernel,
        grid=(steps,),
        in_specs=[row_spec(d_model), table_spec, table_spec] + [_const_spec(c.shape) for c in consts],
        out_specs=out_specs,
        out_shape=out_shape,
        compiler_params=pltpu.CompilerParams(dimension_semantics=("parallel",),
                                             vmem_limit_bytes=VMEM_LIMIT_BYTES),
        name="front",
    )(x, cos, sin, *consts)


def _attn_kernel(safe_ref, floor_ref, qt_ref, k_ref, vt_ref, kmeta_ref, vtmeta_ref, o_ref, m_scr, acc_scr):
    ki = pl.program_id(2)
    q_cols = qt_ref.shape[1]
    step = (pl.program_id(0) * pl.num_programs(1) + pl.program_id(1)) * pl.num_programs(2) + ki
    safe = safe_ref[step]

    def scores_t(hd, keys_ref):
        return jnp.dot(keys_ref[...], qt_ref[hd * LANES:(hd + 1) * LANES, :], preferred_element_type=F32)

    def values_t(hd, values_ref):
        g = hd // Q_PER_KV
        row = lax.broadcasted_iota(jnp.int32, (ATTN_V_ROWS - HEAD_DIM, values_ref.shape[1]), 0)
        ones = jnp.where(row == 0, 1.0, 0.0).astype(BF16)
        return jnp.concatenate([values_ref[g * HEAD_DIM:(g + 1) * HEAD_DIM, :], ones], axis=0)

    def meta_scores_t(hd):
        real = lax.broadcasted_iota(jnp.int32, (kmeta_ref.shape[0], q_cols), 0) < N_META
        return jnp.where(real, scores_t(hd, kmeta_ref), NEG)

    @pl.when((ki == 0) & (safe == 1))
    def _seed_from_meta_keys_against_floor():
        floor = jnp.full((1, q_cols), floor_ref[pl.program_id(0) * pl.num_programs(1) + pl.program_id(1)], F32)
        scores = [jnp.dot(kmeta_ref[:N_META, :], qt_ref[hd * LANES:(hd + 1) * LANES, :],
                          preferred_element_type=F32) for hd in range(N_Q_HEADS)]
        for hd, s in enumerate(scores):
            p = jnp.exp2(s - floor).astype(BF16)
            acc = jnp.dot(values_t(hd, vtmeta_ref)[:, :N_META], p, preferred_element_type=F32)
            m = jnp.maximum(floor, jnp.max(s, axis=0, keepdims=True))
            acc_scr[hd] = jnp.exp2(floor - m) * acc
            m_scr[hd] = jnp.broadcast_to(m, m_scr.shape[1:])

    @pl.when((ki == 0) & (safe != 1))
    def _seed_from_meta_keys():
        for hd in range(N_Q_HEADS):
            s = meta_scores_t(hd)
            m = jnp.max(s, axis=0, keepdims=True)
            p = jnp.exp2(s - m).astype(BF16)
            acc_scr[hd] = jnp.dot(values_t(hd, vtmeta_ref), p, preferred_element_type=F32)
            m_scr[hd] = jnp.broadcast_to(m, m_scr.shape[1:])

    @pl.when(safe == 1)
    def _block_against_previous_maximum():
        s_next = scores_t(0, k_ref)
        for hd in range(N_Q_HEADS):
            m_old = m_scr[hd][:1]
            s = s_next
            if hd + 1 < N_Q_HEADS:
                s_next = scores_t(hd + 1, k_ref)
            p = jnp.exp2(s - m_old).astype(BF16)
            pv = jnp.dot(values_t(hd, vt_ref), p, preferred_element_type=F32)
            m_new = jnp.maximum(m_old, jnp.max(s, axis=0, keepdims=True))
            acc_scr[hd] = jnp.exp2(m_old - m_new) * (acc_scr[hd] + pv)
            m_scr[hd] = jnp.broadcast_to(m_new, m_scr.shape[1:])

    @pl.when(safe != 1)
    def _block_against_own_maximum():
        for hd in range(N_Q_HEADS):
            m_old = m_scr[hd][:1]
            s = scores_t(hd, k_ref)
            m_new = jnp.maximum(m_old, jnp.max(s, axis=0, keepdims=True))
            p = jnp.exp2(s - m_new).astype(BF16)
            pv = jnp.dot(values_t(hd, vt_ref), p, preferred_element_type=F32)
            acc_scr[hd] = jnp.exp2(m_old - m_new) * acc_scr[hd] + pv
            m_scr[hd] = jnp.broadcast_to(m_new, m_scr.shape[1:])

    @pl.when(ki == pl.num_programs(2) - 1)
    def _normalise():
        def normalised(hd):
            acc = acc_scr[hd]
            return acc[:HEAD_DIM] / acc[HEAD_DIM:HEAD_DIM + 1]

        for pair in range(Q_PER_KV):
            both_t = jnp.concatenate([normalised(pair), normalised(Q_PER_KV + pair)], axis=0)
            o_ref[:, pair * LANES:(pair + 1) * LANES] = both_t.T.astype(BF16)


def _attn_safe_blocks(norms, norms_meta, batch, nq, nk):
    def block_max(sq, blocks):
        return jnp.sqrt(jnp.max(sq.reshape(blocks, -1), axis=1))

    q_norm = block_max(norms[:, 0, 0], batch * nq).reshape(batch, nq, 1)
    k_norm = block_max(norms[:, 1, 0], batch * nk).reshape(batch, 1, nk)
    k_meta_norm = jnp.sqrt(jnp.max(norms_meta[:, 1, 0]))
    gap = q_norm * (jnp.maximum(k_norm, k_meta_norm) + k_meta_norm)
    safe = (gap <= ATTN_SAFE_LOG2).astype(jnp.int32).reshape(-1)
    floor = (-q_norm * k_meta_norm).reshape(-1)
    return safe, floor


def _attn_call(qt, k, vt, k_meta, vt_meta, safe, floor, batch, q_rows, k_rows):
    tokens = k.shape[0]
    n_tok = tokens // batch
    nq = n_tok // q_rows
    nk = n_tok // k_rows
    grid_spec = pltpu.PrefetchScalarGridSpec(
        num_scalar_prefetch=2,
        grid=(batch, nq, nk),
        in_specs=[
            pl.BlockSpec((Q_PAD, q_rows), lambda b, i, j, *_: (0, b * nq + i)),
            pl.BlockSpec((k_rows, KV_WIDTH), lambda b, i, j, *_: (b * nk + j, 0)),
            pl.BlockSpec((KV_WIDTH, k_rows), lambda b, i, j, *_: (0, b * nk + j)),
            pl.BlockSpec(k_meta.shape, lambda b, i, j, *_: (0, 0)),
            pl.BlockSpec(vt_meta.shape, lambda b, i, j, *_: (0, 0)),
        ],
        out_specs=pl.BlockSpec((q_rows, ATTN_WIDTH), lambda b, i, j, *_: (b * nq + i, 0)),
        scratch_shapes=[pltpu.VMEM((N_Q_HEADS, 8, q_rows), F32),
                        pltpu.VMEM((N_Q_HEADS, ATTN_V_ROWS, q_rows), F32)],
    )
    return pl.pallas_call(
        _attn_kernel,
        grid_spec=grid_spec,
        out_shape=jax.ShapeDtypeStruct((tokens, ATTN_WIDTH), BF16),
        compiler_params=pltpu.CompilerParams(dimension_semantics=("parallel", "parallel", "arbitrary"),
                                             vmem_limit_bytes=VMEM_LIMIT_BYTES),
        name="attn",
    )(safe, floor, qt, k, vt, k_meta, vt_meta)


def _split_hi_lo(x):
    hi = x.astype(BF16)
    lo = (x - hi.astype(F32)).astype(BF16)
    return hi, lo


def _gate_cumsum(gates, causal):
    t = gates.shape[0]
    row = lax.broadcasted_iota(jnp.int32, (t, t), 0)
    col = lax.broadcasted_iota(jnp.int32, (t, t), 1)
    mask = (col <= row) if causal else (col >= row)
    tri = jnp.where(mask, 1.0, 0.0).astype(BF16)
    hi, lo = _split_hi_lo(gates)
    sums = jnp.dot(tri, hi, preferred_element_type=F32) + jnp.dot(tri, lo, preferred_element_type=F32)
    return sums, mask


def _values_t_with_ones(vt):
    row = lax.broadcasted_iota(jnp.int32, (STATE_ROWS - M_HEAD_DIM, vt.shape[1]), 0)
    return jnp.concatenate([vt, jnp.where(row == 0, 1.0, 0.0).astype(vt.dtype)], axis=0)


def _state_update(state_t, m_in, k, vaug_t, li_row, b_row, b_tot):
    w_end = b_tot - b_row + li_row
    m_new = jnp.maximum(b_tot + m_in, jnp.max(w_end, axis=1, keepdims=True))
    vw = (vaug_t.astype(F32) * jnp.exp(w_end - m_new)).astype(BF16)
    return jnp.exp(b_tot + m_in - m_new) * state_t + jnp.dot(vw, k, preferred_element_type=F32), m_new


def _mlstm_kernel(qtf_ref, vtf_ref, kf_ref, gf_ref, qtb_ref, vtb_ref, kb_ref, gb_ref, kmeta_ref, vtmeta_ref,
                  gmeta_ref, hf_ref, hb_ref, state_scr, m_scr):
    c = pl.program_id(1)

    @pl.when(c == 0)
    def _reset_and_absorb_meta():
        state_scr[...] = jnp.zeros_like(state_scr)
        m_scr[...] = jnp.zeros_like(m_scr)
        gates = gmeta_ref[...]
        row = lax.broadcasted_iota(jnp.int32, gates.shape, 0)
        lane = lax.broadcasted_iota(jnp.int32, gates.shape, 1)
        gates = jnp.where(row < N_META, gates, jnp.where(lane < 2 * M_HEADS, NEG, 0.0))
        sums, _ = _gate_cumsum(gates, causal=True)
        gates_t = gates.T
        sums_t = sums.T
        last = gates.shape[0] - 1
        for hd in range(M_HEADS):
            f_lane = 2 * M_HEADS + hd
            sl = slice(hd * M_HEAD_DIM, (hd + 1) * M_HEAD_DIM)
            vt = vtmeta_ref[M_WIDTH + hd * M_HEAD_DIM:M_WIDTH + (hd + 1) * M_HEAD_DIM, :]
            new_state, m_new = _state_update(
                state_scr[hd], m_scr[hd][:1, :1], kmeta_ref[:, sl], _values_t_with_ones(vt),
                gates_t[hd:hd + 1, :], sums_t[f_lane:f_lane + 1, :], sums[last:last + 1, f_lane:f_lane + 1])
            state_scr[hd] = new_state
            m_scr[hd] = jnp.broadcast_to(m_new, m_scr.shape[1:])

    chains = []
    for causal, qt_ref, vt_ref, k_ref, g_ref, o_ref in ((True, qtf_ref, vtf_ref, kf_ref, gf_ref, hf_ref),
                                                        (False, qtb_ref, vtb_ref, kb_ref, gb_ref, hb_ref)):
        gates = g_ref[...]
        t = gates.shape[0]
        sums, _ = _gate_cumsum(gates, causal)
        gates_t = gates.T
        sums_t = sums.T
        src = lax.broadcasted_iota(jnp.int32, (t, t), 0)
        tgt = lax.broadcasted_iota(jnp.int32, (t, t), 1)
        feeds = (src <= tgt) if causal else (src >= tgt)
        last = t - 1 if causal else 0
        for hd in range(M_HEADS):
            chain = hd if causal else M_HEADS + hd
            i_lane, f_lane = chain, 2 * M_HEADS + chain
            chains.append(dict(
                chain=chain, feeds=feeds, o_ref=o_ref, qt_ref=qt_ref, vt_ref=vt_ref, k_ref=k_ref,
                sl=slice(hd * M_HEAD_DIM, (hd + 1) * M_HEAD_DIM), t=t,
                r_col=gates[:, i_lane:i_lane + 1] - sums[:, f_lane:f_lane + 1],
                li_row=gates_t[i_lane:i_lane + 1, :], b_row=sums_t[f_lane:f_lane + 1, :],
                b_tot=sums[last:last + 1, f_lane:f_lane + 1], m_in=m_scr[chain][:1, :1]))

    for ch in chains:
        lhs = jnp.concatenate([ch["k_ref"][:, ch["sl"]], state_scr[ch["chain"]].astype(BF16)], axis=0)
        both = jnp.dot(lhs, ch["qt_ref"][ch["sl"], :], preferred_element_type=F32)
        ch["kq"], ch["inter"] = both[:ch["t"]], both[ch["t"]:]
    for ch in chains:
        ch["r"] = jnp.where(ch["feeds"], ch["r_col"], NEG)
        ch["g"] = jnp.maximum(jnp.max(ch["r"], axis=0, keepdims=True), ch["m_in"])
    for ch in chains:
        ch["s"] = (ch["kq"] * jnp.exp(ch["r"] - ch["g"])).astype(BF16)
        ch["vaug"] = _values_t_with_ones(ch["vt_ref"][ch["sl"], :])
        w_end = ch["b_tot"] - ch["b_row"] + ch["li_row"]
        ch["m_new"] = jnp.maximum(ch["b_tot"] + ch["m_in"], jnp.max(w_end, axis=1, keepdims=True))
        ch["vw"] = (ch["vaug"].astype(F32) * jnp.exp(w_end - ch["m_new"])).astype(BF16)
    for ch in chains:
        ch["tot"] = jnp.dot(ch["vaug"], ch["s"], preferred_element_type=F32)
        ch["local"] = jnp.dot(ch["vw"], ch["k_ref"][:, ch["sl"]], preferred_element_type=F32)
    for ch in chains:
        tot = ch["tot"] + jnp.exp(ch["m_in"] - ch["g"]) * ch["inter"]
        den = jnp.maximum(jnp.abs(tot[M_HEAD_DIM:M_HEAD_DIM + 1]), jnp.exp(-(ch["b_row"] + ch["g"])))
        ch["o_ref"][ch["sl"], :] = tot[:M_HEAD_DIM] / den
        chain = ch["chain"]
        state_scr[chain] = jnp.exp(ch["b_tot"] + ch["m_in"] - ch["m_new"]) * state_scr[chain] + ch["local"]
        m_scr[chain] = jnp.broadcast_to(ch["m_new"], m_scr.shape[1:])


def _mlstm_call(mt, m, gates, mt_meta, m_meta, gates_meta, batch, chunk):
    tokens = m.shape[0]
    nc = tokens // batch // chunk

    def chunk_of(direction):
        return (lambda b, c: b * nc + c) if direction == "fwd" else (lambda b, c: b * nc + nc - 1 - c)

    def specs(direction):
        at = chunk_of(direction)
        return [pl.BlockSpec((M_WIDTH, chunk), lambda b, c: (0, at(b, c))),
                pl.BlockSpec((M_WIDTH, chunk), lambda b, c: (1, at(b, c))),
                pl.BlockSpec((chunk, M_WIDTH), lambda b, c: (at(b, c), 0)),
                pl.BlockSpec((chunk, LANES), lambda b, c: (at(b, c), 0))]

    def out_spec(direction):
        at = chunk_of(direction)
        return pl.BlockSpec((M_WIDTH, chunk), lambda b, c: (0, at(b, c)))

    return pl.pallas_call(
        _mlstm_kernel,
        grid=(batch, nc),
        in_specs=specs("fwd") + specs("bwd") + [pl.BlockSpec(m_meta.shape, lambda b, c: (0, 0)),
                                                pl.BlockSpec(mt_meta.shape, lambda b, c: (0, 0)),
                                                pl.BlockSpec(gates_meta.shape, lambda b, c: (0, 0))],
        out_specs=[out_spec("fwd"), out_spec("bwd")],
        out_shape=[jax.ShapeDtypeStruct((M_WIDTH, tokens), F32)] * 2,
        scratch_shapes=[pltpu.VMEM((2 * M_HEADS, STATE_ROWS, M_HEAD_DIM), F32),
                        pltpu.VMEM((2 * M_HEADS, 8, LANES), F32)],
        compiler_params=pltpu.CompilerParams(dimension_semantics=("parallel", "arbitrary"),
                                             vmem_limit_bytes=VMEM_LIMIT_BYTES),
        name="mlstm",
    )(mt, mt, m, gates, mt, mt, m, gates, m_meta, mt_meta, gates_meta)


def _back_kernel(h_ref, ao_ref, hf_ref, hb_ref, mo_ref, merge_ref, mgain_ref, wab_ref, wmb_ref, wout_ref,
                 g2_ref, w1_ref, w3_ref, w2_ref, y_ref):
    rows, d_model = h_ref.shape
    for r in range(0, rows, SUB_ROWS):
        rs = slice(r, r + SUB_ROWS)
        a_out = jnp.dot(ao_ref[rs, :], wab_ref[...], preferred_element_type=F32)

        hm = (hf_ref[:, rs] + hb_ref[:, rs]).T
        o_gate = jax.nn.sigmoid(mo_ref[rs, :].astype(F32))
        gated = []
        for hd in range(M_HEADS):
            sl = slice(hd * M_HEAD_DIM, (hd + 1) * M_HEAD_DIM)
            gated.append((_rmsnorm(hm[:, sl], mgain_ref[:, sl]) * o_gate[:, sl]).astype(BF16))
        m_out = jnp.dot(jnp.concatenate(gated, axis=1), wmb_ref[...], preferred_element_type=F32)

        merged = (merge_ref[rs, :d_model].astype(F32) * a_out + merge_ref[rs, d_model:].astype(F32) * m_out)
        h2 = h_ref[rs, :] + jnp.dot(merged.astype(BF16), wout_ref[...], preferred_element_type=F32)
        y_ref[rs, :] = _swiglu_half_residual(h2, g2_ref[...], w1_ref, w3_ref, w2_ref)


def _back_call(h, attn_o, hf, hb, m, merge, wts, rows):
    tokens, d_model = h.shape

    def row_spec(width, col=0):
        return pl.BlockSpec((rows, width), lambda i: (i, col))

    col_spec = pl.BlockSpec((M_WIDTH, rows), lambda i: (0, i))
    consts = (wts["mgain"], wts["wab"], wts["wmb"], wts["wout"], wts["g2"], wts["w1b"], wts["w3b"], wts["w2b"])
    return pl.pallas_call(
        _back_kernel,
        grid=(tokens // rows,),
        in_specs=[row_spec(d_model), row_spec(ATTN_WIDTH), col_spec, col_spec,
                  row_spec(M_WIDTH, 1), row_spec(2 * d_model)] + [_const_spec(c.shape) for c in consts],
        out_specs=row_spec(d_model),
        out_shape=jax.ShapeDtypeStruct((tokens, d_model), F32),
        compiler_params=pltpu.CompilerParams(dimension_semantics=("parallel",),
                                             vmem_limit_bytes=VMEM_LIMIT_BYTES),
        name="back",
    )(h, attn_o, hf, hb, m, merge, *consts)


def _pair_heads(w):
    lead = w.shape[:-1]
    w = w.reshape(*lead, N_KV_HEADS, Q_PER_KV, HEAD_DIM)
    return jnp.swapaxes(w, -3, -2).reshape(*lead, ATTN_WIDTH)


def _prepare_weights(g_ffn1, w1_ffn1, w3_ffn1, w2_ffn1, g_mix, w_in, b_i, b_f, q_gain, k_gain, m_gain,
                     w_attn_br, w_mlstm_br, w_out, g_ffn2, w1_ffn2, w3_ffn2, w2_ffn2):
    d_model = w_in.shape[1]
    attn_w = N_Q_HEADS * HEAD_DIM
    kv_w = N_KV_HEADS * HEAD_DIM
    splits = (attn_w, kv_w, kv_w, M_WIDTH, M_WIDTH, M_WIDTH, M_WIDTH, 2 * M_HEADS, 2 * M_HEADS, d_model, d_model)
    offs = np.cumsum((0,) + splits)
    assert offs[-1] == w_in.shape[2]
    cols = [w_in[0, :, offs[i]:offs[i + 1]] for i in range(len(splits))]
    aq, ak, av, mq, mk, mv, mo, gi, gf, ga, gm = cols
    gate_w = jnp.pad(jnp.concatenate([gi, gf], axis=1), ((0, 0), (0, LANES - 4 * M_HEADS)))
    gate_b = jnp.pad(jnp.concatenate([b_i[0], b_f[0]]), (0, LANES - 4 * M_HEADS))[None]
    row = lambda g: g.astype(F32)[None]
    both_halves = lambda g: jnp.tile(g[0].astype(F32), LANES // HEAD_DIM)[None]
    return {
        "g1": row(g_ffn1[0]), "w1a": w1_ffn1[0].astype(BF16), "w3a": w3_ffn1[0].astype(BF16),
        "w2a": w2_ffn1[0].astype(BF16),
        "gmix": row(g_mix[0]),
        "wq": _pair_heads(aq).astype(BF16), "wkv": jnp.concatenate([ak, av], axis=1).astype(BF16),
        "wm": jnp.concatenate([mq, mk, mv, mo], axis=1).astype(BF16),
        "wg": gate_w.astype(BF16), "gbias": gate_b.astype(F32),
        "wmerge": jnp.concatenate([ga, gm], axis=1).astype(BF16),
        "qgain": both_halves(q_gain), "kgain": both_halves(k_gain),
        "mgain": m_gain[0].astype(F32).reshape(1, M_WIDTH),
        "wab": _pair_heads(w_attn_br[0].T).T.astype(BF16), "wmb": w_mlstm_br[0].astype(BF16), "wout": w_out[0].astype(BF16),
        "g2": row(g_ffn2[0]), "w1b": w1_ffn2[0].astype(BF16), "w3b": w3_ffn2[0].astype(BF16),
        "w2b": w2_ffn2[0].astype(BF16),
    }


def _rope_tables(n_tok):
    t = jnp.arange(n_tok)
    pos = jnp.stack([(t // GRID_W).astype(F32), (t % GRID_W).astype(F32)], axis=-1)
    inv_freq = ROPE_THETA ** (-2.0 * jnp.arange(ROPE_FREQS, dtype=F32) / ROPE_AXIS_DIM)
    ang = pos[:, :, None] * inv_freq
    cos, sin = jnp.cos(ang), jnp.sin(ang)
    cos = jnp.stack([cos, cos], axis=2).reshape(n_tok, HEAD_DIM)
    sin = jnp.stack([-sin, sin], axis=2).reshape(n_tok, HEAD_DIM)
    reps = (1, LANES // HEAD_DIM)
    return jnp.tile(cos, reps), jnp.tile(sin, reps)


def _trunk(x, meta_parts, wts):
    batch, n_tok, d_model = x.shape
    k_meta, v_meta, mt_meta, m_meta, gates_meta, norms_meta = meta_parts
    cos, sin = _rope_tables(n_tok)
    h, q, k, v, mt, m, gates, merge, norms = _front_call(
        x.reshape(batch * n_tok, d_model), cos, sin, wts, FRONT_ROWS)
    q_rows, k_rows = min(ATTN_Q_ROWS, n_tok), min(ATTN_K_ROWS, n_tok)
    safe, floor = _attn_safe_blocks(norms, norms_meta, batch, n_tok // q_rows, n_tok // k_rows)
    attn_o = _attn_call(q, k, v, k_meta, v_meta, safe, floor, batch, q_rows, k_rows)
    hf, hb = _mlstm_call(mt, m, gates, mt_meta, m_meta, gates_meta, batch, min(MLSTM_CHUNK, n_tok))
    y = _back_call(h, attn_o, hf, hb, m, merge, wts, BACK_ROWS)
    return y.reshape(batch, n_tok, d_model)


def kernel(x_prompt, x_sample, meta, g_ffn1, w1_ffn1, w3_ffn1, w2_ffn1, g_mix, w_in, b_i, b_f, q_gain, k_gain,
           m_gain, w_attn_br, w_mlstm_br, w_out, g_ffn2, w1_ffn2, w3_ffn2, w2_ffn2):
    assert w_in.shape[0] == 1, "single-layer trunk: the meta rows' mixer outputs are never consumed"
    assert meta.shape[0] == N_META
    wts = _prepare_weights(g_ffn1, w1_ffn1, w3_ffn1, w2_ffn1, g_mix, w_in, b_i, b_f, q_gain, k_gain, m_gain,
                           w_attn_br, w_mlstm_br, w_out, g_ffn2, w1_ffn2, w3_ffn2, w2_ffn2)
    ones = jnp.ones((META_ROWS, LANES), F32)
    meta_rows = jnp.pad(meta.astype(F32), ((0, META_ROWS - N_META), (0, 0)))
    _, _, k_meta, v_meta, mt_meta, m_meta, gates_meta, _, norms_meta = _front_call(
        meta_rows, ones, jnp.zeros_like(ones), wts, META_ROWS)
    meta_parts = (k_meta, v_meta, mt_meta, m_meta, gates_meta, norms_meta)
    return (_trunk(x_prompt, meta_parts, wts), _trunk(x_sample, meta_parts, wts))
```

```python
import jax
import jax.numpy as jnp
import numpy as np
from jax import lax
from jax.experimental import pallas as pl
from jax.experimental.pallas import tpu as pltpu

F32 = jnp.float32
BF16 = jnp.bfloat16

N_META = 16
GRID_W = 64
EPS = 1e-6
NEG = -1e30
N_Q_HEADS = 8
N_KV_HEADS = 2
Q_PER_KV = N_Q_HEADS // N_KV_HEADS
HEAD_DIM = 64
ROPE_AXIS_DIM = HEAD_DIM // 2
ROPE_FREQS = ROPE_AXIS_DIM // 2
ROPE_THETA = 10000.0
M_HEADS = 4
M_HEAD_DIM = 128
M_WIDTH = M_HEADS * M_HEAD_DIM

LANES = 128
BF16_ROWS = 16
VMEM_LIMIT_BYTES = 60 * 1024 * 1024

SUB_ROWS = 256
FRONT_ROWS = 2 * SUB_ROWS
BACK_ROWS = 2 * SUB_ROWS
ATTN_Q_ROWS = 1024
ATTN_K_ROWS = 4096
MLSTM_CHUNK = 256

Q_PAD = N_Q_HEADS * LANES
ATTN_WIDTH = N_Q_HEADS * HEAD_DIM
KV_WIDTH = N_KV_HEADS * HEAD_DIM
assert KV_WIDTH == LANES and HEAD_DIM * 2 == LANES
Q_SCALE = HEAD_DIM ** -0.5 * 1.4426950408889634
ATTN_SAFE_LOG2 = 90.0
ATTN_V_ROWS = HEAD_DIM + BF16_ROWS
META_ROWS = LANES
STATE_ROWS = M_HEAD_DIM + BF16_ROWS


def _const_spec(shape):
    zeros = (0,) * len(shape)
    return pl.BlockSpec(shape, lambda *_: zeros, pipeline_mode=pl.Buffered(1))


def _rmsnorm(x, g):
    return x * lax.rsqrt(jnp.mean(x * x, axis=-1, keepdims=True) + EPS) * g


def _swiglu_half_residual(x, g, w1_ref, w3_ref, w2_ref):
    u = _rmsnorm(x, g).astype(BF16)
    a = jnp.dot(u, w1_ref[...], preferred_element_type=F32)
    b = jnp.dot(u, w3_ref[...], preferred_element_type=F32)
    act = (a * jax.nn.sigmoid(a) * b).astype(BF16)
    return x + 0.5 * jnp.dot(act, w2_ref[...], preferred_element_type=F32)


def _log_sigmoid(x):
    return jnp.minimum(x, 0.0) - jnp.log1p(jnp.exp(-jnp.abs(x)))


def _half_sums(x, low_half):
    low = jnp.sum(jnp.where(low_half, x, 0.0), axis=-1, keepdims=True)
    high = jnp.sum(jnp.where(low_half, 0.0, x), axis=-1, keepdims=True)
    return low, high


def _head_pair_norm_rope(x, gain, cos, sin_signed, low_half, first_half):
    ms_low, ms_high = _half_sums(x * x, low_half)
    inv = jnp.where(low_half, lax.rsqrt(ms_low * (1.0 / HEAD_DIM) + EPS), lax.rsqrt(ms_high * (1.0 / HEAD_DIM) + EPS))
    y = x * inv * gain
    partner = jnp.where(first_half, pltpu.roll(y, LANES - ROPE_FREQS, 1), pltpu.roll(y, ROPE_FREQS, 1))
    return y * cos + partner * sin_signed


def _front_kernel(x_ref, cos_ref, sin_ref, g1_ref, w1_ref, w3_ref, w2_ref, gmix_ref, wq_ref, wkv_ref,
                  wm_ref, wg_ref, wmerge_ref, qgain_ref, kgain_ref, gbias_ref,
                  h_ref, q_ref, k_ref, v_ref, mt_ref, m_ref, gate_ref, merge_ref, norm_ref):
    sub = min(x_ref.shape[0], SUB_ROWS)
    lane = lax.broadcasted_iota(jnp.int32, (sub, LANES), 1)
    low_half = lane < HEAD_DIM
    first_half = (lane % ROPE_AXIS_DIM) < ROPE_FREQS

    def sq_norms(y_bf16):
        y = y_bf16.astype(F32)
        return jnp.maximum(*_half_sums(y * y, low_half))

    q_sq = jnp.zeros((sub, 1), F32)
    k_sq = jnp.zeros((sub, 1), F32)
    for r in range(0, x_ref.shape[0], sub):
        rs = slice(r, r + sub)
        h = _swiglu_half_residual(x_ref[rs, :], g1_ref[...], w1_ref, w3_ref, w2_ref)
        h_ref[rs, :] = h
        u = _rmsnorm(h, gmix_ref[...]).astype(BF16)
        cos = cos_ref[rs, :]
        sin = sin_ref[rs, :]

        zq = jnp.dot(u, wq_ref[...], preferred_element_type=F32)
        for pair in range(Q_PER_KV):
            y = _head_pair_norm_rope(zq[:, pair * LANES:(pair + 1) * LANES], qgain_ref[...], cos, sin,
                                     low_half, first_half) * Q_SCALE
            q_sq = jnp.maximum(q_sq, sq_norms(y.astype(BF16)))
            for group, keep in enumerate((low_half, ~low_half)):
                hd = group * Q_PER_KV + pair
                q_ref[hd * LANES:(hd + 1) * LANES, rs] = jnp.where(keep, y, 0.0).T.astype(BF16)

        zkv = jnp.dot(u, wkv_ref[...], preferred_element_type=F32)
        k = _head_pair_norm_rope(zkv[:, :LANES], kgain_ref[...], cos, sin, low_half, first_half).astype(BF16)
        k_ref[rs, :] = k
        k_sq = jnp.maximum(k_sq, sq_norms(k))
        v_ref[:, rs] = zkv[:, LANES:].T.astype(BF16)

        zm = jnp.dot(u, wm_ref[...], preferred_element_type=F32)
        mt_ref[:M_WIDTH, rs] = zm[:, :M_WIDTH].T.astype(BF16)
        mt_ref[M_WIDTH:, rs] = zm[:, 2 * M_WIDTH:3 * M_WIDTH].T.astype(BF16)
        m_ref[rs, :M_WIDTH] = (zm[:, M_WIDTH:2 * M_WIDTH] * (M_HEAD_DIM ** -0.5)).astype(BF16)
        m_ref[rs, M_WIDTH:] = zm[:, 3 * M_WIDTH:].astype(BF16)

        zg = jnp.dot(u, wg_ref[...], preferred_element_type=F32) + gbias_ref[...]
        gate_ref[rs, :] = jnp.where(lane < 2 * M_HEADS, zg, _log_sigmoid(zg))

        zmerge = jnp.dot(u, wmerge_ref[...], preferred_element_type=F32)
        merge_ref[rs, :] = jax.nn.sigmoid(zmerge).astype(BF16)

    tile_row = lax.broadcasted_iota(jnp.int32, norm_ref.shape, 0)
    norm_ref[...] = jnp.where(tile_row == 0, jnp.max(q_sq, axis=0, keepdims=True),
                              jnp.max(k_sq, axis=0, keepdims=True))


def _front_call(x, cos, sin, wts, rows):
    tokens, d_model = x.shape
    steps = tokens // rows
    table_steps = cos.shape[0] // rows

    def row_spec(width):
        return pl.BlockSpec((rows, width), lambda i: (i, 0))

    table_spec = pl.BlockSpec((rows, LANES), lambda i: (i % table_steps, 0))
    consts = (wts["g1"], wts["w1a"], wts["w3a"], wts["w2a"], wts["gmix"], wts["wq"], wts["wkv"],
              wts["wm"], wts["wg"], wts["wmerge"], wts["qgain"], wts["kgain"], wts["gbias"])

    def col_spec(height):
        return pl.BlockSpec((height, rows), lambda i: (0, i))

    out_specs = [row_spec(d_model), col_spec(Q_PAD), row_spec(KV_WIDTH), col_spec(KV_WIDTH), col_spec(2 * M_WIDTH),
                 row_spec(2 * M_WIDTH), row_spec(LANES), row_spec(2 * d_model),
                 pl.BlockSpec((None, 8, LANES), lambda i: (i, 0, 0))]
    out_shape = [jax.ShapeDtypeStruct((tokens, d_model), F32), jax.ShapeDtypeStruct((Q_PAD, tokens), BF16),
                 jax.ShapeDtypeStruct((tokens, KV_WIDTH), BF16), jax.ShapeDtypeStruct((KV_WIDTH, tokens), BF16),
                 jax.ShapeDtypeStruct((2 * M_WIDTH, tokens), BF16),
                 jax.ShapeDtypeStruct((tokens, 2 * M_WIDTH), BF16), jax.ShapeDtypeStruct((tokens, LANES), F32),
                 jax.ShapeDtypeStruct((tokens, 2 * d_model), BF16), jax.ShapeDtypeStruct((steps, 8, LANES), F32)]
    return pl.pallas_call(
        _front_kernel,
        grid=(steps,),
        in_specs=[row_spec(d_model), table_spec, table_spec] + [_const_spec(c.shape) for c in consts],
        out_specs=out_specs,
        out_shape=out_shape,
        compiler_params=pltpu.CompilerParams(dimension_semantics=("parallel",),
                                             vmem_limit_bytes=VMEM_LIMIT_BYTES),
        name="front",
    )(x, cos, sin, *consts)


def _attn_kernel(safe_ref, floor_ref, qt_ref, k_ref, vt_ref, kmeta_ref, vtmeta_ref, o_ref, m_scr, acc_scr):
    ki = pl.program_id(2)
    q_cols = qt_ref.shape[1]
    step = (pl.program_id(0) * pl.num_programs(1) + pl.program_id(1)) * pl.num_programs(2) + ki
    safe = safe_ref[step]

    def scores_t(hd, keys_ref):
        return jnp.dot(keys_ref[...], qt_ref[hd * LANES:(hd + 1) * LANES, :], preferred_element_type=F32)

    def values_t(hd, values_ref):
        g = hd // Q_PER_KV
        row = lax.broadcasted_iota(jnp.int32, (ATTN_V_ROWS - HEAD_DIM, values_ref.shape[1]), 0)
        ones = jnp.where(row == 0, 1.0, 0.0).astype(BF16)
        return jnp.concatenate([values_ref[g * HEAD_DIM:(g + 1) * HEAD_DIM, :], ones], axis=0)

    def meta_scores_t(hd):
        real = lax.broadcasted_iota(jnp.int32, (kmeta_ref.shape[0], q_cols), 0) < N_META
        return jnp.where(real, scores_t(hd, kmeta_ref), NEG)

    @pl.when((ki == 0) & (safe == 1))
    def _seed_from_meta_keys_against_floor():
        floor = jnp.full((1, q_cols), floor_ref[pl.program_id(0) * pl.num_programs(1) + pl.program_id(1)], F32)
        scores = [jnp.dot(kmeta_ref[:N_META, :], qt_ref[hd * LANES:(hd + 1) * LANES, :],
                          preferred_element_type=F32) for hd in range(N_Q_HEADS)]
        for hd, s in enumerate(scores):
            p = jnp.exp2(s - floor).astype(BF16)
            acc = jnp.dot(values_t(hd, vtmeta_ref)[:, :N_META], p, preferred_element_type=F32)
            m = jnp.maximum(floor, jnp.max(s, axis=0, keepdims=True))
            acc_scr[hd] = jnp.exp2(floor - m) * acc
            m_scr[hd] = jnp.broadcast_to(m, m_scr.shape[1:])

    @pl.when((ki == 0) & (safe != 1))
    def _seed_from_meta_keys():
        for hd in range(N_Q_HEADS):
            s = meta_scores_t(hd)
            m = jnp.max(s, axis=0, keepdims=True)
            p = jnp.exp2(s - m).astype(BF16)
            acc_scr[hd] = jnp.dot(values_t(hd, vtmeta_ref), p, preferred_element_type=F32)
            m_scr[hd] = jnp.broadcast_to(m, m_scr.shape[1:])

    @pl.when(safe == 1)
    def _block_against_previous_maximum():
        s_next = scores_t(0, k_ref)
        for hd in range(N_Q_HEADS):
            m_old = m_scr[hd][:1]
            s = s_next
            if hd + 1 < N_Q_HEADS:
                s_next = scores_t(hd + 1, k_ref)
            p = jnp.exp2(s - m_old).astype(BF16)
            pv = jnp.dot(values_t(hd, vt_ref), p, preferred_element_type=F32)
            m_new = jnp.maximum(m_old, jnp.max(s, axis=0, keepdims=True))
            acc_scr[hd] = jnp.exp2(m_old - m_new) * (acc_scr[hd] + pv)
            m_scr[hd] = jnp.broadcast_to(m_new, m_scr.shape[1:])

    @pl.when(safe != 1)
    def _block_against_own_maximum():
        for hd in range(N_Q_HEADS):
            m_old = m_scr[hd][:1]
            s = scores_t(hd, k_ref)
            m_new = jnp.maximum(m_old, jnp.max(s, axis=0, keepdims=True))
            p = jnp.exp2(s - m_new).astype(BF16)
            pv = jnp.dot(values_t(hd, vt_ref), p, preferred_element_type=F32)
            acc_scr[hd] = jnp.exp2(m_old - m_new) * acc_scr[hd] + pv
            m_scr[hd] = jnp.broadcast_to(m_new, m_scr.shape[1:])

    @pl.when(ki == pl.num_programs(2) - 1)
    def _normalise():
        def normalised(hd):
            acc = acc_scr[hd]
            return acc[:HEAD_DIM] / acc[HEAD_DIM:HEAD_DIM + 1]

        for pair in range(Q_PER_KV):
            both_t = jnp.concatenate([normalised(pair), normalised(Q_PER_KV + pair)], axis=0)
            o_ref[:, pair * LANES:(pair + 1) * LANES] = both_t.T.astype(BF16)


def _attn_safe_blocks(norms, norms_meta, batch, nq, nk):
    def block_max(sq, blocks):
        return jnp.sqrt(jnp.max(sq.reshape(blocks, -1), axis=1))

    q_norm = block_max(norms[:, 0, 0], batch * nq).reshape(batch, nq, 1)
    k_norm = block_max(norms[:, 1, 0], batch * nk).reshape(batch, 1, nk)
    k_meta_norm = jnp.sqrt(jnp.max(norms_meta[:, 1, 0]))
    gap = q_norm * (jnp.maximum(k_norm, k_meta_norm) + k_meta_norm)
    safe = (gap <= ATTN_SAFE_LOG2).astype(jnp.int32).reshape(-1)
    floor = (-q_norm * k_meta_norm).reshape(-1)
    return safe, floor


def _attn_call(qt, k, vt, k_meta, vt_meta, safe, floor, batch, q_rows, k_rows):
    tokens = k.shape[0]
    n_tok = tokens // batch
    nq = n_tok // q_rows
    nk = n_tok // k_rows
    grid_spec = pltpu.PrefetchScalarGridSpec(
        num_scalar_prefetch=2,
        grid=(batch, nq, nk),
        in_specs=[
            pl.BlockSpec((Q_PAD, q_rows), lambda b, i, j, *_: (0, b * nq + i)),
            pl.BlockSpec((k_rows, KV_WIDTH), lambda b, i, j, *_: (b * nk + j, 0)),
            pl.BlockSpec((KV_WIDTH, k_rows), lambda b, i, j, *_: (0, b * nk + j)),
            pl.BlockSpec(k_meta.shape, lambda b, i, j, *_: (0, 0)),
            pl.BlockSpec(vt_meta.shape, lambda b, i, j, *_: (0, 0)),
        ],
        out_specs=pl.BlockSpec((q_rows, ATTN_WIDTH), lambda b, i, j, *_: (b * nq + i, 0)),
        scratch_shapes=[pltpu.VMEM((N_Q_HEADS, 8, q_rows), F32),
                        pltpu.VMEM((N_Q_HEADS, ATTN_V_ROWS, q_rows), F32)],
    )
    return pl.pallas_call(
        _attn_kernel,
        grid_spec=grid_spec,
        out_shape=jax.ShapeDtypeStruct((tokens, ATTN_WIDTH), BF16),
        compiler_params=pltpu.CompilerParams(dimension_semantics=("parallel", "parallel", "arbitrary"),
                                             vmem_limit_bytes=VMEM_LIMIT_BYTES),
        name="attn",
    )(safe, floor, qt, k, vt, k_meta, vt_meta)


def _split_hi_lo(x):
    hi = x.astype(BF16)
    lo = (x - hi.astype(F32)).astype(BF16)
    return hi, lo


def _gate_cumsum(gates, causal):
    t = gates.shape[0]
    row = lax.broadcasted_iota(jnp.int32, (t, t), 0)
    col = lax.broadcasted_iota(jnp.int32, (t, t), 1)
    mask = (col <= row) if causal else (col >= row)
    tri = jnp.where(mask, 1.0, 0.0).astype(BF16)
    hi, lo = _split_hi_lo(gates)
    sums = jnp.dot(tri, hi, preferred_element_type=F32) + jnp.dot(tri, lo, preferred_element_type=F32)
    return sums, mask


def _values_t_with_ones(vt):
    row = lax.broadcasted_iota(jnp.int32, (STATE_ROWS - M_HEAD_DIM, vt.shape[1]), 0)
    return jnp.concatenate([vt, jnp.where(row == 0, 1.0, 0.0).astype(vt.dtype)], axis=0)


def _state_update(state_t, m_in, k, vaug_t, li_row, b_row, b_tot):
    w_end = b_tot - b_row + li_row
    m_new = jnp.maximum(b_tot + m_in, jnp.max(w_end, axis=1, keepdims=True))
    vw = (vaug_t.astype(F32) * jnp.exp(w_end - m_new)).astype(BF16)
    return jnp.exp(b_tot + m_in - m_new) * state_t + jnp.dot(vw, k, preferred_element_type=F32), m_new


def _mlstm_kernel(qtf_ref, vtf_ref, kf_ref, gf_ref, qtb_ref, vtb_ref, kb_ref, gb_ref, kmeta_ref, vtmeta_ref,
                  gmeta_ref, hf_ref, hb_ref, state_scr, m_scr):
    c = pl.program_id(1)

    @pl.when(c == 0)
    def _reset_and_absorb_meta():
        state_scr[...] = jnp.zeros_like(state_scr)
        m_scr[...] = jnp.zeros_like(m_scr)
        gates = gmeta_ref[...]
        row = lax.broadcasted_iota(jnp.int32, gates.shape, 0)
        lane = lax.broadcasted_iota(jnp.int32, gates.shape, 1)
        gates = jnp.where(row < N_META, gates, jnp.where(lane < 2 * M_HEADS, NEG, 0.0))
        sums, _ = _gate_cumsum(gates, causal=True)
        gates_t = gates.T
        sums_t = sums.T
        last = gates.shape[0] - 1
        for hd in range(M_HEADS):
            f_lane = 2 * M_HEADS + hd
            sl = slice(hd * M_HEAD_DIM, (hd + 1) * M_HEAD_DIM)
            vt = vtmeta_ref[M_WIDTH + hd * M_HEAD_DIM:M_WIDTH + (hd + 1) * M_HEAD_DIM, :]
            new_state, m_new = _state_update(
                state_scr[hd], m_scr[hd][:1, :1], kmeta_ref[:, sl], _values_t_with_ones(vt),
                gates_t[hd:hd + 1, :], sums_t[f_lane:f_lane + 1, :], sums[last:last + 1, f_lane:f_lane + 1])
            state_scr[hd] = new_state
            m_scr[hd] = jnp.broadcast_to(m_new, m_scr.shape[1:])

    chains = []
    for causal, qt_ref, vt_ref, k_ref, g_ref, o_ref in ((True, qtf_ref, vtf_ref, kf_ref, gf_ref, hf_ref),
                                                        (False, qtb_ref, vtb_ref, kb_ref, gb_ref, hb_ref)):
        gates = g_ref[...]
        t = gates.shape[0]
        sums, _ = _gate_cumsum(gates, causal)
        gates_t = gates.T
        sums_t = sums.T
        src = lax.broadcasted_iota(jnp.int32, (t, t), 0)
        tgt = lax.broadcasted_iota(jnp.int32, (t, t), 1)
        feeds = (src <= tgt) if causal else (src >= tgt)
        last = t - 1 if causal else 0
        for hd in range(M_HEADS):
            chain = hd if causal else M_HEADS + hd
            i_lane, f_lane = chain, 2 * M_HEADS + chain
            chains.append(dict(
                chain=chain, feeds=feeds, o_ref=o_ref, qt_ref=qt_ref, vt_ref=vt_ref, k_ref=k_ref,
                sl=slice(hd * M_HEAD_DIM, (hd + 1) * M_HEAD_DIM), t=t,
                r_col=gates[:, i_lane:i_lane + 1] - sums[:, f_lane:f_lane + 1],
                li_row=gates_t[i_lane:i_lane + 1, :], b_row=sums_t[f_lane:f_lane + 1, :],
                b_tot=sums[last:last + 1, f_lane:f_lane + 1], m_in=m_scr[chain][:1, :1]))

    for ch in chains:
        lhs = jnp.concatenate([ch["k_ref"][:, ch["sl"]], state_scr[ch["chain"]].astype(BF16)], axis=0)
        both = jnp.dot(lhs, ch["qt_ref"][ch["sl"], :], preferred_element_type=F32)
        ch["kq"], ch["inter"] = both[:ch["t"]], both[ch["t"]:]
    for ch in chains:
        ch["r"] = jnp.where(ch["feeds"], ch["r_col"], NEG)
        ch["g"] = jnp.maximum(jnp.max(ch["r"], axis=0, keepdims=True), ch["m_in"])
    for ch in chains:
        ch["s"] = (ch["kq"] * jnp.exp(ch["r"] - ch["g"])).astype(BF16)
        ch["vaug"] = _values_t_with_ones(ch["vt_ref"][ch["sl"], :])
        w_end = ch["b_tot"] - ch["b_row"] + ch["li_row"]
        ch["m_new"] = jnp.maximum(ch["b_tot"] + ch["m_in"], jnp.max(w_end, axis=1, keepdims=True))
        ch["vw"] = (ch["vaug"].astype(F32) * jnp.exp(w_end - ch["m_new"])).astype(BF16)
    for ch in chains:
        ch["tot"] = jnp.dot(ch["vaug"], ch["s"], preferred_element_type=F32)
        ch["local"] = jnp.dot(ch["vw"], ch["k_ref"][:, ch["sl"]], preferred_element_type=F32)
    for ch in chains:
        tot = ch["tot"] + jnp.exp(ch["m_in"] - ch["g"]) * ch["inter"]
        den = jnp.maximum(jnp.abs(tot[M_HEAD_DIM:M_HEAD_DIM + 1]), jnp.exp(-(ch["b_row"] + ch["g"])))
        ch["o_ref"][ch["sl"], :] = tot[:M_HEAD_DIM] / den
        chain = ch["chain"]
        state_scr[chain] = jnp.exp(ch["b_tot"] + ch["m_in"] - ch["m_new"]) * state_scr[chain] + ch["local"]
        m_scr[chain] = jnp.broadcast_to(ch["m_new"], m_scr.shape[1:])


def _mlstm_call(mt, m, gates, mt_meta, m_meta, gates_meta, batch, chunk):
    tokens = m.shape[0]
    nc = tokens // batch // chunk

    def chunk_of(direction):
        return (lambda b, c: b * nc + c) if direction == "fwd" else (lambda b, c: b * nc + nc - 1 - c)

    def specs(direction):
        at = chunk_of(direction)
        return [pl.BlockSpec((M_WIDTH, chunk), lambda b, c: (0, at(b, c))),
                pl.BlockSpec((M_WIDTH, chunk), lambda b, c: (1, at(b, c))),
                pl.BlockSpec((chunk, M_WIDTH), lambda b, c: (at(b, c), 0)),
                pl.BlockSpec((chunk, LANES), lambda b, c: (at(b, c), 0))]

    def out_spec(direction):
        at = chunk_of(direction)
        return pl.BlockSpec((M_WIDTH, chunk), lambda b, c: (0, at(b, c)))

    return pl.pallas_call(
        _mlstm_kernel,
        grid=(batch, nc),
        in_specs=specs("fwd") + specs("bwd") + [pl.BlockSpec(m_meta.shape, lambda b, c: (0, 0)),
                                                pl.BlockSpec(mt_meta.shape, lambda b, c: (0, 0)),
                                                pl.BlockSpec(gates_meta.shape, lambda b, c: (0, 0))],
        out_specs=[out_spec("fwd"), out_spec("bwd")],
        out_shape=[jax.ShapeDtypeStruct((M_WIDTH, tokens), F32)] * 2,
        scratch_shapes=[pltpu.VMEM((2 * M_HEADS, STATE_ROWS, M_HEAD_DIM), F32),
                        pltpu.VMEM((2 * M_HEADS, 8, LANES), F32)],
        compiler_params=pltpu.CompilerParams(dimension_semantics=("parallel", "arbitrary"),
                                             vmem_limit_bytes=VMEM_LIMIT_BYTES),
        name="mlstm",
    )(mt, mt, m, gates, mt, mt, m, gates, m_meta, mt_meta, gates_meta)


def _back_kernel(h_ref, ao_ref, hf_ref, hb_ref, mo_ref, merge_ref, mgain_ref, wab_ref, wmb_ref, wout_ref,
                 g2_ref, w1_ref, w3_ref, w2_ref, y_ref):
    rows, d_model = h_ref.shape
    for r in range(0, rows, SUB_ROWS):
        rs = slice(r, r + SUB_ROWS)
        a_out = jnp.dot(ao_ref[rs, :], wab_ref[...], preferred_element_type=F32)

        hm = (hf_ref[:, rs] + hb_ref[:, rs]).T
        o_gate = jax.nn.sigmoid(mo_ref[rs, :].astype(F32))
        gated = []
        for hd in range(M_HEADS):
            sl = slice(hd * M_HEAD_DIM, (hd + 1) * M_HEAD_DIM)
            gated.append((_rmsnorm(hm[:, sl], mgain_ref[:, sl]) * o_gate[:, sl]).astype(BF16))
        m_out = jnp.dot(jnp.concatenate(gated, axis=1), wmb_ref[...], preferred_element_type=F32)

        merged = (merge_ref[rs, :d_model].astype(F32) * a_out + merge_ref[rs, d_model:].astype(F32) * m_out)
        h2 = h_ref[rs, :] + jnp.dot(merged.astype(BF16), wout_ref[...], preferred_element_type=F32)
        y_ref[rs, :] = _swiglu_half_residual(h2, g2_ref[...], w1_ref, w3_ref, w2_ref)


def _back_call(h, attn_o, hf, hb, m, merge, wts, rows):
    tokens, d_model = h.shape

    def row_spec(width, col=0):
        return pl.BlockSpec((rows, width), lambda i: (i, col))

    col_spec = pl.BlockSpec((M_WIDTH, rows), lambda i: (0, i))
    consts = (wts["mgain"], wts["wab"], wts["wmb"], wts["wout"], wts["g2"], wts["w1b"], wts["w3b"], wts["w2b"])
    return pl.pallas_call(
        _back_kernel,
        grid=(tokens // rows,),
        in_specs=[row_spec(d_model), row_spec(ATTN_WIDTH), col_spec, col_spec,
                  row_spec(M_WIDTH, 1), row_spec(2 * d_model)] + [_const_spec(c.shape) for c in consts],
        out_specs=row_spec(d_model),
        out_shape=jax.ShapeDtypeStruct((tokens, d_model), F32),
        compiler_params=pltpu.CompilerParams(dimension_semantics=("parallel",),
                                             vmem_limit_bytes=VMEM_LIMIT_BYTES),
        name="back",
    )(h, attn_o, hf, hb, m, merge, *consts)


def _pair_heads(w):
    lead = w.shape[:-1]
    w = w.reshape(*lead, N_KV_HEADS, Q_PER_KV, HEAD_DIM)
    return jnp.swapaxes(w, -3, -2).reshape(*lead, ATTN_WIDTH)


def _prepare_weights(g_ffn1, w1_ffn1, w3_ffn1, w2_ffn1, g_mix, w_in, b_i, b_f, q_gain, k_gain, m_gain,
                     w_attn_br, w_mlstm_br, w_out, g_ffn2, w1_ffn2, w3_ffn2, w2_ffn2):
    d_model = w_in.shape[1]
    attn_w = N_Q_HEADS * HEAD_DIM
    kv_w = N_KV_HEADS * HEAD_DIM
    splits = (attn_w, kv_w, kv_w, M_WIDTH, M_WIDTH, M_WIDTH, M_WIDTH, 2 * M_HEADS, 2 * M_HEADS, d_model, d_model)
    offs = np.cumsum((0,) + splits)
    assert offs[-1] == w_in.shape[2]
    cols = [w_in[0, :, offs[i]:offs[i + 1]] for i in range(len(splits))]
    aq, ak, av, mq, mk, mv, mo, gi, gf, ga, gm = cols
    gate_w = jnp.pad(jnp.concatenate([gi, gf], axis=1), ((0, 0), (0, LANES - 4 * M_HEADS)))
    gate_b = jnp.pad(jnp.concatenate([b_i[0], b_f[0]]), (0, LANES - 4 * M_HEADS))[None]
    row = lambda g: g.astype(F32)[None]
    both_halves = lambda g: jnp.tile(g[0].astype(F32), LANES // HEAD_DIM)[None]
    return {
        "g1": row(g_ffn1[0]), "w1a": w1_ffn1[0].astype(BF16), "w3a": w3_ffn1[0].astype(BF16),
        "w2a": w2_ffn1[0].astype(BF16),
        "gmix": row(g_mix[0]),
        "wq": _pair_heads(aq).astype(BF16), "wkv": jnp.concatenate([ak, av], axis=1).astype(BF16),
        "wm": jnp.concatenate([mq, mk, mv, mo], axis=1).astype(BF16),
        "wg": gate_w.astype(BF16), "gbias": gate_b.astype(F32),
        "wmerge": jnp.concatenate([ga, gm], axis=1).astype(BF16),
        "qgain": both_halves(q_gain), "kgain": both_halves(k_gain),
        "mgain": m_gain[0].astype(F32).reshape(1, M_WIDTH),
        "wab": _pair_heads(w_attn_br[0].T).T.astype(BF16), "wmb": w_mlstm_br[0].astype(BF16), "wout": w_out[0].astype(BF16),
        "g2": row(g_ffn2[0]), "w1b": w1_ffn2[0].astype(BF16), "w3b": w3_ffn2[0].astype(BF16),
        "w2b": w2_ffn2[0].astype(BF16),
    }


def _rope_tables(n_tok):
    t = jnp.arange(n_tok)
    pos = jnp.stack([(t // GRID_W).astype(F32), (t % GRID_W).astype(F32)], axis=-1)
    inv_freq = ROPE_THETA ** (-2.0 * jnp.arange(ROPE_FREQS, dtype=F32) / ROPE_AXIS_DIM)
    ang = pos[:, :, None] * inv_freq
    cos, sin = jnp.cos(ang), jnp.sin(ang)
    cos = jnp.stack([cos, cos], axis=2).reshape(n_tok, HEAD_DIM)
    sin = jnp.stack([-sin, sin], axis=2).reshape(n_tok, HEAD_DIM)
    reps = (1, LANES // HEAD_DIM)
    return jnp.tile(cos, reps), jnp.tile(sin, reps)


def _trunk(x, meta_parts, wts):
    batch, n_tok, d_model = x.shape
    k_meta, v_meta, mt_meta, m_meta, gates_meta, norms_meta = meta_parts
    cos, sin = _rope_tables(n_tok)
    h, q, k, v, mt, m, gates, merge, norms = _front_call(
        x.reshape(batch * n_tok, d_model), cos, sin, wts, FRONT_ROWS)
    q_rows, k_rows, chunk = min(ATTN_Q_ROWS, n_tok), min(ATTN_K_ROWS, n_tok), min(MLSTM_CHUNK, n_tok)
    assert n_tok % FRONT_ROWS == 0 and n_tok % q_rows == 0 and n_tok % k_rows == 0 and n_tok % chunk == 0
    assert q_rows % FRONT_ROWS == 0 and k_rows % FRONT_ROWS == 0
    safe, floor = _attn_safe_blocks(norms, norms_meta, batch, n_tok // q_rows, n_tok // k_rows)
    attn_o = _attn_call(q, k, v, k_meta, v_meta, safe, floor, batch, q_rows, k_rows)
    hf, hb = _mlstm_call(mt, m, gates, mt_meta, m_meta, gates_meta, batch, chunk)
    y = _back_call(h, attn_o, hf, hb, m, merge, wts, BACK_ROWS)
    return y.reshape(batch, n_tok, d_model)


def kernel(x_prompt, x_sample, meta, g_ffn1, w1_ffn1, w3_ffn1, w2_ffn1, g_mix, w_in, b_i, b_f, q_gain, k_gain,
           m_gain, w_attn_br, w_mlstm_br, w_out, g_ffn2, w1_ffn2, w3_ffn2, w2_ffn2):
    assert w_in.shape[0] == 1, "single-layer trunk: the meta rows' mixer outputs are never consumed"
    assert meta.shape[0] == N_META
    wts = _prepare_weights(g_ffn1, w1_ffn1, w3_ffn1, w2_ffn1, g_mix, w_in, b_i, b_f, q_gain, k_gain, m_gain,
                           w_attn_br, w_mlstm_br, w_out, g_ffn2, w1_ffn2, w3_ffn2, w2_ffn2)
    ones = jnp.ones((META_ROWS, LANES), F32)
    meta_rows = jnp.pad(meta.astype(F32), ((0, META_ROWS - N_META), (0, 0)))
    _, _, k_meta, v_meta, mt_meta, m_meta, gates_meta, _, norms_meta = _front_call(
        meta_rows, ones, jnp.zeros_like(ones), wts, META_ROWS)
    meta_parts = (k_meta, v_meta, mt_meta, m_meta, gates_meta, norms_meta)
    return (_trunk(x_prompt, meta_parts, wts), _trunk(x_sample, meta_parts, wts))
```

```python
import jax
import jax.numpy as jnp
import numpy as np
from jax import lax
from jax.experimental import pallas as pl
from jax.experimental.pallas import tpu as pltpu

F32 = jnp.float32
BF16 = jnp.bfloat16

N_META = 16
GRID_W = 64
EPS = 1e-6
NEG = -1e30
N_Q_HEADS = 8
N_KV_HEADS = 2
Q_PER_KV = N_Q_HEADS // N_KV_HEADS
HEAD_DIM = 64
ROPE_AXIS_DIM = HEAD_DIM // 2
ROPE_FREQS = ROPE_AXIS_DIM // 2
ROPE_THETA = 10000.0
M_HEADS = 4
M_HEAD_DIM = 128
M_WIDTH = M_HEADS * M_HEAD_DIM

LANES = 128
BF16_ROWS = 16
VMEM_LIMIT_BYTES = 60 * 1024 * 1024

SUB_ROWS = 256
FRONT_ROWS = 2 * SUB_ROWS
BACK_ROWS = 2 * SUB_ROWS
ATTN_Q_ROWS = 1024
ATTN_K_ROWS = 2048
MLSTM_CHUNK = 256

Q_PAD = N_Q_HEADS * LANES
ATTN_WIDTH = N_Q_HEADS * HEAD_DIM
KV_WIDTH = N_KV_HEADS * HEAD_DIM
assert KV_WIDTH == LANES and HEAD_DIM * 2 == LANES
Q_SCALE = HEAD_DIM ** -0.5 * 1.4426950408889634
ATTN_SAFE_LOG2 = 90.0
ATTN_V_ROWS = HEAD_DIM + BF16_ROWS
META_ROWS = LANES
STATE_ROWS = M_HEAD_DIM + BF16_ROWS


def _const_spec(shape):
    zeros = (0,) * len(shape)
    return pl.BlockSpec(shape, lambda *_: zeros, pipeline_mode=pl.Buffered(1))


def _rmsnorm(x, g):
    return x * lax.rsqrt(jnp.mean(x * x, axis=-1, keepdims=True) + EPS) * g


def _swiglu_half_residual(x, g, w1_ref, w3_ref, w2_ref):
    u = _rmsnorm(x, g).astype(BF16)
    a = jnp.dot(u, w1_ref[...], preferred_element_type=F32)
    b = jnp.dot(u, w3_ref[...], preferred_element_type=F32)
    act = (a * jax.nn.sigmoid(a) * b).astype(BF16)
    return x + 0.5 * jnp.dot(act, w2_ref[...], preferred_element_type=F32)


def _log_sigmoid(x):
    return jnp.minimum(x, 0.0) - jnp.log1p(jnp.exp(-jnp.abs(x)))


def _half_sums(x, low_half):
    low = jnp.sum(jnp.where(low_half, x, 0.0), axis=-1, keepdims=True)
    high = jnp.sum(jnp.where(low_half, 0.0, x), axis=-1, keepdims=True)
    return low, high


def _head_pair_norm_rope(x, gain, cos, sin_signed, low_half, first_half):
    ms_low, ms_high = _half_sums(x * x, low_half)
    inv = jnp.where(low_half, lax.rsqrt(ms_low * (1.0 / HEAD_DIM) + EPS), lax.rsqrt(ms_high * (1.0 / HEAD_DIM) + EPS))
    y = x * inv * gain
    partner = jnp.where(first_half, pltpu.roll(y, LANES - ROPE_FREQS, 1), pltpu.roll(y, ROPE_FREQS, 1))
    return y * cos + partner * sin_signed


def _front_kernel(x_ref, cos_ref, sin_ref, g1_ref, w1_ref, w3_ref, w2_ref, gmix_ref, wq_ref, wkv_ref,
                  wm_ref, wg_ref, wmerge_ref, qgain_ref, kgain_ref, gbias_ref,
                  h_ref, q_ref, k_ref, v_ref, mt_ref, m_ref, gate_ref, merge_ref, norm_ref):
    sub = min(x_ref.shape[0], SUB_ROWS)
    lane = lax.broadcasted_iota(jnp.int32, (sub, LANES), 1)
    low_half = lane < HEAD_DIM
    first_half = (lane % ROPE_AXIS_DIM) < ROPE_FREQS

    def sq_norms(y_bf16):
        y = y_bf16.astype(F32)
        return jnp.maximum(*_half_sums(y * y, low_half))

    q_sq = jnp.zeros((sub, 1), F32)
    k_sq = jnp.zeros((sub, 1), F32)
    for r in range(0, x_ref.shape[0], sub):
        rs = slice(r, r + sub)
        h = _swiglu_half_residual(x_ref[rs, :], g1_ref[...], w1_ref, w3_ref, w2_ref)
        h_ref[rs, :] = h
        u = _rmsnorm(h, gmix_ref[...]).astype(BF16)
        cos = cos_ref[rs, :]
        sin = sin_ref[rs, :]

        zq = jnp.dot(u, wq_ref[...], preferred_element_type=F32)
        for pair in range(Q_PER_KV):
            y = _head_pair_norm_rope(zq[:, pair * LANES:(pair + 1) * LANES], qgain_ref[...], cos, sin,
                                     low_half, first_half) * Q_SCALE
            q_sq = jnp.maximum(q_sq, sq_norms(y.astype(BF16)))
            for group, keep in enumerate((low_half, ~low_half)):
                hd = group * Q_PER_KV + pair
                q_ref[hd * LANES:(hd + 1) * LANES, rs] = jnp.where(keep, y, 0.0).T.astype(BF16)

        zkv = jnp.dot(u, wkv_ref[...], preferred_element_type=F32)
        k = _head_pair_norm_rope(zkv[:, :LANES], kgain_ref[...], cos, sin, low_half, first_half).astype(BF16)
        k_ref[rs, :] = k
        k_sq = jnp.maximum(k_sq, sq_norms(k))
        v_ref[:, rs] = zkv[:, LANES:].T.astype(BF16)

        zm = jnp.dot(u, wm_ref[...], preferred_element_type=F32)
        mt_ref[:M_WIDTH, rs] = zm[:, :M_WIDTH].T.astype(BF16)
        mt_ref[M_WIDTH:, rs] = zm[:, 2 * M_WIDTH:3 * M_WIDTH].T.astype(BF16)
        m_ref[rs, :M_WIDTH] = (zm[:, M_WIDTH:2 * M_WIDTH] * (M_HEAD_DIM ** -0.5)).astype(BF16)
        m_ref[rs, M_WIDTH:] = zm[:, 3 * M_WIDTH:].astype(BF16)

        zg = jnp.dot(u, wg_ref[...], preferred_element_type=F32) + gbias_ref[...]
        gate_ref[rs, :] = jnp.where(lane < 2 * M_HEADS, zg, _log_sigmoid(zg))

        zmerge = jnp.dot(u, wmerge_ref[...], preferred_element_type=F32)
        merge_ref[rs, :] = jax.nn.sigmoid(zmerge).astype(BF16)

    tile_row = lax.broadcasted_iota(jnp.int32, norm_ref.shape, 0)
    norm_ref[...] = jnp.where(tile_row == 0, jnp.max(q_sq, axis=0, keepdims=True),
                              jnp.max(k_sq, axis=0, keepdims=True))


def _front_call(x, cos, sin, wts, rows):
    tokens, d_model = x.shape
    steps = tokens // rows
    table_steps = cos.shape[0] // rows

    def row_spec(width):
        return pl.BlockSpec((rows, width), lambda i: (i, 0))

    table_spec = pl.BlockSpec((rows, LANES), lambda i: (i % table_steps, 0))
    consts = (wts["g1"], wts["w1a"], wts["w3a"], wts["w2a"], wts["gmix"], wts["wq"], wts["wkv"],
              wts["wm"], wts["wg"], wts["wmerge"], wts["qgain"], wts["kgain"], wts["gbias"])

    def col_spec(height):
        return pl.BlockSpec((height, rows), lambda i: (0, i))

    out_specs = [row_spec(d_model), col_spec(Q_PAD), row_spec(KV_WIDTH), col_spec(KV_WIDTH), col_spec(2 * M_WIDTH),
                 row_spec(2 * M_WIDTH), row_spec(LANES), row_spec(2 * d_model),
                 pl.BlockSpec((None, 8, LANES), lambda i: (i, 0, 0))]
    out_shape = [jax.ShapeDtypeStruct((tokens, d_model), F32), jax.ShapeDtypeStruct((Q_PAD, tokens), BF16),
                 jax.ShapeDtypeStruct((tokens, KV_WIDTH), BF16), jax.ShapeDtypeStruct((KV_WIDTH, tokens), BF16),
                 jax.ShapeDtypeStruct((2 * M_WIDTH, tokens), BF16),
                 jax.ShapeDtypeStruct((tokens, 2 * M_WIDTH), BF16), jax.ShapeDtypeStruct((tokens, LANES), F32),
                 jax.ShapeDtypeStruct((tokens, 2 * d_model), BF16), jax.ShapeDtypeStruct((steps, 8, LANES), F32)]
    return pl.pallas_call(
        _front_kernel,
        grid=(steps,),
        in_specs=[row_spec(d_model), table_spec, table_spec] + [_const_spec(c.shape) for c in consts],
        out_specs=out_specs,
        out_shape=out_shape,
        compiler_params=pltpu.CompilerParams(dimension_semantics=("parallel",),
                                             vmem_limit_bytes=VMEM_LIMIT_BYTES),
        name="front",
    )(x, cos, sin, *consts)


def _attn_kernel(safe_ref, floor_ref, qt_ref, k_ref, vt_ref, kmeta_ref, vtmeta_ref, o_ref, m_scr, acc_scr):
    ki = pl.program_id(2)
    q_cols = qt_ref.shape[1]
    step = (pl.program_id(0) * pl.num_programs(1) + pl.program_id(1)) * pl.num_programs(2) + ki
    safe = safe_ref[step]

    def scores_t(hd, keys_ref):
        return jnp.dot(keys_ref[...], qt_ref[hd * LANES:(hd + 1) * LANES, :], preferred_element_type=F32)

    def values_t(hd, values_ref):
        g = hd // Q_PER_KV
        row = lax.broadcasted_iota(jnp.int32, (ATTN_V_ROWS - HEAD_DIM, values_ref.shape[1]), 0)
        ones = jnp.where(row == 0, 1.0, 0.0).astype(BF16)
        return jnp.concatenate([values_ref[g * HEAD_DIM:(g + 1) * HEAD_DIM, :], ones], axis=0)

    def meta_scores_t(hd):
        real = lax.broadcasted_iota(jnp.int32, (kmeta_ref.shape[0], q_cols), 0) < N_META
        return jnp.where(real, scores_t(hd, kmeta_ref), NEG)

    @pl.when((ki == 0) & (safe == 1))
    def _seed_from_meta_keys_against_floor():
        floor = jnp.full((1, q_cols), floor_ref[pl.program_id(0) * pl.num_programs(1) + pl.program_id(1)], F32)
        scores = [jnp.dot(kmeta_ref[:N_META, :], qt_ref[hd * LANES:(hd + 1) * LANES, :],
                          preferred_element_type=F32) for hd in range(N_Q_HEADS)]
        for hd, s in enumerate(scores):
            p = jnp.exp2(s - floor).astype(BF16)
            acc = jnp.dot(values_t(hd, vtmeta_ref)[:, :N_META], p, preferred_element_type=F32)
            m = jnp.maximum(floor, jnp.max(s, axis=0, keepdims=True))
            acc_scr[hd] = jnp.exp2(floor - m) * acc
            m_scr[hd] = jnp.broadcast_to(m, m_scr.shape[1:])

    @pl.when((ki == 0) & (safe != 1))
    def _seed_from_meta_keys():
        for hd in range(N_Q_HEADS):
            s = meta_scores_t(hd)
            m = jnp.max(s, axis=0, keepdims=True)
            p = jnp.exp2(s - m).astype(BF16)
            acc_scr[hd] = jnp.dot(values_t(hd, vtmeta_ref), p, preferred_element_type=F32)
            m_scr[hd] = jnp.broadcast_to(m, m_scr.shape[1:])

    @pl.when(safe == 1)
    def _block_against_previous_maximum():
        s_next = scores_t(0, k_ref)
        for hd in range(N_Q_HEADS):
            m_old = m_scr[hd][:1]
            s = s_next
            if hd + 1 < N_Q_HEADS:
                s_next = scores_t(hd + 1, k_ref)
            p = jnp.exp2(s - m_old).astype(BF16)
            pv = jnp.dot(values_t(hd, vt_ref), p, preferred_element_type=F32)
            m_new = jnp.maximum(m_old, jnp.max(s, axis=0, keepdims=True))
            acc_scr[hd] = jnp.exp2(m_old - m_new) * (acc_scr[hd] + pv)
            m_scr[hd] = jnp.broadcast_to(m_new, m_scr.shape[1:])

    @pl.when(safe != 1)
    def _block_against_own_maximum():
        for hd in range(N_Q_HEADS):
            m_old = m_scr[hd][:1]
            s = scores_t(hd, k_ref)
            m_new = jnp.maximum(m_old, jnp.max(s, axis=0, keepdims=True))
            p = jnp.exp2(s - m_new).astype(BF16)
            pv = jnp.dot(values_t(hd, vt_ref), p, preferred_element_type=F32)
            acc_scr[hd] = jnp.exp2(m_old - m_new) * acc_scr[hd] + pv
            m_scr[hd] = jnp.broadcast_to(m_new, m_scr.shape[1:])

    @pl.when(ki == pl.num_programs(2) - 1)
    def _normalise():
        def normalised(hd):
            acc = acc_scr[hd]
            return acc[:HEAD_DIM] / acc[HEAD_DIM:HEAD_DIM + 1]

        for pair in range(Q_PER_KV):
            both_t = jnp.concatenate([normalised(pair), normalised(Q_PER_KV + pair)], axis=0)
            o_ref[:, pair * LANES:(pair + 1) * LANES] = both_t.T.astype(BF16)


def _attn_safe_blocks(norms, norms_meta, batch, nq, nk):
    def block_max(sq, blocks):
        return jnp.sqrt(jnp.max(sq.reshape(blocks, -1), axis=1))

    q_norm = block_max(norms[:, 0, 0], batch * nq).reshape(batch, nq, 1)
    k_norm = block_max(norms[:, 1, 0], batch * nk).reshape(batch, 1, nk)
    k_meta_norm = jnp.sqrt(jnp.max(norms_meta[:, 1, 0]))
    gap = q_norm * (jnp.maximum(k_norm, k_meta_norm) + k_meta_norm)
    safe = (gap <= ATTN_SAFE_LOG2).astype(jnp.int32).reshape(-1)
    floor = (-q_norm * k_meta_norm).reshape(-1)
    return safe, floor


def _attn_call(qt, k, vt, k_meta, vt_meta, safe, floor, batch, q_rows, k_rows):
    tokens = k.shape[0]
    n_tok = tokens // batch
    nq = n_tok // q_rows
    nk = n_tok // k_rows
    grid_spec = pltpu.PrefetchScalarGridSpec(
        num_scalar_prefetch=2,
        grid=(batch, nq, nk),
        in_specs=[
            pl.BlockSpec((Q_PAD, q_rows), lambda b, i, j, *_: (0, b * nq + i)),
            pl.BlockSpec((k_rows, KV_WIDTH), lambda b, i, j, *_: (b * nk + j, 0)),
            pl.BlockSpec((KV_WIDTH, k_rows), lambda b, i, j, *_: (0, b * nk + j)),
            pl.BlockSpec(k_meta.shape, lambda b, i, j, *_: (0, 0)),
            pl.BlockSpec(vt_meta.shape, lambda b, i, j, *_: (0, 0)),
        ],
        out_specs=pl.BlockSpec((q_rows, ATTN_WIDTH), lambda b, i, j, *_: (b * nq + i, 0)),
        scratch_shapes=[pltpu.VMEM((N_Q_HEADS, 8, q_rows), F32),
                        pltpu.VMEM((N_Q_HEADS, ATTN_V_ROWS, q_rows), F32)],
    )
    return pl.pallas_call(
        _attn_kernel,
        grid_spec=grid_spec,
        out_shape=jax.ShapeDtypeStruct((tokens, ATTN_WIDTH), BF16),
        compiler_params=pltpu.CompilerParams(dimension_semantics=("parallel", "parallel", "arbitrary"),
                                             vmem_limit_bytes=VMEM_LIMIT_BYTES),
        name="attn",
    )(safe, floor, qt, k, vt, k_meta, vt_meta)


def _split_hi_lo(x):
    hi = x.astype(BF16)
    lo = (x - hi.astype(F32)).astype(BF16)
    return hi, lo


def _gate_cumsum(gates, causal):
    t = gates.shape[0]
    row = lax.broadcasted_iota(jnp.int32, (t, t), 0)
    col = lax.broadcasted_iota(jnp.int32, (t, t), 1)
    mask = (col <= row) if causal else (col >= row)
    tri = jnp.where(mask, 1.0, 0.0).astype(BF16)
    hi, lo = _split_hi_lo(gates)
    sums = jnp.dot(tri, hi, preferred_element_type=F32) + jnp.dot(tri, lo, preferred_element_type=F32)
    return sums, mask


def _values_t_with_ones(vt):
    row = lax.broadcasted_iota(jnp.int32, (STATE_ROWS - M_HEAD_DIM, vt.shape[1]), 0)
    return jnp.concatenate([vt, jnp.where(row == 0, 1.0, 0.0).astype(vt.dtype)], axis=0)


def _state_update(state_t, m_in, k, vaug_t, li_row, b_row, b_tot):
    w_end = b_tot - b_row + li_row
    m_new = jnp.maximum(b_tot + m_in, jnp.max(w_end, axis=1, keepdims=True))
    vw = (vaug_t.astype(F32) * jnp.exp(w_end - m_new)).astype(BF16)
    return jnp.exp(b_tot + m_in - m_new) * state_t + jnp.dot(vw, k, preferred_element_type=F32), m_new


def _mlstm_kernel(qtf_ref, vtf_ref, kf_ref, gf_ref, qtb_ref, vtb_ref, kb_ref, gb_ref, kmeta_ref, vtmeta_ref,
                  gmeta_ref, hf_ref, hb_ref, state_scr, m_scr):
    c = pl.program_id(1)

    @pl.when(c == 0)
    def _reset_and_absorb_meta():
        state_scr[...] = jnp.zeros_like(state_scr)
        m_scr[...] = jnp.zeros_like(m_scr)
        gates = gmeta_ref[...]
        row = lax.broadcasted_iota(jnp.int32, gates.shape, 0)
        lane = lax.broadcasted_iota(jnp.int32, gates.shape, 1)
        gates = jnp.where(row < N_META, gates, jnp.where(lane < 2 * M_HEADS, NEG, 0.0))
        sums, _ = _gate_cumsum(gates, causal=True)
        gates_t = gates.T
        sums_t = sums.T
        last = gates.shape[0] - 1
        for hd in range(M_HEADS):
            f_lane = 2 * M_HEADS + hd
            sl = slice(hd * M_HEAD_DIM, (hd + 1) * M_HEAD_DIM)
            vt = vtmeta_ref[M_WIDTH + hd * M_HEAD_DIM:M_WIDTH + (hd + 1) * M_HEAD_DIM, :]
            new_state, m_new = _state_update(
                state_scr[hd], m_scr[hd][:1, :1], kmeta_ref[:, sl], _values_t_with_ones(vt),
                gates_t[hd:hd + 1, :], sums_t[f_lane:f_lane + 1, :], sums[last:last + 1, f_lane:f_lane + 1])
            state_scr[hd] = new_state
            m_scr[hd] = jnp.broadcast_to(m_new, m_scr.shape[1:])

    chains = []
    for causal, qt_ref, vt_ref, k_ref, g_ref, o_ref in ((True, qtf_ref, vtf_ref, kf_ref, gf_ref, hf_ref),
                                                        (False, qtb_ref, vtb_ref, kb_ref, gb_ref, hb_ref)):
        gates = g_ref[...]
        t = gates.shape[0]
        sums, _ = _gate_cumsum(gates, causal)
        gates_t = gates.T
        sums_t = sums.T
        src = lax.broadcasted_iota(jnp.int32, (t, t), 0)
        tgt = lax.broadcasted_iota(jnp.int32, (t, t), 1)
        feeds = (src <= tgt) if causal else (src >= tgt)
        last = t - 1 if causal else 0
        for hd in range(M_HEADS):
            chain = hd if causal else M_HEADS + hd
            i_lane, f_lane = chain, 2 * M_HEADS + chain
            chains.append(dict(
                chain=chain, feeds=feeds, o_ref=o_ref, qt_ref=qt_ref, vt_ref=vt_ref, k_ref=k_ref,
                sl=slice(hd * M_HEAD_DIM, (hd + 1) * M_HEAD_DIM), t=t,
                r_col=gates[:, i_lane:i_lane + 1] - sums[:, f_lane:f_lane + 1],
                li_row=gates_t[i_lane:i_lane + 1, :], b_row=sums_t[f_lane:f_lane + 1, :],
                b_tot=sums[last:last + 1, f_lane:f_lane + 1], m_in=m_scr[chain][:1, :1]))

    for ch in chains:
        lhs = jnp.concatenate([ch["k_ref"][:, ch["sl"]], state_scr[ch["chain"]].astype(BF16)], axis=0)
        both = jnp.dot(lhs, ch["qt_ref"][ch["sl"], :], preferred_element_type=F32)
        ch["kq"], ch["inter"] = both[:ch["t"]], both[ch["t"]:]
    for ch in chains:
        ch["r"] = jnp.where(ch["feeds"], ch["r_col"], NEG)
        ch["g"] = jnp.maximum(jnp.max(ch["r"], axis=0, keepdims=True), ch["m_in"])
    for ch in chains:
        ch["s"] = (ch["kq"] * jnp.exp(ch["r"] - ch["g"])).astype(BF16)
        ch["vaug"] = _values_t_with_ones(ch["vt_ref"][ch["sl"], :])
        w_end = ch["b_tot"] - ch["b_row"] + ch["li_row"]
        ch["m_new"] = jnp.maximum(ch["b_tot"] + ch["m_in"], jnp.max(w_end, axis=1, keepdims=True))
        ch["vw"] = (ch["vaug"].astype(F32) * jnp.exp(w_end - ch["m_new"])).astype(BF16)
    for ch in chains:
        ch["tot"] = jnp.dot(ch["vaug"], ch["s"], preferred_element_type=F32)
        ch["local"] = jnp.dot(ch["vw"], ch["k_ref"][:, ch["sl"]], preferred_element_type=F32)
    for ch in chains:
        tot = ch["tot"] + jnp.exp(ch["m_in"] - ch["g"]) * ch["inter"]
        den = jnp.maximum(jnp.abs(tot[M_HEAD_DIM:M_HEAD_DIM + 1]), jnp.exp(-(ch["b_row"] + ch["g"])))
        ch["o_ref"][ch["sl"], :] = tot[:M_HEAD_DIM] / den
        chain = ch["chain"]
        state_scr[chain] = jnp.exp(ch["b_tot"] + ch["m_in"] - ch["m_new"]) * state_scr[chain] + ch["local"]
        m_scr[chain] = jnp.broadcast_to(ch["m_new"], m_scr.shape[1:])


def _mlstm_call(mt, m, gates, mt_meta, m_meta, gates_meta, batch, chunk):
    tokens = m.shape[0]
    nc = tokens // batch // chunk

    def chunk_of(direction):
        return (lambda b, c: b * nc + c) if direction == "fwd" else (lambda b, c: b * nc + nc - 1 - c)

    def specs(direction):
        at = chunk_of(direction)
        return [pl.BlockSpec((M_WIDTH, chunk), lambda b, c: (0, at(b, c))),
                pl.BlockSpec((M_WIDTH, chunk), lambda b, c: (1, at(b, c))),
                pl.BlockSpec((chunk, M_WIDTH), lambda b, c: (at(b, c), 0)),
                pl.BlockSpec((chunk, LANES), lambda b, c: (at(b, c), 0))]

    def out_spec(direction):
        at = chunk_of(direction)
        return pl.BlockSpec((M_WIDTH, chunk), lambda b, c: (0, at(b, c)))

    return pl.pallas_call(
        _mlstm_kernel,
        grid=(batch, nc),
        in_specs=specs("fwd") + specs("bwd") + [pl.BlockSpec(m_meta.shape, lambda b, c: (0, 0)),
                                                pl.BlockSpec(mt_meta.shape, lambda b, c: (0, 0)),
                                                pl.BlockSpec(gates_meta.shape, lambda b, c: (0, 0))],
        out_specs=[out_spec("fwd"), out_spec("bwd")],
        out_shape=[jax.ShapeDtypeStruct((M_WIDTH, tokens), F32)] * 2,
        scratch_shapes=[pltpu.VMEM((2 * M_HEADS, STATE_ROWS, M_HEAD_DIM), F32),
                        pltpu.VMEM((2 * M_HEADS, 8, LANES), F32)],
        compiler_params=pltpu.CompilerParams(dimension_semantics=("parallel", "arbitrary"),
                                             vmem_limit_bytes=VMEM_LIMIT_BYTES),
        name="mlstm",
    )(mt, mt, m, gates, mt, mt, m, gates, m_meta, mt_meta, gates_meta)


def _back_kernel(h_ref, ao_ref, hf_ref, hb_ref, mo_ref, merge_ref, mgain_ref, wab_ref, wmb_ref, wout_ref,
                 g2_ref, w1_ref, w3_ref, w2_ref, y_ref):
    rows, d_model = h_ref.shape
    for r in range(0, rows, SUB_ROWS):
        rs = slice(r, r + SUB_ROWS)
        a_out = jnp.dot(ao_ref[rs, :], wab_ref[...], preferred_element_type=F32)

        hm = (hf_ref[:, rs] + hb_ref[:, rs]).T
        o_gate = jax.nn.sigmoid(mo_ref[rs, :].astype(F32))
        gated = []
        for hd in range(M_HEADS):
            sl = slice(hd * M_HEAD_DIM, (hd + 1) * M_HEAD_DIM)
            gated.append((_rmsnorm(hm[:, sl], mgain_ref[:, sl]) * o_gate[:, sl]).astype(BF16))
        m_out = jnp.dot(jnp.concatenate(gated, axis=1), wmb_ref[...], preferred_element_type=F32)

        merged = (merge_ref[rs, :d_model].astype(F32) * a_out + merge_ref[rs, d_model:].astype(F32) * m_out)
        h2 = h_ref[rs, :] + jnp.dot(merged.astype(BF16), wout_ref[...], preferred_element_type=F32)
        y_ref[rs, :] = _swiglu_half_residual(h2, g2_ref[...], w1_ref, w3_ref, w2_ref)


def _back_call(h, attn_o, hf, hb, m, merge, wts, rows):
    tokens, d_model = h.shape

    def row_spec(width, col=0):
        return pl.BlockSpec((rows, width), lambda i: (i, col))

    col_spec = pl.BlockSpec((M_WIDTH, rows), lambda i: (0, i))
    consts = (wts["mgain"], wts["wab"], wts["wmb"], wts["wout"], wts["g2"], wts["w1b"], wts["w3b"], wts["w2b"])
    return pl.pallas_call(
        _back_kernel,
        grid=(tokens // rows,),
        in_specs=[row_spec(d_model), row_spec(ATTN_WIDTH), col_spec, col_spec,
                  row_spec(M_WIDTH, 1), row_spec(2 * d_model)] + [_const_spec(c.shape) for c in consts],
        out_specs=row_spec(d_model),
        out_shape=jax.ShapeDtypeStruct((tokens, d_model), F32),
        compiler_params=pltpu.CompilerParams(dimension_semantics=("parallel",),
                                             vmem_limit_bytes=VMEM_LIMIT_BYTES),
        name="back",
    )(h, attn_o, hf, hb, m, merge, *consts)


def _pair_heads(w):
    lead = w.shape[:-1]
    w = w.reshape(*lead, N_KV_HEADS, Q_PER_KV, HEAD_DIM)
    return jnp.swapaxes(w, -3, -2).reshape(*lead, ATTN_WIDTH)


def _prepare_weights(g_ffn1, w1_ffn1, w3_ffn1, w2_ffn1, g_mix, w_in, b_i, b_f, q_gain, k_gain, m_gain,
                     w_attn_br, w_mlstm_br, w_out, g_ffn2, w1_ffn2, w3_ffn2, w2_ffn2):
    d_model = w_in.shape[1]
    attn_w = N_Q_HEADS * HEAD_DIM
    kv_w = N_KV_HEADS * HEAD_DIM
    splits = (attn_w, kv_w, kv_w, M_WIDTH, M_WIDTH, M_WIDTH, M_WIDTH, 2 * M_HEADS, 2 * M_HEADS, d_model, d_model)
    offs = np.cumsum((0,) + splits)
    assert offs[-1] == w_in.shape[2]
    cols = [w_in[0, :, offs[i]:offs[i + 1]] for i in range(len(splits))]
    aq, ak, av, mq, mk, mv, mo, gi, gf, ga, gm = cols
    gate_w = jnp.pad(jnp.concatenate([gi, gf], axis=1), ((0, 0), (0, LANES - 4 * M_HEADS)))
    gate_b = jnp.pad(jnp.concatenate([b_i[0], b_f[0]]), (0, LANES - 4 * M_HEADS))[None]
    row = lambda g: g.astype(F32)[None]
    both_halves = lambda g: jnp.tile(g[0].astype(F32), LANES // HEAD_DIM)[None]
    return {
        "g1": row(g_ffn1[0]), "w1a": w1_ffn1[0].astype(BF16), "w3a": w3_ffn1[0].astype(BF16),
        "w2a": w2_ffn1[0].astype(BF16),
        "gmix": row(g_mix[0]),
        "wq": _pair_heads(aq).astype(BF16), "wkv": jnp.concatenate([ak, av], axis=1).astype(BF16),
        "wm": jnp.concatenate([mq, mk, mv, mo], axis=1).astype(BF16),
        "wg": gate_w.astype(BF16), "gbias": gate_b.astype(F32),
        "wmerge": jnp.concatenate([ga, gm], axis=1).astype(BF16),
        "qgain": both_halves(q_gain), "kgain": both_halves(k_gain),
        "mgain": m_gain[0].astype(F32).reshape(1, M_WIDTH),
        "wab": _pair_heads(w_attn_br[0].T).T.astype(BF16), "wmb": w_mlstm_br[0].astype(BF16), "wout": w_out[0].astype(BF16),
        "g2": row(g_ffn2[0]), "w1b": w1_ffn2[0].astype(BF16), "w3b": w3_ffn2[0].astype(BF16),
        "w2b": w2_ffn2[0].astype(BF16),
    }


def _rope_tables(n_tok):
    t = jnp.arange(n_tok)
    pos = jnp.stack([(t // GRID_W).astype(F32), (t % GRID_W).astype(F32)], axis=-1)
    inv_freq = ROPE_THETA ** (-2.0 * jnp.arange(ROPE_FREQS, dtype=F32) / ROPE_AXIS_DIM)
    ang = pos[:, :, None] * inv_freq
    cos, sin = jnp.cos(ang), jnp.sin(ang)
    cos = jnp.stack([cos, cos], axis=2).reshape(n_tok, HEAD_DIM)
    sin = jnp.stack([-sin, sin], axis=2).reshape(n_tok, HEAD_DIM)
    reps = (1, LANES // HEAD_DIM)
    return jnp.tile(cos, reps), jnp.tile(sin, reps)


def _trunk(x, meta_parts, wts):
    batch, n_tok, d_model = x.shape
    k_meta, v_meta, mt_meta, m_meta, gates_meta, norms_meta = meta_parts
    cos, sin = _rope_tables(n_tok)
    h, q, k, v, mt, m, gates, merge, norms = _front_call(
        x.reshape(batch * n_tok, d_model), cos, sin, wts, FRONT_ROWS)
    q_rows, k_rows, chunk = min(ATTN_Q_ROWS, n_tok), min(ATTN_K_ROWS, n_tok), min(MLSTM_CHUNK, n_tok)
    assert n_tok % FRONT_ROWS == 0 and n_tok % q_rows == 0 and n_tok % k_rows == 0 and n_tok % chunk == 0
    assert q_rows % FRONT_ROWS == 0 and k_rows % FRONT_ROWS == 0
    safe, floor = _attn_safe_blocks(norms, norms_meta, batch, n_tok // q_rows, n_tok // k_rows)
    attn_o = _attn_call(q, k, v, k_meta, v_meta, safe, floor, batch, q_rows, k_rows)
    hf, hb = _mlstm_call(mt, m, gates, mt_meta, m_meta, gates_meta, batch, chunk)
    y = _back_call(h, attn_o, hf, hb, m, merge, wts, BACK_ROWS)
    return y.reshape(batch, n_tok, d_model)


def kernel(x_prompt, x_sample, meta, g_ffn1, w1_ffn1, w3_ffn1, w2_ffn1, g_mix, w_in, b_i, b_f, q_gain, k_gain,
           m_gain, w_attn_br, w_mlstm_br, w_out, g_ffn2, w1_ffn2, w3_ffn2, w2_ffn2):
    assert w_in.shape[0] == 1, "single-layer trunk: the meta rows' mixer outputs are never consumed"
    assert meta.shape[0] == N_META
    wts = _prepare_weights(g_ffn1, w1_ffn1, w3_ffn1, w2_ffn1, g_mix, w_in, b_i, b_f, q_gain, k_gain, m_gain,
                           w_attn_br, w_mlstm_br, w_out, g_ffn2, w1_ffn2, w3_ffn2, w2_ffn2)
    ones = jnp.ones((META_ROWS, LANES), F32)
    meta_rows = jnp.pad(meta.astype(F32), ((0, META_ROWS - N_META), (0, 0)))
    _, _, k_meta, v_meta, mt_meta, m_meta, gates_meta, _, norms_meta = _front_call(
        meta_rows, ones, jnp.zeros_like(ones), wts, META_ROWS)
    meta_parts = (k_meta, v_meta, mt_meta, m_meta, gates_meta, norms_meta)
    return (_trunk(x_prompt, meta_parts, wts), _trunk(x_sample, meta_parts, wts))
```

```python
import jax
import jax.numpy as jnp
import numpy as np
from jax import lax
from jax.experimental import pallas as pl
from jax.experimental.pallas import tpu as pltpu

F32 = jnp.float32
BF16 = jnp.bfloat16

N_META = 16
GRID_W = 64
EPS = 1e-6
NEG = -1e30
N_Q_HEADS = 8
N_KV_HEADS = 2
Q_PER_KV = N_Q_HEADS // N_KV_HEADS
HEAD_DIM = 64
ROPE_AXIS_DIM = HEAD_DIM // 2
ROPE_FREQS = ROPE_AXIS_DIM // 2
ROPE_THETA = 10000.0
M_HEADS = 4
M_HEAD_DIM = 128
M_WIDTH = M_HEADS * M_HEAD_DIM

LANES = 128
BF16_ROWS = 16
VMEM_LIMIT_BYTES = 60 * 1024 * 1024

SUB_ROWS = 256
FRONT_ROWS = 2 * SUB_ROWS
BACK_ROWS = 2 * SUB_ROWS
ATTN_Q_ROWS = 1024
ATTN_K_ROWS = 1024
MLSTM_CHUNK = 256

Q_PAD = N_Q_HEADS * LANES
ATTN_WIDTH = N_Q_HEADS * HEAD_DIM
KV_WIDTH = N_KV_HEADS * HEAD_DIM
assert KV_WIDTH == LANES and HEAD_DIM * 2 == LANES
Q_SCALE = HEAD_DIM ** -0.5 * 1.4426950408889634
ATTN_SAFE_LOG2 = 90.0
ATTN_V_ROWS = HEAD_DIM + BF16_ROWS
META_ROWS = LANES
STATE_ROWS = M_HEAD_DIM + BF16_ROWS


def _const_spec(shape):
    zeros = (0,) * len(shape)
    return pl.BlockSpec(shape, lambda *_: zeros, pipeline_mode=pl.Buffered(1))


def _rmsnorm(x, g):
    return x * lax.rsqrt(jnp.mean(x * x, axis=-1, keepdims=True) + EPS) * g


def _swiglu_half_residual(x, g, w1_ref, w3_ref, w2_ref):
    u = _rmsnorm(x, g).astype(BF16)
    a = jnp.dot(u, w1_ref[...], preferred_element_type=F32)
    b = jnp.dot(u, w3_ref[...], preferred_element_type=F32)
    act = (a * jax.nn.sigmoid(a) * b).astype(BF16)
    return x + 0.5 * jnp.dot(act, w2_ref[...], preferred_element_type=F32)


def _log_sigmoid(x):
    return jnp.minimum(x, 0.0) - jnp.log1p(jnp.exp(-jnp.abs(x)))


def _half_sums(x, low_half):
    low = jnp.sum(jnp.where(low_half, x, 0.0), axis=-1, keepdims=True)
    high = jnp.sum(jnp.where(low_half, 0.0, x), axis=-1, keepdims=True)
    return low, high


def _head_pair_norm_rope(x, gain, cos, sin_signed, low_half, first_half):
    ms_low, ms_high = _half_sums(x * x, low_half)
    inv = jnp.where(low_half, lax.rsqrt(ms_low * (1.0 / HEAD_DIM) + EPS), lax.rsqrt(ms_high * (1.0 / HEAD_DIM) + EPS))
    y = x * inv * gain
    partner = jnp.where(first_half, pltpu.roll(y, LANES - ROPE_FREQS, 1), pltpu.roll(y, ROPE_FREQS, 1))
    return y * cos + partner * sin_signed


def _front_kernel(x_ref, cos_ref, sin_ref, g1_ref, w1_ref, w3_ref, w2_ref, gmix_ref, wq_ref, wkv_ref,
                  wm_ref, wg_ref, wmerge_ref, qgain_ref, kgain_ref, gbias_ref,
                  h_ref, q_ref, k_ref, v_ref, mt_ref, m_ref, gate_ref, merge_ref, norm_ref):
    sub = min(x_ref.shape[0], SUB_ROWS)
    lane = lax.broadcasted_iota(jnp.int32, (sub, LANES), 1)
    low_half = lane < HEAD_DIM
    first_half = (lane % ROPE_AXIS_DIM) < ROPE_FREQS

    def sq_norms(y_bf16):
        y = y_bf16.astype(F32)
        return jnp.maximum(*_half_sums(y * y, low_half))

    q_sq = jnp.zeros((sub, 1), F32)
    k_sq = jnp.zeros((sub, 1), F32)
    for r in range(0, x_ref.shape[0], sub):
        rs = slice(r, r + sub)
        h = _swiglu_half_residual(x_ref[rs, :], g1_ref[...], w1_ref, w3_ref, w2_ref)
        h_ref[rs, :] = h
        u = _rmsnorm(h, gmix_ref[...]).astype(BF16)
        cos = cos_ref[rs, :]
        sin = sin_ref[rs, :]

        zq = jnp.dot(u, wq_ref[...], preferred_element_type=F32)
        for pair in range(Q_PER_KV):
            y = _head_pair_norm_rope(zq[:, pair * LANES:(pair + 1) * LANES], qgain_ref[...], cos, sin,
                                     low_half, first_half) * Q_SCALE
            q_sq = jnp.maximum(q_sq, sq_norms(y.astype(BF16)))
            for group, keep in enumerate((low_half, ~low_half)):
                hd = group * Q_PER_KV + pair
                q_ref[hd * LANES:(hd + 1) * LANES, rs] = jnp.where(keep, y, 0.0).T.astype(BF16)

        zkv = jnp.dot(u, wkv_ref[...], preferred_element_type=F32)
        k = _head_pair_norm_rope(zkv[:, :LANES], kgain_ref[...], cos, sin, low_half, first_half).astype(BF16)
        k_ref[rs, :] = k
        k_sq = jnp.maximum(k_sq, sq_norms(k))
        v_ref[:, rs] = zkv[:, LANES:].T.astype(BF16)

        zm = jnp.dot(u, wm_ref[...], preferred_element_type=F32)
        mt_ref[:M_WIDTH, rs] = zm[:, :M_WIDTH].T.astype(BF16)
        mt_ref[M_WIDTH:, rs] = zm[:, 2 * M_WIDTH:3 * M_WIDTH].T.astype(BF16)
        m_ref[rs, :M_WIDTH] = (zm[:, M_WIDTH:2 * M_WIDTH] * (M_HEAD_DIM ** -0.5)).astype(BF16)
        m_ref[rs, M_WIDTH:] = zm[:, 3 * M_WIDTH:].astype(BF16)

        zg = jnp.dot(u, wg_ref[...], preferred_element_type=F32) + gbias_ref[...]
        gate_ref[rs, :] = jnp.where(lane < 2 * M_HEADS, zg, _log_sigmoid(zg))

        zmerge = jnp.dot(u, wmerge_ref[...], preferred_element_type=F32)
        merge_ref[rs, :] = jax.nn.sigmoid(zmerge).astype(BF16)

    tile_row = lax.broadcasted_iota(jnp.int32, norm_ref.shape, 0)
    norm_ref[...] = jnp.where(tile_row == 0, jnp.max(q_sq, axis=0, keepdims=True),
                              jnp.max(k_sq, axis=0, keepdims=True))


def _front_call(x, cos, sin, wts, rows):
    tokens, d_model = x.shape
    steps = tokens // rows
    table_steps = cos.shape[0] // rows

    def row_spec(width):
        return pl.BlockSpec((rows, width), lambda i: (i, 0))

    table_spec = pl.BlockSpec((rows, LANES), lambda i: (i % table_steps, 0))
    consts = (wts["g1"], wts["w1a"], wts["w3a"], wts["w2a"], wts["gmix"], wts["wq"], wts["wkv"],
              wts["wm"], wts["wg"], wts["wmerge"], wts["qgain"], wts["kgain"], wts["gbias"])

    def col_spec(height):
        return pl.BlockSpec((height, rows), lambda i: (0, i))

    out_specs = [row_spec(d_model), col_spec(Q_PAD), row_spec(KV_WIDTH), col_spec(KV_WIDTH), col_spec(2 * M_WIDTH),
                 row_spec(2 * M_WIDTH), row_spec(LANES), row_spec(2 * d_model),
                 pl.BlockSpec((None, 8, LANES), lambda i: (i, 0, 0))]
    out_shape = [jax.ShapeDtypeStruct((tokens, d_model), F32), jax.ShapeDtypeStruct((Q_PAD, tokens), BF16),
                 jax.ShapeDtypeStruct((tokens, KV_WIDTH), BF16), jax.ShapeDtypeStruct((KV_WIDTH, tokens), BF16),
                 jax.ShapeDtypeStruct((2 * M_WIDTH, tokens), BF16),
                 jax.ShapeDtypeStruct((tokens, 2 * M_WIDTH), BF16), jax.ShapeDtypeStruct((tokens, LANES), F32),
                 jax.ShapeDtypeStruct((tokens, 2 * d_model), BF16), jax.ShapeDtypeStruct((steps, 8, LANES), F32)]
    return pl.pallas_call(
        _front_kernel,
        grid=(steps,),
        in_specs=[row_spec(d_model), table_spec, table_spec] + [_const_spec(c.shape) for c in consts],
        out_specs=out_specs,
        out_shape=out_shape,
        compiler_params=pltpu.CompilerParams(dimension_semantics=("parallel",),
                                             vmem_limit_bytes=VMEM_LIMIT_BYTES),
        name="front",
    )(x, cos, sin, *consts)


def _attn_kernel(safe_ref, floor_ref, qt_ref, k_ref, vt_ref, kmeta_ref, vtmeta_ref, o_ref, m_scr, acc_scr):
    ki = pl.program_id(2)
    q_cols = qt_ref.shape[1]
    step = (pl.program_id(0) * pl.num_programs(1) + pl.program_id(1)) * pl.num_programs(2) + ki
    safe = safe_ref[step]

    def scores_t(hd, keys_ref):
        return jnp.dot(keys_ref[...], qt_ref[hd * LANES:(hd + 1) * LANES, :], preferred_element_type=F32)

    def values_t(hd, values_ref):
        g = hd // Q_PER_KV
        row = lax.broadcasted_iota(jnp.int32, (ATTN_V_ROWS - HEAD_DIM, values_ref.shape[1]), 0)
        ones = jnp.where(row == 0, 1.0, 0.0).astype(BF16)
        return jnp.concatenate([values_ref[g * HEAD_DIM:(g + 1) * HEAD_DIM, :], ones], axis=0)

    def meta_scores_t(hd):
        real = lax.broadcasted_iota(jnp.int32, (kmeta_ref.shape[0], q_cols), 0) < N_META
        return jnp.where(real, scores_t(hd, kmeta_ref), NEG)

    @pl.when((ki == 0) & (safe == 1))
    def _seed_from_meta_keys_against_floor():
        floor = jnp.full((1, q_cols), floor_ref[pl.program_id(0) * pl.num_programs(1) + pl.program_id(1)], F32)
        scores = [jnp.dot(kmeta_ref[:N_META, :], qt_ref[hd * LANES:(hd + 1) * LANES, :],
                          preferred_element_type=F32) for hd in range(N_Q_HEADS)]
        for hd, s in enumerate(scores):
            p = jnp.exp2(s - floor).astype(BF16)
            acc = jnp.dot(values_t(hd, vtmeta_ref)[:, :N_META], p, preferred_element_type=F32)
            m = jnp.maximum(floor, jnp.max(s, axis=0, keepdims=True))
            acc_scr[hd] = jnp.exp2(floor - m) * acc
            m_scr[hd] = jnp.broadcast_to(m, m_scr.shape[1:])

    @pl.when((ki == 0) & (safe != 1))
    def _seed_from_meta_keys():
        for hd in range(N_Q_HEADS):
            s = meta_scores_t(hd)
            m = jnp.max(s, axis=0, keepdims=True)
            p = jnp.exp2(s - m).astype(BF16)
            acc_scr[hd] = jnp.dot(values_t(hd, vtmeta_ref), p, preferred_element_type=F32)
            m_scr[hd] = jnp.broadcast_to(m, m_scr.shape[1:])

    @pl.when(safe == 1)
    def _block_against_previous_maximum():
        s_next = scores_t(0, k_ref)
        for hd in range(N_Q_HEADS):
            m_old = m_scr[hd][:1]
            s = s_next
            if hd + 1 < N_Q_HEADS:
                s_next = scores_t(hd + 1, k_ref)
            p = jnp.exp2(s - m_old).astype(BF16)
            pv = jnp.dot(values_t(hd, vt_ref), p, preferred_element_type=F32)
            m_new = jnp.maximum(m_old, jnp.max(s, axis=0, keepdims=True))
            acc_scr[hd] = jnp.exp2(m_old - m_new) * (acc_scr[hd] + pv)
            m_scr[hd] = jnp.broadcast_to(m_new, m_scr.shape[1:])

    @pl.when(safe != 1)
    def _block_against_own_maximum():
        for hd in range(N_Q_HEADS):
            m_old = m_scr[hd][:1]
            s = scores_t(hd, k_ref)
            m_new = jnp.maximum(m_old, jnp.max(s, axis=0, keepdims=True))
            p = jnp.exp2(s - m_new).astype(BF16)
            pv = jnp.dot(values_t(hd, vt_ref), p, preferred_element_type=F32)
            acc_scr[hd] = jnp.exp2(m_old - m_new) * acc_scr[hd] + pv
            m_scr[hd] = jnp.broadcast_to(m_new, m_scr.shape[1:])

    @pl.when(ki == pl.num_programs(2) - 1)
    def _normalise():
        def normalised(hd):
            acc = acc_scr[hd]
            return acc[:HEAD_DIM] / acc[HEAD_DIM:HEAD_DIM + 1]

        for pair in range(Q_PER_KV):
            both_t = jnp.concatenate([normalised(pair), normalised(Q_PER_KV + pair)], axis=0)
            o_ref[:, pair * LANES:(pair + 1) * LANES] = both_t.T.astype(BF16)


def _attn_safe_blocks(norms, norms_meta, batch, nq, nk):
    def block_max(sq, blocks):
        return jnp.sqrt(jnp.max(sq.reshape(blocks, -1), axis=1))

    q_norm = block_max(norms[:, 0, 0], batch * nq).reshape(batch, nq, 1)
    k_norm = block_max(norms[:, 1, 0], batch * nk).reshape(batch, 1, nk)
    k_meta_norm = jnp.sqrt(jnp.max(norms_meta[:, 1, 0]))
    gap = q_norm * (jnp.maximum(k_norm, k_meta_norm) + k_meta_norm)
    safe = (gap <= ATTN_SAFE_LOG2).astype(jnp.int32).reshape(-1)
    floor = (-q_norm * k_meta_norm).reshape(-1)
    return safe, floor


def _attn_call(qt, k, vt, k_meta, vt_meta, safe, floor, batch, q_rows, k_rows):
    tokens = k.shape[0]
    n_tok = tokens // batch
    nq = n_tok // q_rows
    nk = n_tok // k_rows
    grid_spec = pltpu.PrefetchScalarGridSpec(
        num_scalar_prefetch=2,
        grid=(batch, nq, nk),
        in_specs=[
            pl.BlockSpec((Q_PAD, q_rows), lambda b, i, j, *_: (0, b * nq + i)),
            pl.BlockSpec((k_rows, KV_WIDTH), lambda b, i, j, *_: (b * nk + j, 0)),
            pl.BlockSpec((KV_WIDTH, k_rows), lambda b, i, j, *_: (0, b * nk + j)),
            pl.BlockSpec(k_meta.shape, lambda b, i, j, *_: (0, 0)),
            pl.BlockSpec(vt_meta.shape, lambda b, i, j, *_: (0, 0)),
        ],
        out_specs=pl.BlockSpec((q_rows, ATTN_WIDTH), lambda b, i, j, *_: (b * nq + i, 0)),
        scratch_shapes=[pltpu.VMEM((N_Q_HEADS, 8, q_rows), F32),
                        pltpu.VMEM((N_Q_HEADS, ATTN_V_ROWS, q_rows), F32)],
    )
    return pl.pallas_call(
        _attn_kernel,
        grid_spec=grid_spec,
        out_shape=jax.ShapeDtypeStruct((tokens, ATTN_WIDTH), BF16),
        compiler_params=pltpu.CompilerParams(dimension_semantics=("parallel", "parallel", "arbitrary"),
                                             vmem_limit_bytes=VMEM_LIMIT_BYTES),
        name="attn",
    )(safe, floor, qt, k, vt, k_meta, vt_meta)


def _split_hi_lo(x):
    hi = x.astype(BF16)
    lo = (x - hi.astype(F32)).astype(BF16)
    return hi, lo


def _gate_cumsum(gates, causal):
    t = gates.shape[0]
    row = lax.broadcasted_iota(jnp.int32, (t, t), 0)
    col = lax.broadcasted_iota(jnp.int32, (t, t), 1)
    mask = (col <= row) if causal else (col >= row)
    tri = jnp.where(mask, 1.0, 0.0).astype(BF16)
    hi, lo = _split_hi_lo(gates)
    sums = jnp.dot(tri, hi, preferred_element_type=F32) + jnp.dot(tri, lo, preferred_element_type=F32)
    return sums, mask


def _values_t_with_ones(vt):
    row = lax.broadcasted_iota(jnp.int32, (STATE_ROWS - M_HEAD_DIM, vt.shape[1]), 0)
    return jnp.concatenate([vt, jnp.where(row == 0, 1.0, 0.0).astype(vt.dtype)], axis=0)


def _state_update(state_t, m_in, k, vaug_t, li_row, b_row, b_tot):
    w_end = b_tot - b_row + li_row
    m_new = jnp.maximum(b_tot + m_in, jnp.max(w_end, axis=1, keepdims=True))
    vw = (vaug_t.astype(F32) * jnp.exp(w_end - m_new)).astype(BF16)
    return jnp.exp(b_tot + m_in - m_new) * state_t + jnp.dot(vw, k, preferred_element_type=F32), m_new


def _mlstm_kernel(qtf_ref, vtf_ref, kf_ref, gf_ref, qtb_ref, vtb_ref, kb_ref, gb_ref, kmeta_ref, vtmeta_ref,
                  gmeta_ref, hf_ref, hb_ref, state_scr, m_scr):
    c = pl.program_id(1)

    @pl.when(c == 0)
    def _reset_and_absorb_meta():
        state_scr[...] = jnp.zeros_like(state_scr)
        m_scr[...] = jnp.zeros_like(m_scr)
        gates = gmeta_ref[...]
        row = lax.broadcasted_iota(jnp.int32, gates.shape, 0)
        lane = lax.broadcasted_iota(jnp.int32, gates.shape, 1)
        gates = jnp.where(row < N_META, gates, jnp.where(lane < 2 * M_HEADS, NEG, 0.0))
        sums, _ = _gate_cumsum(gates, causal=True)
        gates_t = gates.T
        sums_t = sums.T
        last = gates.shape[0] - 1
        for hd in range(M_HEADS):
            f_lane = 2 * M_HEADS + hd
            sl = slice(hd * M_HEAD_DIM, (hd + 1) * M_HEAD_DIM)
            vt = vtmeta_ref[M_WIDTH + hd * M_HEAD_DIM:M_WIDTH + (hd + 1) * M_HEAD_DIM, :]
            new_state, m_new = _state_update(
                state_scr[hd], m_scr[hd][:1, :1], kmeta_ref[:, sl], _values_t_with_ones(vt),
                gates_t[hd:hd + 1, :], sums_t[f_lane:f_lane + 1, :], sums[last:last + 1, f_lane:f_lane + 1])
            state_scr[hd] = new_state
            m_scr[hd] = jnp.broadcast_to(m_new, m_scr.shape[1:])

    chains = []
    for causal, qt_ref, vt_ref, k_ref, g_ref, o_ref in ((True, qtf_ref, vtf_ref, kf_ref, gf_ref, hf_ref),
                                                        (False, qtb_ref, vtb_ref, kb_ref, gb_ref, hb_ref)):
        gates = g_ref[...]
        t = gates.shape[0]
        sums, _ = _gate_cumsum(gates, causal)
        gates_t = gates.T
        sums_t = sums.T
        src = lax.broadcasted_iota(jnp.int32, (t, t), 0)
        tgt = lax.broadcasted_iota(jnp.int32, (t, t), 1)
        feeds = (src <= tgt) if causal else (src >= tgt)
        last = t - 1 if causal else 0
        for hd in range(M_HEADS):
            chain = hd if causal else M_HEADS + hd
            i_lane, f_lane = chain, 2 * M_HEADS + chain
            chains.append(dict(
                chain=chain, feeds=feeds, o_ref=o_ref, qt_ref=qt_ref, vt_ref=vt_ref, k_ref=k_ref,
                sl=slice(hd * M_HEAD_DIM, (hd + 1) * M_HEAD_DIM), t=t,
                r_col=gates[:, i_lane:i_lane + 1] - sums[:, f_lane:f_lane + 1],
                li_row=gates_t[i_lane:i_lane + 1, :], b_row=sums_t[f_lane:f_lane + 1, :],
                b_tot=sums[last:last + 1, f_lane:f_lane + 1], m_in=m_scr[chain][:1, :1]))

    for ch in chains:
        lhs = jnp.concatenate([ch["k_ref"][:, ch["sl"]], state_scr[ch["chain"]].astype(BF16)], axis=0)
        both = jnp.dot(lhs, ch["qt_ref"][ch["sl"], :], preferred_element_type=F32)
        ch["kq"], ch["inter"] = both[:ch["t"]], both[ch["t"]:]
    for ch in chains:
        ch["r"] = jnp.where(ch["feeds"], ch["r_col"], NEG)
        ch["g"] = jnp.maximum(jnp.max(ch["r"], axis=0, keepdims=True), ch["m_in"])
    for ch in chains:
        ch["s"] = (ch["kq"] * jnp.exp(ch["r"] - ch["g"])).astype(BF16)
        ch["vaug"] = _values_t_with_ones(ch["vt_ref"][ch["sl"], :])
        w_end = ch["b_tot"] - ch["b_row"] + ch["li_row"]
        ch["m_new"] = jnp.maximum(ch["b_tot"] + ch["m_in"], jnp.max(w_end, axis=1, keepdims=True))
        ch["vw"] = (ch["vaug"].astype(F32) * jnp.exp(w_end - ch["m_new"])).astype(BF16)
    for ch in chains:
        ch["tot"] = jnp.dot(ch["vaug"], ch["s"], preferred_element_type=F32)
        ch["local"] = jnp.dot(ch["vw"], ch["k_ref"][:, ch["sl"]], preferred_element_type=F32)
    for ch in chains:
        tot = ch["tot"] + jnp.exp(ch["m_in"] - ch["g"]) * ch["inter"]
        den = jnp.maximum(jnp.abs(tot[M_HEAD_DIM:M_HEAD_DIM + 1]), jnp.exp(-(ch["b_row"] + ch["g"])))
        ch["o_ref"][ch["sl"], :] = tot[:M_HEAD_DIM] / den
        chain = ch["chain"]
        state_scr[chain] = jnp.exp(ch["b_tot"] + ch["m_in"] - ch["m_new"]) * state_scr[chain] + ch["local"]
        m_scr[chain] = jnp.broadcast_to(ch["m_new"], m_scr.shape[1:])


def _mlstm_call(mt, m, gates, mt_meta, m_meta, gates_meta, batch, chunk):
    tokens = m.shape[0]
    nc = tokens // batch // chunk

    def chunk_of(direction):
        return (lambda b, c: b * nc + c) if direction == "fwd" else (lambda b, c: b * nc + nc - 1 - c)

    def specs(direction):
        at = chunk_of(direction)
        return [pl.BlockSpec((M_WIDTH, chunk), lambda b, c: (0, at(b, c))),
                pl.BlockSpec((M_WIDTH, chunk), lambda b, c: (1, at(b, c))),
                pl.BlockSpec((chunk, M_WIDTH), lambda b, c: (at(b, c), 0)),
                pl.BlockSpec((chunk, LANES), lambda b, c: (at(b, c), 0))]

    def out_spec(direction):
        at = chunk_of(direction)
        return pl.BlockSpec((M_WIDTH, chunk), lambda b, c: (0, at(b, c)))

    return pl.pallas_call(
        _mlstm_kernel,
        grid=(batch, nc),
        in_specs=specs("fwd") + specs("bwd") + [pl.BlockSpec(m_meta.shape, lambda b, c: (0, 0)),
                                                pl.BlockSpec(mt_meta.shape, lambda b, c: (0, 0)),
                                                pl.BlockSpec(gates_meta.shape, lambda b, c: (0, 0))],
        out_specs=[out_spec("fwd"), out_spec("bwd")],
        out_shape=[jax.ShapeDtypeStruct((M_WIDTH, tokens), F32)] * 2,
        scratch_shapes=[pltpu.VMEM((2 * M_HEADS, STATE_ROWS, M_HEAD_DIM), F32),
                        pltpu.VMEM((2 * M_HEADS, 8, LANES), F32)],
        compiler_params=pltpu.CompilerParams(dimension_semantics=("parallel", "arbitrary"),
                                             vmem_limit_bytes=VMEM_LIMIT_BYTES),
        name="mlstm",
    )(mt, mt, m, gates, mt, mt, m, gates, m_meta, mt_meta, gates_meta)


def _back_kernel(h_ref, ao_ref, hf_ref, hb_ref, mo_ref, merge_ref, mgain_ref, wab_ref, wmb_ref, wout_ref,
                 g2_ref, w1_ref, w3_ref, w2_ref, y_ref):
    rows, d_model = h_ref.shape
    for r in range(0, rows, SUB_ROWS):
        rs = slice(r, r + SUB_ROWS)
        a_out = jnp.dot(ao_ref[rs, :], wab_ref[...], preferred_element_type=F32)

        hm = (hf_ref[:, rs] + hb_ref[:, rs]).T
        o_gate = jax.nn.sigmoid(mo_ref[rs, :].astype(F32))
        gated = []
        for hd in range(M_HEADS):
            sl = slice(hd * M_HEAD_DIM, (hd + 1) * M_HEAD_DIM)
            gated.append((_rmsnorm(hm[:, sl], mgain_ref[:, sl]) * o_gate[:, sl]).astype(BF16))
        m_out = jnp.dot(jnp.concatenate(gated, axis=1), wmb_ref[...], preferred_element_type=F32)

        merged = (merge_ref[rs, :d_model].astype(F32) * a_out + merge_ref[rs, d_model:].astype(F32) * m_out)
        h2 = h_ref[rs, :] + jnp.dot(merged.astype(BF16), wout_ref[...], preferred_element_type=F32)
        y_ref[rs, :] = _swiglu_half_residual(h2, g2_ref[...], w1_ref, w3_ref, w2_ref)


def _back_call(h, attn_o, hf, hb, m, merge, wts, rows):
    tokens, d_model = h.shape

    def row_spec(width, col=0):
        return pl.BlockSpec((rows, width), lambda i: (i, col))

    col_spec = pl.BlockSpec((M_WIDTH, rows), lambda i: (0, i))
    consts = (wts["mgain"], wts["wab"], wts["wmb"], wts["wout"], wts["g2"], wts["w1b"], wts["w3b"], wts["w2b"])
    return pl.pallas_call(
        _back_kernel,
        grid=(tokens // rows,),
        in_specs=[row_spec(d_model), row_spec(ATTN_WIDTH), col_spec, col_spec,
                  row_spec(M_WIDTH, 1), row_spec(2 * d_model)] + [_const_spec(c.shape) for c in consts],
        out_specs=row_spec(d_model),
        out_shape=jax.ShapeDtypeStruct((tokens, d_model), F32),
        compiler_params=pltpu.CompilerParams(dimension_semantics=("parallel",),
                                             vmem_limit_bytes=VMEM_LIMIT_BYTES),
        name="back",
    )(h, attn_o, hf, hb, m, merge, *consts)


def _pair_heads(w):
    lead = w.shape[:-1]
    w = w.reshape(*lead, N_KV_HEADS, Q_PER_KV, HEAD_DIM)
    return jnp.swapaxes(w, -3, -2).reshape(*lead, ATTN_WIDTH)


def _prepare_weights(g_ffn1, w1_ffn1, w3_ffn1, w2_ffn1, g_mix, w_in, b_i, b_f, q_gain, k_gain, m_gain,
                     w_attn_br, w_mlstm_br, w_out, g_ffn2, w1_ffn2, w3_ffn2, w2_ffn2):
    d_model = w_in.shape[1]
    attn_w = N_Q_HEADS * HEAD_DIM
    kv_w = N_KV_HEADS * HEAD_DIM
    splits = (attn_w, kv_w, kv_w, M_WIDTH, M_WIDTH, M_WIDTH, M_WIDTH, 2 * M_HEADS, 2 * M_HEADS, d_model, d_model)
    offs = np.cumsum((0,) + splits)
    assert offs[-1] == w_in.shape[2]
    cols = [w_in[0, :, offs[i]:offs[i + 1]] for i in range(len(splits))]
    aq, ak, av, mq, mk, mv, mo, gi, gf, ga, gm = cols
    gate_w = jnp.pad(jnp.concatenate([gi, gf], axis=1), ((0, 0), (0, LANES - 4 * M_HEADS)))
    gate_b = jnp.pad(jnp.concatenate([b_i[0], b_f[0]]), (0, LANES - 4 * M_HEADS))[None]
    row = lambda g: g.astype(F32)[None]
    both_halves = lambda g: jnp.tile(g[0].astype(F32), LANES // HEAD_DIM)[None]
    return {
        "g1": row(g_ffn1[0]), "w1a": w1_ffn1[0].astype(BF16), "w3a": w3_ffn1[0].astype(BF16),
        "w2a": w2_ffn1[0].astype(BF16),
        "gmix": row(g_mix[0]),
        "wq": _pair_heads(aq).astype(BF16), "wkv": jnp.concatenate([ak, av], axis=1).astype(BF16),
        "wm": jnp.concatenate([mq, mk, mv, mo], axis=1).astype(BF16),
        "wg": gate_w.astype(BF16), "gbias": gate_b.astype(F32),
        "wmerge": jnp.concatenate([ga, gm], axis=1).astype(BF16),
        "qgain": both_halves(q_gain), "kgain": both_halves(k_gain),
        "mgain": m_gain[0].astype(F32).reshape(1, M_WIDTH),
        "wab": _pair_heads(w_attn_br[0].T).T.astype(BF16), "wmb": w_mlstm_br[0].astype(BF16), "wout": w_out[0].astype(BF16),
        "g2": row(g_ffn2[0]), "w1b": w1_ffn2[0].astype(BF16), "w3b": w3_ffn2[0].astype(BF16),
        "w2b": w2_ffn2[0].astype(BF16),
    }


def _rope_tables(n_tok):
    t = jnp.arange(n_tok)
    pos = jnp.stack([(t // GRID_W).astype(F32), (t % GRID_W).astype(F32)], axis=-1)
    inv_freq = ROPE_THETA ** (-2.0 * jnp.arange(ROPE_FREQS, dtype=F32) / ROPE_AXIS_DIM)
    ang = pos[:, :, None] * inv_freq
    cos, sin = jnp.cos(ang), jnp.sin(ang)
    cos = jnp.stack([cos, cos], axis=2).reshape(n_tok, HEAD_DIM)
    sin = jnp.stack([-sin, sin], axis=2).reshape(n_tok, HEAD_DIM)
    reps = (1, LANES // HEAD_DIM)
    return jnp.tile(cos, reps), jnp.tile(sin, reps)


def _trunk(x, meta_parts, wts):
    batch, n_tok, d_model = x.shape
    k_meta, v_meta, mt_meta, m_meta, gates_meta, norms_meta = meta_parts
    cos, sin = _rope_tables(n_tok)
    h, q, k, v, mt, m, gates, merge, norms = _front_call(
        x.reshape(batch * n_tok, d_model), cos, sin, wts, FRONT_ROWS)
    q_rows, k_rows, chunk = min(ATTN_Q_ROWS, n_tok), min(ATTN_K_ROWS, n_tok), min(MLSTM_CHUNK, n_tok)
    assert n_tok % FRONT_ROWS == 0 and n_tok % q_rows == 0 and n_tok % k_rows == 0 and n_tok % chunk == 0
    assert q_rows % FRONT_ROWS == 0 and k_rows % FRONT_ROWS == 0
    safe, floor = _attn_safe_blocks(norms, norms_meta, batch, n_tok // q_rows, n_tok // k_rows)
    attn_o = _attn_call(q, k, v, k_meta, v_meta, safe, floor, batch, q_rows, k_rows)
    hf, hb = _mlstm_call(mt, m, gates, mt_meta, m_meta, gates_meta, batch, chunk)
    y = _back_call(h, attn_o, hf, hb, m, merge, wts, BACK_ROWS)
    return y.reshape(batch, n_tok, d_model)


def kernel(x_prompt, x_sample, meta, g_ffn1, w1_ffn1, w3_ffn1, w2_ffn1, g_mix, w_in, b_i, b_f, q_gain, k_gain,
           m_gain, w_attn_br, w_mlstm_br, w_out, g_ffn2, w1_ffn2, w3_ffn2, w2_ffn2):
    assert w_in.shape[0] == 1, "single-layer trunk: the meta rows' mixer outputs are never consumed"
    assert meta.shape[0] == N_META
    wts = _prepare_weights(g_ffn1, w1_ffn1, w3_ffn1, w2_ffn1, g_mix, w_in, b_i, b_f, q_gain, k_gain, m_gain,
                           w_attn_br, w_mlstm_br, w_out, g_ffn2, w1_ffn2, w3_ffn2, w2_ffn2)
    ones = jnp.ones((META_ROWS, LANES), F32)
    meta_rows = jnp.pad(meta.astype(F32), ((0, META_ROWS - N_META), (0, 0)))
    _, _, k_meta, v_meta, mt_meta, m_meta, gates_meta, _, norms_meta = _front_call(
        meta_rows, ones, jnp.zeros_like(ones), wts, META_ROWS)
    meta_parts = (k_meta, v_meta, mt_meta, m_meta, gates_meta, norms_meta)
    return (_trunk(x_prompt, meta_parts, wts), _trunk(x_sample, meta_parts, wts))
```

```python
import jax
import jax.numpy as jnp
import numpy as np
from jax import lax
from jax.experimental import pallas as pl
from jax.experimental.pallas import tpu as pltpu

F32 = jnp.float32
BF16 = jnp.bfloat16

N_META = 16
GRID_W = 64
EPS = 1e-6
NEG = -1e30
N_Q_HEADS = 8
N_KV_HEADS = 2
Q_PER_KV = N_Q_HEADS // N_KV_HEADS
HEAD_DIM = 64
ROPE_AXIS_DIM = HEAD_DIM // 2
ROPE_FREQS = ROPE_AXIS_DIM // 2
ROPE_THETA = 10000.0
M_HEADS = 4
M_HEAD_DIM = 128
M_WIDTH = M_HEADS * M_HEAD_DIM

LANES = 128
BF16_ROWS = 16
VMEM_LIMIT_BYTES = 60 * 1024 * 1024

SUB_ROWS = 256
FRONT_ROWS = 2 * SUB_ROWS
BACK_ROWS = 2 * SUB_ROWS
ATTN_Q_ROWS = 1024
ATTN_K_ROWS = 2048
MLSTM_CHUNK = 256

Q_PAD = N_Q_HEADS * LANES
ATTN_WIDTH = N_Q_HEADS * HEAD_DIM
KV_WIDTH = N_KV_HEADS * HEAD_DIM
assert KV_WIDTH == LANES and HEAD_DIM * 2 == LANES
Q_SCALE = HEAD_DIM ** -0.5 * 1.4426950408889634
ATTN_SAFE_LOG2 = 90.0
ATTN_V_ROWS = HEAD_DIM + BF16_ROWS
META_ROWS = LANES
STATE_ROWS = M_HEAD_DIM + BF16_ROWS


def _const_spec(shape):
    zeros = (0,) * len(shape)
    return pl.BlockSpec(shape, lambda *_: zeros, pipeline_mode=pl.Buffered(1))


def _rmsnorm(x, g):
    return x * lax.rsqrt(jnp.mean(x * x, axis=-1, keepdims=True) + EPS) * g


def _swiglu_half_residual(x, g, w1_ref, w3_ref, w2_ref):
    u = _rmsnorm(x, g).astype(BF16)
    a = jnp.dot(u, w1_ref[...], preferred_element_type=F32)
    b = jnp.dot(u, w3_ref[...], preferred_element_type=F32)
    act = (a * jax.nn.sigmoid(a) * b).astype(BF16)
    return x + 0.5 * jnp.dot(act, w2_ref[...], preferred_element_type=F32)


def _log_sigmoid(x):
    return jnp.minimum(x, 0.0) - jnp.log1p(jnp.exp(-jnp.abs(x)))


def _half_sums(x, low_half):
    low = jnp.sum(jnp.where(low_half, x, 0.0), axis=-1, keepdims=True)
    high = jnp.sum(jnp.where(low_half, 0.0, x), axis=-1, keepdims=True)
    return low, high


def _head_pair_norm_rope(x, gain, cos, sin_signed, low_half, first_half):
    ms_low, ms_high = _half_sums(x * x, low_half)
    inv = jnp.where(low_half, lax.rsqrt(ms_low * (1.0 / HEAD_DIM) + EPS), lax.rsqrt(ms_high * (1.0 / HEAD_DIM) + EPS))
    y = x * inv * gain
    partner = jnp.where(first_half, pltpu.roll(y, LANES - ROPE_FREQS, 1), pltpu.roll(y, ROPE_FREQS, 1))
    return y * cos + partner * sin_signed


def _front_kernel(x_ref, cos_ref, sin_ref, g1_ref, w1_ref, w3_ref, w2_ref, gmix_ref, wq_ref, wkv_ref,
                  wm_ref, wg_ref, wmerge_ref, qgain_ref, kgain_ref, gbias_ref,
                  h_ref, q_ref, k_ref, v_ref, mt_ref, m_ref, gate_ref, merge_ref, norm_ref):
    sub = min(x_ref.shape[0], SUB_ROWS)
    lane = lax.broadcasted_iota(jnp.int32, (sub, LANES), 1)
    low_half = lane < HEAD_DIM
    first_half = (lane % ROPE_AXIS_DIM) < ROPE_FREQS

    def sq_norms(y_bf16):
        y = y_bf16.astype(F32)
        return jnp.maximum(*_half_sums(y * y, low_half))

    q_sq = jnp.zeros((sub, 1), F32)
    k_sq = jnp.zeros((sub, 1), F32)
    for r in range(0, x_ref.shape[0], sub):
        rs = slice(r, r + sub)
        h = _swiglu_half_residual(x_ref[rs, :], g1_ref[...], w1_ref, w3_ref, w2_ref)
        h_ref[rs, :] = h
        u = _rmsnorm(h, gmix_ref[...]).astype(BF16)
        cos = cos_ref[rs, :]
        sin = sin_ref[rs, :]

        zq = jnp.dot(u, wq_ref[...], preferred_element_type=F32)
        for pair in range(Q_PER_KV):
            y = _head_pair_norm_rope(zq[:, pair * LANES:(pair + 1) * LANES], qgain_ref[...], cos, sin,
                                     low_half, first_half) * Q_SCALE
            q_sq = jnp.maximum(q_sq, sq_norms(y.astype(BF16)))
            for group, keep in enumerate((low_half, ~low_half)):
                hd = group * Q_PER_KV + pair
                q_ref[hd * LANES:(hd + 1) * LANES, rs] = jnp.where(keep, y, 0.0).T.astype(BF16)

        zkv = jnp.dot(u, wkv_ref[...], preferred_element_type=F32)
        k = _head_pair_norm_rope(zkv[:, :LANES], kgain_ref[...], cos, sin, low_half, first_half).astype(BF16)
        k_ref[rs, :] = k
        k_sq = jnp.maximum(k_sq, sq_norms(k))
        v_ref[:, rs] = zkv[:, LANES:].T.astype(BF16)

        zm = jnp.dot(u, wm_ref[...], preferred_element_type=F32)
        mt_ref[:M_WIDTH, rs] = zm[:, :M_WIDTH].T.astype(BF16)
        mt_ref[M_WIDTH:, rs] = zm[:, 2 * M_WIDTH:3 * M_WIDTH].T.astype(BF16)
        m_ref[rs, :M_WIDTH] = (zm[:, M_WIDTH:2 * M_WIDTH] * (M_HEAD_DIM ** -0.5)).astype(BF16)
        m_ref[rs, M_WIDTH:] = zm[:, 3 * M_WIDTH:].astype(BF16)

        zg = jnp.dot(u, wg_ref[...], preferred_element_type=F32) + gbias_ref[...]
        gate_ref[rs, :] = jnp.where(lane < 2 * M_HEADS, zg, _log_sigmoid(zg))

        zmerge = jnp.dot(u, wmerge_ref[...], preferred_element_type=F32)
        merge_ref[rs, :] = jax.nn.sigmoid(zmerge).astype(BF16)

    tile_row = lax.broadcasted_iota(jnp.int32, norm_ref.shape, 0)
    norm_ref[...] = jnp.where(tile_row == 0, jnp.max(q_sq, axis=0, keepdims=True),
                              jnp.max(k_sq, axis=0, keepdims=True))


def _front_call(x, cos, sin, wts, rows):
    tokens, d_model = x.shape
    steps = tokens // rows
    table_steps = cos.shape[0] // rows

    def row_spec(width):
        return pl.BlockSpec((rows, width), lambda i: (i, 0))

    table_spec = pl.BlockSpec((rows, LANES), lambda i: (i % table_steps, 0))
    consts = (wts["g1"], wts["w1a"], wts["w3a"], wts["w2a"], wts["gmix"], wts["wq"], wts["wkv"],
              wts["wm"], wts["wg"], wts["wmerge"], wts["qgain"], wts["kgain"], wts["gbias"])

    def col_spec(height):
        return pl.BlockSpec((height, rows), lambda i: (0, i))

    out_specs = [row_spec(d_model), col_spec(Q_PAD), row_spec(KV_WIDTH), col_spec(KV_WIDTH), col_spec(2 * M_WIDTH),
                 row_spec(2 * M_WIDTH), row_spec(LANES), row_spec(2 * d_model),
                 pl.BlockSpec((None, 8, LANES), lambda i: (i, 0, 0))]
    out_shape = [jax.ShapeDtypeStruct((tokens, d_model), F32), jax.ShapeDtypeStruct((Q_PAD, tokens), BF16),
                 jax.ShapeDtypeStruct((tokens, KV_WIDTH), BF16), jax.ShapeDtypeStruct((KV_WIDTH, tokens), BF16),
                 jax.ShapeDtypeStruct((2 * M_WIDTH, tokens), BF16),
                 jax.ShapeDtypeStruct((tokens, 2 * M_WIDTH), BF16), jax.ShapeDtypeStruct((tokens, LANES), F32),
                 jax.ShapeDtypeStruct((tokens, 2 * d_model), BF16), jax.ShapeDtypeStruct((steps, 8, LANES), F32)]
    return pl.pallas_call(
        _front_kernel,
        grid=(steps,),
        in_specs=[row_spec(d_model), table_spec, table_spec] + [_const_spec(c.shape) for c in consts],
        out_specs=out_specs,
        out_shape=out_shape,
        compiler_params=pltpu.CompilerParams(dimension_semantics=("parallel",),
                                             vmem_limit_bytes=VMEM_LIMIT_BYTES),
        name="front",
    )(x, cos, sin, *consts)


def _attn_kernel(safe_ref, floor_ref, qt_ref, k_ref, vt_ref, kmeta_ref, vtmeta_ref, o_ref, m_scr, acc_scr):
    ki = pl.program_id(2)
    q_cols = qt_ref.shape[1]
    step = (pl.program_id(0) * pl.num_programs(1) + pl.program_id(1)) * pl.num_programs(2) + ki
    safe = safe_ref[step]

    def scores_t(hd, keys_ref):
        return jnp.dot(keys_ref[...], qt_ref[hd * LANES:(hd + 1) * LANES, :], preferred_element_type=F32)

    def values_t(hd, values_ref):
        g = hd // Q_PER_KV
        row = lax.broadcasted_iota(jnp.int32, (ATTN_V_ROWS - HEAD_DIM, values_ref.shape[1]), 0)
        ones = jnp.where(row == 0, 1.0, 0.0).astype(BF16)
        return jnp.concatenate([values_ref[g * HEAD_DIM:(g + 1) * HEAD_DIM, :], ones], axis=0)

    def meta_scores_t(hd):
        real = lax.broadcasted_iota(jnp.int32, (kmeta_ref.shape[0], q_cols), 0) < N_META
        return jnp.where(real, scores_t(hd, kmeta_ref), NEG)

    @pl.when((ki == 0) & (safe == 1))
    def _seed_from_meta_keys_against_floor():
        floor = jnp.full((1, q_cols), floor_ref[pl.program_id(0) * pl.num_programs(1) + pl.program_id(1)], F32)
        scores = [jnp.dot(kmeta_ref[:N_META, :], qt_ref[hd * LANES:(hd + 1) * LANES, :],
                          preferred_element_type=F32) for hd in range(N_Q_HEADS)]
        for hd, s in enumerate(scores):
            p = jnp.exp2(s - floor).astype(BF16)
            acc = jnp.dot(values_t(hd, vtmeta_ref)[:, :N_META], p, preferred_element_type=F32)
            m = jnp.maximum(floor, jnp.max(s, axis=0, keepdims=True))
            acc_scr[hd] = jnp.exp2(floor - m) * acc
            m_scr[hd] = jnp.broadcast_to(m, m_scr.shape[1:])

    @pl.when((ki == 0) & (safe != 1))
    def _seed_from_meta_keys():
        for hd in range(N_Q_HEADS):
            s = meta_scores_t(hd)
            m = jnp.max(s, axis=0, keepdims=True)
            p = jnp.exp2(s - m).astype(BF16)
            acc_scr[hd] = jnp.dot(values_t(hd, vtmeta_ref), p, preferred_element_type=F32)
            m_scr[hd] = jnp.broadcast_to(m, m_scr.shape[1:])

    @pl.when(safe == 1)
    def _block_against_previous_maximum():
        for hd in range(N_Q_HEADS):
            m_old = m_scr[hd][:1]
            s = scores_t(hd, k_ref)
            p = jnp.exp2(s - m_old).astype(BF16)
            pv = jnp.dot(values_t(hd, vt_ref), p, preferred_element_type=F32)
            m_new = jnp.maximum(m_old, jnp.max(s, axis=0, keepdims=True))
            acc_scr[hd] = jnp.exp2(m_old - m_new) * (acc_scr[hd] + pv)
            m_scr[hd] = jnp.broadcast_to(m_new, m_scr.shape[1:])

    @pl.when(safe != 1)
    def _block_against_own_maximum():
        for hd in range(N_Q_HEADS):
            m_old = m_scr[hd][:1]
            s = scores_t(hd, k_ref)
            m_new = jnp.maximum(m_old, jnp.max(s, axis=0, keepdims=True))
            p = jnp.exp2(s - m_new).astype(BF16)
            pv = jnp.dot(values_t(hd, vt_ref), p, preferred_element_type=F32)
            acc_scr[hd] = jnp.exp2(m_old - m_new) * acc_scr[hd] + pv
            m_scr[hd] = jnp.broadcast_to(m_new, m_scr.shape[1:])

    @pl.when(ki == pl.num_programs(2) - 1)
    def _normalise():
        def normalised(hd):
            acc = acc_scr[hd]
            return acc[:HEAD_DIM] / acc[HEAD_DIM:HEAD_DIM + 1]

        for pair in range(Q_PER_KV):
            both_t = jnp.concatenate([normalised(pair), normalised(Q_PER_KV + pair)], axis=0)
            o_ref[:, pair * LANES:(pair + 1) * LANES] = both_t.T.astype(BF16)


def _attn_safe_blocks(norms, norms_meta, batch, nq, nk):
    def block_max(sq, blocks):
        return jnp.sqrt(jnp.max(sq.reshape(blocks, -1), axis=1))

    q_norm = block_max(norms[:, 0, 0], batch * nq).reshape(batch, nq, 1)
    k_norm = block_max(norms[:, 1, 0], batch * nk).reshape(batch, 1, nk)
    k_meta_norm = jnp.sqrt(jnp.max(norms_meta[:, 1, 0]))
    gap = q_norm * (jnp.maximum(k_norm, k_meta_norm) + k_meta_norm)
    safe = (gap <= ATTN_SAFE_LOG2).astype(jnp.int32).reshape(-1)
    floor = (-q_norm * k_meta_norm).reshape(-1)
    return safe, floor


def _attn_call(qt, k, vt, k_meta, vt_meta, safe, floor, batch, q_rows, k_rows):
    tokens = k.shape[0]
    n_tok = tokens // batch
    nq = n_tok // q_rows
    nk = n_tok // k_rows
    grid_spec = pltpu.PrefetchScalarGridSpec(
        num_scalar_prefetch=2,
        grid=(batch, nq, nk),
        in_specs=[
            pl.BlockSpec((Q_PAD, q_rows), lambda b, i, j, *_: (0, b * nq + i)),
            pl.BlockSpec((k_rows, KV_WIDTH), lambda b, i, j, *_: (b * nk + j, 0)),
            pl.BlockSpec((KV_WIDTH, k_rows), lambda b, i, j, *_: (0, b * nk + j)),
            pl.BlockSpec(k_meta.shape, lambda b, i, j, *_: (0, 0)),
            pl.BlockSpec(vt_meta.shape, lambda b, i, j, *_: (0, 0)),
        ],
        out_specs=pl.BlockSpec((q_rows, ATTN_WIDTH), lambda b, i, j, *_: (b * nq + i, 0)),
        scratch_shapes=[pltpu.VMEM((N_Q_HEADS, 8, q_rows), F32),
                        pltpu.VMEM((N_Q_HEADS, ATTN_V_ROWS, q_rows), F32)],
    )
    return pl.pallas_call(
        _attn_kernel,
        grid_spec=grid_spec,
        out_shape=jax.ShapeDtypeStruct((tokens, ATTN_WIDTH), BF16),
        compiler_params=pltpu.CompilerParams(dimension_semantics=("parallel", "parallel", "arbitrary"),
                                             vmem_limit_bytes=VMEM_LIMIT_BYTES),
        name="attn",
    )(safe, floor, qt, k, vt, k_meta, vt_meta)


def _split_hi_lo(x):
    hi = x.astype(BF16)
    lo = (x - hi.astype(F32)).astype(BF16)
    return hi, lo


def _gate_cumsum(gates, causal):
    t = gates.shape[0]
    row = lax.broadcasted_iota(jnp.int32, (t, t), 0)
    col = lax.broadcasted_iota(jnp.int32, (t, t), 1)
    mask = (col <= row) if causal else (col >= row)
    tri = jnp.where(mask, 1.0, 0.0).astype(BF16)
    hi, lo = _split_hi_lo(gates)
    sums = jnp.dot(tri, hi, preferred_element_type=F32) + jnp.dot(tri, lo, preferred_element_type=F32)
    return sums, mask


def _values_t_with_ones(vt):
    row = lax.broadcasted_iota(jnp.int32, (STATE_ROWS - M_HEAD_DIM, vt.shape[1]), 0)
    return jnp.concatenate([vt, jnp.where(row == 0, 1.0, 0.0).astype(vt.dtype)], axis=0)


def _state_update(state_t, m_in, k, vaug_t, li_row, b_row, b_tot):
    w_end = b_tot - b_row + li_row
    m_new = jnp.maximum(b_tot + m_in, jnp.max(w_end, axis=1, keepdims=True))
    vw = (vaug_t.astype(F32) * jnp.exp(w_end - m_new)).astype(BF16)
    return jnp.exp(b_tot + m_in - m_new) * state_t + jnp.dot(vw, k, preferred_element_type=F32), m_new


def _mlstm_kernel(qtf_ref, vtf_ref, kf_ref, gf_ref, qtb_ref, vtb_ref, kb_ref, gb_ref, kmeta_ref, vtmeta_ref,
                  gmeta_ref, hf_ref, hb_ref, state_scr, m_scr):
    c = pl.program_id(1)

    @pl.when(c == 0)
    def _reset_and_absorb_meta():
        state_scr[...] = jnp.zeros_like(state_scr)
        m_scr[...] = jnp.zeros_like(m_scr)
        gates = gmeta_ref[...]
        row = lax.broadcasted_iota(jnp.int32, gates.shape, 0)
        lane = lax.broadcasted_iota(jnp.int32, gates.shape, 1)
        gates = jnp.where(row < N_META, gates, jnp.where(lane < 2 * M_HEADS, NEG, 0.0))
        sums, _ = _gate_cumsum(gates, causal=True)
        gates_t = gates.T
        sums_t = sums.T
        last = gates.shape[0] - 1
        for hd in range(M_HEADS):
            f_lane = 2 * M_HEADS + hd
            sl = slice(hd * M_HEAD_DIM, (hd + 1) * M_HEAD_DIM)
            vt = vtmeta_ref[M_WIDTH + hd * M_HEAD_DIM:M_WIDTH + (hd + 1) * M_HEAD_DIM, :]
            new_state, m_new = _state_update(
                state_scr[hd], m_scr[hd][:1, :1], kmeta_ref[:, sl], _values_t_with_ones(vt),
                gates_t[hd:hd + 1, :], sums_t[f_lane:f_lane + 1, :], sums[last:last + 1, f_lane:f_lane + 1])
            state_scr[hd] = new_state
            m_scr[hd] = jnp.broadcast_to(m_new, m_scr.shape[1:])

    chains = []
    for causal, qt_ref, vt_ref, k_ref, g_ref, o_ref in ((True, qtf_ref, vtf_ref, kf_ref, gf_ref, hf_ref),
                                                        (False, qtb_ref, vtb_ref, kb_ref, gb_ref, hb_ref)):
        gates = g_ref[...]
        t = gates.shape[0]
        sums, _ = _gate_cumsum(gates, causal)
        gates_t = gates.T
        sums_t = sums.T
        src = lax.broadcasted_iota(jnp.int32, (t, t), 0)
        tgt = lax.broadcasted_iota(jnp.int32, (t, t), 1)
        feeds = (src <= tgt) if causal else (src >= tgt)
        last = t - 1 if causal else 0
        for hd in range(M_HEADS):
            chain = hd if causal else M_HEADS + hd
            i_lane, f_lane = chain, 2 * M_HEADS + chain
            chains.append(dict(
                chain=chain, feeds=feeds, o_ref=o_ref, qt_ref=qt_ref, vt_ref=vt_ref, k_ref=k_ref,
                sl=slice(hd * M_HEAD_DIM, (hd + 1) * M_HEAD_DIM), t=t,
                r_col=gates[:, i_lane:i_lane + 1] - sums[:, f_lane:f_lane + 1],
                li_row=gates_t[i_lane:i_lane + 1, :], b_row=sums_t[f_lane:f_lane + 1, :],
                b_tot=sums[last:last + 1, f_lane:f_lane + 1], m_in=m_scr[chain][:1, :1]))

    for ch in chains:
        lhs = jnp.concatenate([ch["k_ref"][:, ch["sl"]], state_scr[ch["chain"]].astype(BF16)], axis=0)
        both = jnp.dot(lhs, ch["qt_ref"][ch["sl"], :], preferred_element_type=F32)
        ch["kq"], ch["inter"] = both[:ch["t"]], both[ch["t"]:]
    for ch in chains:
        ch["r"] = jnp.where(ch["feeds"], ch["r_col"], NEG)
        ch["g"] = jnp.maximum(jnp.max(ch["r"], axis=0, keepdims=True), ch["m_in"])
    for ch in chains:
        ch["s"] = (ch["kq"] * jnp.exp(ch["r"] - ch["g"])).astype(BF16)
        ch["vaug"] = _values_t_with_ones(ch["vt_ref"][ch["sl"], :])
        w_end = ch["b_tot"] - ch["b_row"] + ch["li_row"]
        ch["m_new"] = jnp.maximum(ch["b_tot"] + ch["m_in"], jnp.max(w_end, axis=1, keepdims=True))
        ch["vw"] = (ch["vaug"].astype(F32) * jnp.exp(w_end - ch["m_new"])).astype(BF16)
    for ch in chains:
        ch["tot"] = jnp.dot(ch["vaug"], ch["s"], preferred_element_type=F32)
        ch["local"] = jnp.dot(ch["vw"], ch["k_ref"][:, ch["sl"]], preferred_element_type=F32)
    for ch in chains:
        tot = ch["tot"] + jnp.exp(ch["m_in"] - ch["g"]) * ch["inter"]
        den = jnp.maximum(jnp.abs(tot[M_HEAD_DIM:M_HEAD_DIM + 1]), jnp.exp(-(ch["b_row"] + ch["g"])))
        ch["o_ref"][ch["sl"], :] = tot[:M_HEAD_DIM] / den
        chain = ch["chain"]
        state_scr[chain] = jnp.exp(ch["b_tot"] + ch["m_in"] - ch["m_new"]) * state_scr[chain] + ch["local"]
        m_scr[chain] = jnp.broadcast_to(ch["m_new"], m_scr.shape[1:])


def _mlstm_call(mt, m, gates, mt_meta, m_meta, gates_meta, batch, chunk):
    tokens = m.shape[0]
    nc = tokens // batch // chunk

    def chunk_of(direction):
        return (lambda b, c: b * nc + c) if direction == "fwd" else (lambda b, c: b * nc + nc - 1 - c)

    def specs(direction):
        at = chunk_of(direction)
        return [pl.BlockSpec((M_WIDTH, chunk), lambda b, c: (0, at(b, c))),
                pl.BlockSpec((M_WIDTH, chunk), lambda b, c: (1, at(b, c))),
                pl.BlockSpec((chunk, M_WIDTH), lambda b, c: (at(b, c), 0)),
                pl.BlockSpec((chunk, LANES), lambda b, c: (at(b, c), 0))]

    def out_spec(direction):
        at = chunk_of(direction)
        return pl.BlockSpec((M_WIDTH, chunk), lambda b, c: (0, at(b, c)))

    return pl.pallas_call(
        _mlstm_kernel,
        grid=(batch, nc),
        in_specs=specs("fwd") + specs("bwd") + [pl.BlockSpec(m_meta.shape, lambda b, c: (0, 0)),
                                                pl.BlockSpec(mt_meta.shape, lambda b, c: (0, 0)),
                                                pl.BlockSpec(gates_meta.shape, lambda b, c: (0, 0))],
        out_specs=[out_spec("fwd"), out_spec("bwd")],
        out_shape=[jax.ShapeDtypeStruct((M_WIDTH, tokens), F32)] * 2,
        scratch_shapes=[pltpu.VMEM((2 * M_HEADS, STATE_ROWS, M_HEAD_DIM), F32),
                        pltpu.VMEM((2 * M_HEADS, 8, LANES), F32)],
        compiler_params=pltpu.CompilerParams(dimension_semantics=("parallel", "arbitrary"),
                                             vmem_limit_bytes=VMEM_LIMIT_BYTES),
        name="mlstm",
    )(mt, mt, m, gates, mt, mt, m, gates, m_meta, mt_meta, gates_meta)


def _back_kernel(h_ref, ao_ref, hf_ref, hb_ref, mo_ref, merge_ref, mgain_ref, wab_ref, wmb_ref, wout_ref,
                 g2_ref, w1_ref, w3_ref, w2_ref, y_ref):
    rows, d_model = h_ref.shape
    for r in range(0, rows, SUB_ROWS):
        rs = slice(r, r + SUB_ROWS)
        a_out = jnp.dot(ao_ref[rs, :], wab_ref[...], preferred_element_type=F32)

        hm = (hf_ref[:, rs] + hb_ref[:, rs]).T
        o_gate = jax.nn.sigmoid(mo_ref[rs, :].astype(F32))
        gated = []
        for hd in range(M_HEADS):
            sl = slice(hd * M_HEAD_DIM, (hd + 1) * M_HEAD_DIM)
            gated.append((_rmsnorm(hm[:, sl], mgain_ref[:, sl]) * o_gate[:, sl]).astype(BF16))
        m_out = jnp.dot(jnp.concatenate(gated, axis=1), wmb_ref[...], preferred_element_type=F32)

        merged = (merge_ref[rs, :d_model].astype(F32) * a_out + merge_ref[rs, d_model:].astype(F32) * m_out)
        h2 = h_ref[rs, :] + jnp.dot(merged.astype(BF16), wout_ref[...], preferred_element_type=F32)
        y_ref[rs, :] = _swiglu_half_residual(h2, g2_ref[...], w1_ref, w3_ref, w2_ref)


def _back_call(h, attn_o, hf, hb, m, merge, wts, rows):
    tokens, d_model = h.shape

    def row_spec(width, col=0):
        return pl.BlockSpec((rows, width), lambda i: (i, col))

    col_spec = pl.BlockSpec((M_WIDTH, rows), lambda i: (0, i))
    consts = (wts["mgain"], wts["wab"], wts["wmb"], wts["wout"], wts["g2"], wts["w1b"], wts["w3b"], wts["w2b"])
    return pl.pallas_call(
        _back_kernel,
        grid=(tokens // rows,),
        in_specs=[row_spec(d_model), row_spec(ATTN_WIDTH), col_spec, col_spec,
                  row_spec(M_WIDTH, 1), row_spec(2 * d_model)] + [_const_spec(c.shape) for c in consts],
        out_specs=row_spec(d_model),
        out_shape=jax.ShapeDtypeStruct((tokens, d_model), F32),
        compiler_params=pltpu.CompilerParams(dimension_semantics=("parallel",),
                                             vmem_limit_bytes=VMEM_LIMIT_BYTES),
        name="back",
    )(h, attn_o, hf, hb, m, merge, *consts)


def _pair_heads(w):
    lead = w.shape[:-1]
    w = w.reshape(*lead, N_KV_HEADS, Q_PER_KV, HEAD_DIM)
    return jnp.swapaxes(w, -3, -2).reshape(*lead, ATTN_WIDTH)


def _prepare_weights(g_ffn1, w1_ffn1, w3_ffn1, w2_ffn1, g_mix, w_in, b_i, b_f, q_gain, k_gain, m_gain,
                     w_attn_br, w_mlstm_br, w_out, g_ffn2, w1_ffn2, w3_ffn2, w2_ffn2):
    d_model = w_in.shape[1]
    attn_w = N_Q_HEADS * HEAD_DIM
    kv_w = N_KV_HEADS * HEAD_DIM
    splits = (attn_w, kv_w, kv_w, M_WIDTH, M_WIDTH, M_WIDTH, M_WIDTH, 2 * M_HEADS, 2 * M_HEADS, d_model, d_model)
    offs = np.cumsum((0,) + splits)
    assert offs[-1] == w_in.shape[2]
    cols = [w_in[0, :, offs[i]:offs[i + 1]] for i in range(len(splits))]
    aq, ak, av, mq, mk, mv, mo, gi, gf, ga, gm = cols
    gate_w = jnp.pad(jnp.concatenate([gi, gf], axis=1), ((0, 0), (0, LANES - 4 * M_HEADS)))
    gate_b = jnp.pad(jnp.concatenate([b_i[0], b_f[0]]), (0, LANES - 4 * M_HEADS))[None]
    row = lambda g: g.astype(F32)[None]
    both_halves = lambda g: jnp.tile(g[0].astype(F32), LANES // HEAD_DIM)[None]
    return {
        "g1": row(g_ffn1[0]), "w1a": w1_ffn1[0].astype(BF16), "w3a": w3_ffn1[0].astype(BF16),
        "w2a": w2_ffn1[0].astype(BF16),
        "gmix": row(g_mix[0]),
        "wq": _pair_heads(aq).astype(BF16), "wkv": jnp.concatenate([ak, av], axis=1).astype(BF16),
        "wm": jnp.concatenate([mq, mk, mv, mo], axis=1).astype(BF16),
        "wg": gate_w.astype(BF16), "gbias": gate_b.astype(F32),
        "wmerge": jnp.concatenate([ga, gm], axis=1).astype(BF16),
        "qgain": both_halves(q_gain), "kgain": both_halves(k_gain),
        "mgain": m_gain[0].astype(F32).reshape(1, M_WIDTH),
        "wab": _pair_heads(w_attn_br[0].T).T.astype(BF16), "wmb": w_mlstm_br[0].astype(BF16), "wout": w_out[0].astype(BF16),
        "g2": row(g_ffn2[0]), "w1b": w1_ffn2[0].astype(BF16), "w3b": w3_ffn2[0].astype(BF16),
        "w2b": w2_ffn2[0].astype(BF16),
    }


def _rope_tables(n_tok):
    t = jnp.arange(n_tok)
    pos = jnp.stack([(t // GRID_W).astype(F32), (t % GRID_W).astype(F32)], axis=-1)
    inv_freq = ROPE_THETA ** (-2.0 * jnp.arange(ROPE_FREQS, dtype=F32) / ROPE_AXIS_DIM)
    ang = pos[:, :, None] * inv_freq
    cos, sin = jnp.cos(ang), jnp.sin(ang)
    cos = jnp.stack([cos, cos], axis=2).reshape(n_tok, HEAD_DIM)
    sin = jnp.stack([-sin, sin], axis=2).reshape(n_tok, HEAD_DIM)
    reps = (1, LANES // HEAD_DIM)
    return jnp.tile(cos, reps), jnp.tile(sin, reps)


def _trunk(x, meta_parts, wts):
    batch, n_tok, d_model = x.shape
    k_meta, v_meta, mt_meta, m_meta, gates_meta, norms_meta = meta_parts
    cos, sin = _rope_tables(n_tok)
    h, q, k, v, mt, m, gates, merge, norms = _front_call(
        x.reshape(batch * n_tok, d_model), cos, sin, wts, FRONT_ROWS)
    q_rows, k_rows, chunk = min(ATTN_Q_ROWS, n_tok), min(ATTN_K_ROWS, n_tok), min(MLSTM_CHUNK, n_tok)
    assert n_tok % FRONT_ROWS == 0 and n_tok % q_rows == 0 and n_tok % k_rows == 0 and n_tok % chunk == 0
    assert q_rows % FRONT_ROWS == 0 and k_rows % FRONT_ROWS == 0
    safe, floor = _attn_safe_blocks(norms, norms_meta, batch, n_tok // q_rows, n_tok // k_rows)
    attn_o = _attn_call(q, k, v, k_meta, v_meta, safe, floor, batch, q_rows, k_rows)
    hf, hb = _mlstm_call(mt, m, gates, mt_meta, m_meta, gates_meta, batch, chunk)
    y = _back_call(h, attn_o, hf, hb, m, merge, wts, BACK_ROWS)
    return y.reshape(batch, n_tok, d_model)


def kernel(x_prompt, x_sample, meta, g_ffn1, w1_ffn1, w3_ffn1, w2_ffn1, g_mix, w_in, b_i, b_f, q_gain, k_gain,
           m_gain, w_attn_br, w_mlstm_br, w_out, g_ffn2, w1_ffn2, w3_ffn2, w2_ffn2):
    assert w_in.shape[0] == 1, "single-layer trunk: the meta rows' mixer outputs are never consumed"
    assert meta.shape[0] == N_META
    wts = _prepare_weights(g_ffn1, w1_ffn1, w3_ffn1, w2_ffn1, g_mix, w_in, b_i, b_f, q_gain, k_gain, m_gain,
                           w_attn_br, w_mlstm_br, w_out, g_ffn2, w1_ffn2, w3_ffn2, w2_ffn2)
    ones = jnp.ones((META_ROWS, LANES), F32)
    meta_rows = jnp.pad(meta.astype(F32), ((0, META_ROWS - N_META), (0, 0)))
    _, _, k_meta, v_meta, mt_meta, m_meta, gates_meta, _, norms_meta = _front_call(
        meta_rows, ones, jnp.zeros_like(ones), wts, META_ROWS)
    meta_parts = (k_meta, v_meta, mt_meta, m_meta, gates_meta, norms_meta)
    return (_trunk(x_prompt, meta_parts, wts), _trunk(x_sample, meta_parts, wts))
```

```python
import jax
import jax.numpy as jnp
import numpy as np
from jax import lax
from jax.experimental import pallas as pl
from jax.experimental.pallas import tpu as pltpu

F32 = jnp.float32
BF16 = jnp.bfloat16

N_META = 16
GRID_W = 64
EPS = 1e-6
NEG = -1e30
N_Q_HEADS = 8
N_KV_HEADS = 2
Q_PER_KV = N_Q_HEADS // N_KV_HEADS
HEAD_DIM = 64
ROPE_AXIS_DIM = HEAD_DIM // 2
ROPE_FREQS = ROPE_AXIS_DIM // 2
ROPE_THETA = 10000.0
M_HEADS = 4
M_HEAD_DIM = 128
M_WIDTH = M_HEADS * M_HEAD_DIM

LANES = 128
BF16_ROWS = 16
VMEM_LIMIT_BYTES = 60 * 1024 * 1024

SUB_ROWS = 256
FRONT_ROWS = 2 * SUB_ROWS
BACK_ROWS = 2 * SUB_ROWS
ATTN_Q_ROWS = 1024
ATTN_K_ROWS = 2048
MLSTM_CHUNK = 256
MLSTM_STEP_TOKENS = 2 * MLSTM_CHUNK

Q_PAD = N_Q_HEADS * LANES
ATTN_WIDTH = N_Q_HEADS * HEAD_DIM
KV_WIDTH = N_KV_HEADS * HEAD_DIM
assert KV_WIDTH == LANES and HEAD_DIM * 2 == LANES
Q_SCALE = HEAD_DIM ** -0.5 * 1.4426950408889634
ATTN_SAFE_LOG2 = 90.0
ATTN_V_ROWS = HEAD_DIM + BF16_ROWS
META_ROWS = LANES
STATE_ROWS = M_HEAD_DIM + BF16_ROWS


def _const_spec(shape):
    zeros = (0,) * len(shape)
    return pl.BlockSpec(shape, lambda *_: zeros, pipeline_mode=pl.Buffered(1))


def _rmsnorm(x, g):
    return x * lax.rsqrt(jnp.mean(x * x, axis=-1, keepdims=True) + EPS) * g


def _swiglu_half_residual(x, g, w1_ref, w3_ref, w2_ref):
    u = _rmsnorm(x, g).astype(BF16)
    a = jnp.dot(u, w1_ref[...], preferred_element_type=F32)
    b = jnp.dot(u, w3_ref[...], preferred_element_type=F32)
    act = (a * jax.nn.sigmoid(a) * b).astype(BF16)
    return x + 0.5 * jnp.dot(act, w2_ref[...], preferred_element_type=F32)


def _log_sigmoid(x):
    return jnp.minimum(x, 0.0) - jnp.log1p(jnp.exp(-jnp.abs(x)))


def _half_sums(x, low_half):
    low = jnp.sum(jnp.where(low_half, x, 0.0), axis=-1, keepdims=True)
    high = jnp.sum(jnp.where(low_half, 0.0, x), axis=-1, keepdims=True)
    return low, high


def _head_pair_norm_rope(x, gain, cos, sin_signed, low_half, first_half):
    ms_low, ms_high = _half_sums(x * x, low_half)
    inv = jnp.where(low_half, lax.rsqrt(ms_low * (1.0 / HEAD_DIM) + EPS), lax.rsqrt(ms_high * (1.0 / HEAD_DIM) + EPS))
    y = x * inv * gain
    partner = jnp.where(first_half, pltpu.roll(y, LANES - ROPE_FREQS, 1), pltpu.roll(y, ROPE_FREQS, 1))
    return y * cos + partner * sin_signed


def _front_kernel(x_ref, cos_ref, sin_ref, g1_ref, w1_ref, w3_ref, w2_ref, gmix_ref, wq_ref, wkv_ref,
                  wm_ref, wg_ref, wmerge_ref, qgain_ref, kgain_ref, gbias_ref,
                  h_ref, q_ref, k_ref, v_ref, mt_ref, m_ref, gate_ref, merge_ref, norm_ref):
    sub = min(x_ref.shape[0], SUB_ROWS)
    lane = lax.broadcasted_iota(jnp.int32, (sub, LANES), 1)
    low_half = lane < HEAD_DIM
    first_half = (lane % ROPE_AXIS_DIM) < ROPE_FREQS

    def sq_norms(y_bf16):
        y = y_bf16.astype(F32)
        return jnp.maximum(*_half_sums(y * y, low_half))

    q_sq = jnp.zeros((sub, 1), F32)
    k_sq = jnp.zeros((sub, 1), F32)
    for r in range(0, x_ref.shape[0], sub):
        rs = slice(r, r + sub)
        h = _swiglu_half_residual(x_ref[rs, :], g1_ref[...], w1_ref, w3_ref, w2_ref)
        h_ref[rs, :] = h
        u = _rmsnorm(h, gmix_ref[...]).astype(BF16)
        cos = cos_ref[rs, :]
        sin = sin_ref[rs, :]

        zq = jnp.dot(u, wq_ref[...], preferred_element_type=F32)
        for pair in range(Q_PER_KV):
            y = _head_pair_norm_rope(zq[:, pair * LANES:(pair + 1) * LANES], qgain_ref[...], cos, sin,
                                     low_half, first_half) * Q_SCALE
            q_sq = jnp.maximum(q_sq, sq_norms(y.astype(BF16)))
            for group, keep in enumerate((low_half, ~low_half)):
                hd = group * Q_PER_KV + pair
                q_ref[hd * LANES:(hd + 1) * LANES, rs] = jnp.where(keep, y, 0.0).T.astype(BF16)

        zkv = jnp.dot(u, wkv_ref[...], preferred_element_type=F32)
        k = _head_pair_norm_rope(zkv[:, :LANES], kgain_ref[...], cos, sin, low_half, first_half).astype(BF16)
        k_ref[rs, :] = k
        k_sq = jnp.maximum(k_sq, sq_norms(k))
        v_ref[:, rs] = zkv[:, LANES:].T.astype(BF16)

        zm = jnp.dot(u, wm_ref[...], preferred_element_type=F32)
        mt_ref[:M_WIDTH, rs] = zm[:, :M_WIDTH].T.astype(BF16)
        mt_ref[M_WIDTH:, rs] = zm[:, 2 * M_WIDTH:3 * M_WIDTH].T.astype(BF16)
        m_ref[rs, :M_WIDTH] = (zm[:, M_WIDTH:2 * M_WIDTH] * (M_HEAD_DIM ** -0.5)).astype(BF16)
        m_ref[rs, M_WIDTH:] = zm[:, 3 * M_WIDTH:].astype(BF16)

        zg = jnp.dot(u, wg_ref[...], preferred_element_type=F32) + gbias_ref[...]
        gate_ref[rs, :] = jnp.where(lane < 2 * M_HEADS, zg, _log_sigmoid(zg))

        zmerge = jnp.dot(u, wmerge_ref[...], preferred_element_type=F32)
        merge_ref[rs, :] = jax.nn.sigmoid(zmerge).astype(BF16)

    tile_row = lax.broadcasted_iota(jnp.int32, norm_ref.shape, 0)
    norm_ref[...] = jnp.where(tile_row == 0, jnp.max(q_sq, axis=0, keepdims=True),
                              jnp.max(k_sq, axis=0, keepdims=True))


def _front_call(x, cos, sin, wts, rows):
    tokens, d_model = x.shape
    steps = tokens // rows
    table_steps = cos.shape[0] // rows

    def row_spec(width):
        return pl.BlockSpec((rows, width), lambda i: (i, 0))

    table_spec = pl.BlockSpec((rows, LANES), lambda i: (i % table_steps, 0))
    consts = (wts["g1"], wts["w1a"], wts["w3a"], wts["w2a"], wts["gmix"], wts["wq"], wts["wkv"],
              wts["wm"], wts["wg"], wts["wmerge"], wts["qgain"], wts["kgain"], wts["gbias"])

    def col_spec(height):
        return pl.BlockSpec((height, rows), lambda i: (0, i))

    out_specs = [row_spec(d_model), col_spec(Q_PAD), row_spec(KV_WIDTH), col_spec(KV_WIDTH), col_spec(2 * M_WIDTH),
                 row_spec(2 * M_WIDTH), row_spec(LANES), row_spec(2 * d_model),
                 pl.BlockSpec((None, 8, LANES), lambda i: (i, 0, 0))]
    out_shape = [jax.ShapeDtypeStruct((tokens, d_model), F32), jax.ShapeDtypeStruct((Q_PAD, tokens), BF16),
                 jax.ShapeDtypeStruct((tokens, KV_WIDTH), BF16), jax.ShapeDtypeStruct((KV_WIDTH, tokens), BF16),
                 jax.ShapeDtypeStruct((2 * M_WIDTH, tokens), BF16),
                 jax.ShapeDtypeStruct((tokens, 2 * M_WIDTH), BF16), jax.ShapeDtypeStruct((tokens, LANES), F32),
                 jax.ShapeDtypeStruct((tokens, 2 * d_model), BF16), jax.ShapeDtypeStruct((steps, 8, LANES), F32)]
    return pl.pallas_call(
        _front_kernel,
        grid=(steps,),
        in_specs=[row_spec(d_model), table_spec, table_spec] + [_const_spec(c.shape) for c in consts],
        out_specs=out_specs,
        out_shape=out_shape,
        compiler_params=pltpu.CompilerParams(dimension_semantics=("parallel",),
                                             vmem_limit_bytes=VMEM_LIMIT_BYTES),
        name="front",
    )(x, cos, sin, *consts)


def _attn_kernel(safe_ref, floor_ref, qt_ref, k_ref, vt_ref, kmeta_ref, vtmeta_ref, o_ref, m_scr, acc_scr):
    ki = pl.program_id(2)
    q_cols = qt_ref.shape[1]
    step = (pl.program_id(0) * pl.num_programs(1) + pl.program_id(1)) * pl.num_programs(2) + ki
    safe = safe_ref[step]

    def scores_t(hd, keys_ref):
        return jnp.dot(keys_ref[...], qt_ref[hd * LANES:(hd + 1) * LANES, :], preferred_element_type=F32)

    def values_t(hd, values_ref):
        g = hd // Q_PER_KV
        row = lax.broadcasted_iota(jnp.int32, (ATTN_V_ROWS - HEAD_DIM, values_ref.shape[1]), 0)
        ones = jnp.where(row == 0, 1.0, 0.0).astype(BF16)
        return jnp.concatenate([values_ref[g * HEAD_DIM:(g + 1) * HEAD_DIM, :], ones], axis=0)

    def meta_scores_t(hd):
        real = lax.broadcasted_iota(jnp.int32, (kmeta_ref.shape[0], q_cols), 0) < N_META
        return jnp.where(real, scores_t(hd, kmeta_ref), NEG)

    @pl.when((ki == 0) & (safe == 1))
    def _seed_from_meta_keys_against_floor():
        floor = jnp.full((1, q_cols), floor_ref[pl.program_id(0) * pl.num_programs(1) + pl.program_id(1)], F32)
        scores = [jnp.dot(kmeta_ref[:N_META, :], qt_ref[hd * LANES:(hd + 1) * LANES, :],
                          preferred_element_type=F32) for hd in range(N_Q_HEADS)]
        for hd, s in enumerate(scores):
            p = jnp.exp2(s - floor).astype(BF16)
            acc = jnp.dot(values_t(hd, vtmeta_ref)[:, :N_META], p, preferred_element_type=F32)
            m = jnp.maximum(floor, jnp.max(s, axis=0, keepdims=True))
            acc_scr[hd] = jnp.exp2(floor - m) * acc
            m_scr[hd] = jnp.broadcast_to(m, m_scr.shape[1:])

    @pl.when((ki == 0) & (safe != 1))
    def _seed_from_meta_keys():
        for hd in range(N_Q_HEADS):
            s = meta_scores_t(hd)
            m = jnp.max(s, axis=0, keepdims=True)
            p = jnp.exp2(s - m).astype(BF16)
            acc_scr[hd] = jnp.dot(values_t(hd, vtmeta_ref), p, preferred_element_type=F32)
            m_scr[hd] = jnp.broadcast_to(m, m_scr.shape[1:])

    @pl.when(safe == 1)
    def _block_against_previous_maximum():
        s_next = scores_t(0, k_ref)
        for hd in range(N_Q_HEADS):
            m_old = m_scr[hd][:1]
            s = s_next
            if hd + 1 < N_Q_HEADS:
                s_next = scores_t(hd + 1, k_ref)
            p = jnp.exp2(s - m_old).astype(BF16)
            pv = jnp.dot(values_t(hd, vt_ref), p, preferred_element_type=F32)
            m_new = jnp.maximum(m_old, jnp.max(s, axis=0, keepdims=True))
            acc_scr[hd] = jnp.exp2(m_old - m_new) * (acc_scr[hd] + pv)
            m_scr[hd] = jnp.broadcast_to(m_new, m_scr.shape[1:])

    @pl.when(safe != 1)
    def _block_against_own_maximum():
        for hd in range(N_Q_HEADS):
            m_old = m_scr[hd][:1]
            s = scores_t(hd, k_ref)
            m_new = jnp.maximum(m_old, jnp.max(s, axis=0, keepdims=True))
            p = jnp.exp2(s - m_new).astype(BF16)
            pv = jnp.dot(values_t(hd, vt_ref), p, preferred_element_type=F32)
            acc_scr[hd] = jnp.exp2(m_old - m_new) * acc_scr[hd] + pv
            m_scr[hd] = jnp.broadcast_to(m_new, m_scr.shape[1:])

    @pl.when(ki == pl.num_programs(2) - 1)
    def _normalise():
        def normalised(hd):
            acc = acc_scr[hd]
            return acc[:HEAD_DIM] / acc[HEAD_DIM:HEAD_DIM + 1]

        for pair in range(Q_PER_KV):
            both_t = jnp.concatenate([normalised(pair), normalised(Q_PER_KV + pair)], axis=0)
            o_ref[:, pair * LANES:(pair + 1) * LANES] = both_t.T.astype(BF16)


def _attn_safe_blocks(norms, norms_meta, batch, nq, nk):
    def block_max(sq, blocks):
        return jnp.sqrt(jnp.max(sq.reshape(blocks, -1), axis=1))

    q_norm = block_max(norms[:, 0, 0], batch * nq).reshape(batch, nq, 1)
    k_norm = block_max(norms[:, 1, 0], batch * nk).reshape(batch, 1, nk)
    k_meta_norm = jnp.sqrt(jnp.max(norms_meta[:, 1, 0]))
    gap = q_norm * (jnp.maximum(k_norm, k_meta_norm) + k_meta_norm)
    safe = (gap <= ATTN_SAFE_LOG2).astype(jnp.int32).reshape(-1)
    floor = (-q_norm * k_meta_norm).reshape(-1)
    return safe, floor


def _attn_call(qt, k, vt, k_meta, vt_meta, safe, floor, batch, q_rows, k_rows):
    tokens = k.shape[0]
    n_tok = tokens // batch
    nq = n_tok // q_rows
    nk = n_tok // k_rows
    grid_spec = pltpu.PrefetchScalarGridSpec(
        num_scalar_prefetch=2,
        grid=(batch, nq, nk),
        in_specs=[
            pl.BlockSpec((Q_PAD, q_rows), lambda b, i, j, *_: (0, b * nq + i)),
            pl.BlockSpec((k_rows, KV_WIDTH), lambda b, i, j, *_: (b * nk + j, 0)),
            pl.BlockSpec((KV_WIDTH, k_rows), lambda b, i, j, *_: (0, b * nk + j)),
            pl.BlockSpec(k_meta.shape, lambda b, i, j, *_: (0, 0)),
            pl.BlockSpec(vt_meta.shape, lambda b, i, j, *_: (0, 0)),
        ],
        out_specs=pl.BlockSpec((q_rows, ATTN_WIDTH), lambda b, i, j, *_: (b * nq + i, 0)),
        scratch_shapes=[pltpu.VMEM((N_Q_HEADS, 8, q_rows), F32),
                        pltpu.VMEM((N_Q_HEADS, ATTN_V_ROWS, q_rows), F32)],
    )
    return pl.pallas_call(
        _attn_kernel,
        grid_spec=grid_spec,
        out_shape=jax.ShapeDtypeStruct((tokens, ATTN_WIDTH), BF16),
        compiler_params=pltpu.CompilerParams(dimension_semantics=("parallel", "parallel", "arbitrary"),
                                             vmem_limit_bytes=VMEM_LIMIT_BYTES),
        name="attn",
    )(safe, floor, qt, k, vt, k_meta, vt_meta)


def _split_hi_lo(x):
    hi = x.astype(BF16)
    lo = (x - hi.astype(F32)).astype(BF16)
    return hi, lo


def _gate_cumsum(gates, causal):
    t = gates.shape[0]
    row = lax.broadcasted_iota(jnp.int32, (t, t), 0)
    col = lax.broadcasted_iota(jnp.int32, (t, t), 1)
    mask = (col <= row) if causal else (col >= row)
    tri = jnp.where(mask, 1.0, 0.0).astype(BF16)
    hi, lo = _split_hi_lo(gates)
    sums = jnp.dot(tri, hi, preferred_element_type=F32) + jnp.dot(tri, lo, preferred_element_type=F32)
    return sums, mask


def _values_t_with_ones(vt):
    row = lax.broadcasted_iota(jnp.int32, (STATE_ROWS - M_HEAD_DIM, vt.shape[1]), 0)
    return jnp.concatenate([vt, jnp.where(row == 0, 1.0, 0.0).astype(vt.dtype)], axis=0)


def _state_update(state_t, m_in, k, vaug_t, li_row, b_row, b_tot):
    w_end = b_tot - b_row + li_row
    m_new = jnp.maximum(b_tot + m_in, jnp.max(w_end, axis=1, keepdims=True))
    vw = (vaug_t.astype(F32) * jnp.exp(w_end - m_new)).astype(BF16)
    return jnp.exp(b_tot + m_in - m_new) * state_t + jnp.dot(vw, k, preferred_element_type=F32), m_new


def _mlstm_kernel(qtf_ref, vtf_ref, kf_ref, gf_ref, qtb_ref, vtb_ref, kb_ref, gb_ref, kmeta_ref, vtmeta_ref,
                  gmeta_ref, hf_ref, hb_ref, state_scr, m_scr):
    c = pl.program_id(1)

    @pl.when(c == 0)
    def _reset_and_absorb_meta():
        state_scr[...] = jnp.zeros_like(state_scr)
        m_scr[...] = jnp.zeros_like(m_scr)
        gates = gmeta_ref[...]
        row = lax.broadcasted_iota(jnp.int32, gates.shape, 0)
        lane = lax.broadcasted_iota(jnp.int32, gates.shape, 1)
        gates = jnp.where(row < N_META, gates, jnp.where(lane < 2 * M_HEADS, NEG, 0.0))
        sums, _ = _gate_cumsum(gates, causal=True)
        gates_t = gates.T
        sums_t = sums.T
        last = gates.shape[0] - 1
        for hd in range(M_HEADS):
            f_lane = 2 * M_HEADS + hd
            sl = slice(hd * M_HEAD_DIM, (hd + 1) * M_HEAD_DIM)
            vt = vtmeta_ref[M_WIDTH + hd * M_HEAD_DIM:M_WIDTH + (hd + 1) * M_HEAD_DIM, :]
            new_state, m_new = _state_update(
                state_scr[hd], m_scr[hd][:1, :1], kmeta_ref[:, sl], _values_t_with_ones(vt),
                gates_t[hd:hd + 1, :], sums_t[f_lane:f_lane + 1, :], sums[last:last + 1, f_lane:f_lane + 1])
            state_scr[hd] = new_state
            m_scr[hd] = jnp.broadcast_to(m_new, m_scr.shape[1:])

    t = MLSTM_CHUNK if gf_ref.shape[0] % MLSTM_CHUNK == 0 else gf_ref.shape[0]
    n_sub = gf_ref.shape[0] // t
    src = lax.broadcasted_iota(jnp.int32, (t, t), 0)
    tgt = lax.broadcasted_iota(jnp.int32, (t, t), 1)
    links = []
    m_carry = {chain: m_scr[chain][:1, :1] for chain in range(2 * M_HEADS)}
    for order in range(n_sub):
        for causal, qt_ref, vt_ref, k_ref, g_ref, o_ref in ((True, qtf_ref, vtf_ref, kf_ref, gf_ref, hf_ref),
                                                            (False, qtb_ref, vtb_ref, kb_ref, gb_ref, hb_ref)):
            j = order if causal else n_sub - 1 - order
            ts = slice(j * t, (j + 1) * t)
            gates = g_ref[ts, :]
            sums, _ = _gate_cumsum(gates, causal)
            gates_t = gates.T
            sums_t = sums.T
            feeds = (src <= tgt) if causal else (src >= tgt)
            last = t - 1 if causal else 0
            for hd in range(M_HEADS):
                chain = hd if causal else M_HEADS + hd
                i_lane, f_lane = chain, 2 * M_HEADS + chain
                li_row, b_row = gates_t[i_lane:i_lane + 1, :], sums_t[f_lane:f_lane + 1, :]
                b_tot = sums[last:last + 1, f_lane:f_lane + 1]
                w_end = b_tot - b_row + li_row
                m_in = m_carry[chain]
                m_new = jnp.maximum(b_tot + m_in, jnp.max(w_end, axis=1, keepdims=True))
                m_carry[chain] = m_new
                links.append(dict(
                    chain=chain, first=order == 0, feeds=feeds, o_ref=o_ref, ts=ts,
                    sl=slice(hd * M_HEAD_DIM, (hd + 1) * M_HEAD_DIM), qt_ref=qt_ref, vt_ref=vt_ref, k_ref=k_ref,
                    r_col=gates[:, i_lane:i_lane + 1] - sums[:, f_lane:f_lane + 1],
                    b_row=b_row, b_tot=b_tot, w_end=w_end, m_in=m_in, m_new=m_new))

    for ln in links:
        k = ln["k_ref"][ln["ts"], ln["sl"]]
        qt = ln["qt_ref"][ln["sl"], ln["ts"]]
        if ln["first"]:
            both = jnp.dot(jnp.concatenate([k, state_scr[ln["chain"]].astype(BF16)], axis=0), qt,
                           preferred_element_type=F32)
            ln["kq"], ln["inter"] = both[:t], both[t:]
        else:
            ln["kq"] = jnp.dot(k, qt, preferred_element_type=F32)
    for ln in links:
        ln["r"] = jnp.where(ln["feeds"], ln["r_col"], NEG)
        ln["g"] = jnp.maximum(jnp.max(ln["r"], axis=0, keepdims=True), ln["m_in"])
    for ln in links:
        ln["s"] = (ln["kq"] * jnp.exp(ln["r"] - ln["g"])).astype(BF16)
        ln["vaug"] = _values_t_with_ones(ln["vt_ref"][ln["sl"], ln["ts"]])
        ln["vw"] = (ln["vaug"].astype(F32) * jnp.exp(ln["w_end"] - ln["m_new"])).astype(BF16)
    for ln in links:
        ln["tot"] = jnp.dot(ln["vaug"], ln["s"], preferred_element_type=F32)
        ln["local"] = jnp.dot(ln["vw"], ln["k_ref"][ln["ts"], ln["sl"]], preferred_element_type=F32)
    state = {chain: state_scr[chain] for chain in range(2 * M_HEADS)}
    for ln in links:
        chain = ln["chain"]
        if not ln["first"]:
            ln["inter"] = jnp.dot(state[chain].astype(BF16), ln["qt_ref"][ln["sl"], ln["ts"]],
                                  preferred_element_type=F32)
        tot = ln["tot"] + jnp.exp(ln["m_in"] - ln["g"]) * ln["inter"]
        den = jnp.maximum(jnp.abs(tot[M_HEAD_DIM:M_HEAD_DIM + 1]), jnp.exp(-(ln["b_row"] + ln["g"])))
        ln["o_ref"][ln["sl"], ln["ts"]] = tot[:M_HEAD_DIM] / den
        state[chain] = jnp.exp(ln["b_tot"] + ln["m_in"] - ln["m_new"]) * state[chain] + ln["local"]
    for chain in range(2 * M_HEADS):
        state_scr[chain] = state[chain]
        m_scr[chain] = jnp.broadcast_to(m_carry[chain], m_scr.shape[1:])


def _mlstm_call(mt, m, gates, mt_meta, m_meta, gates_meta, batch, chunk):
    tokens = m.shape[0]
    nc = tokens // batch // chunk

    def chunk_of(direction):
        return (lambda b, c: b * nc + c) if direction == "fwd" else (lambda b, c: b * nc + nc - 1 - c)

    def specs(direction):
        at = chunk_of(direction)
        return [pl.BlockSpec((M_WIDTH, chunk), lambda b, c: (0, at(b, c))),
                pl.BlockSpec((M_WIDTH, chunk), lambda b, c: (1, at(b, c))),
                pl.BlockSpec((chunk, M_WIDTH), lambda b, c: (at(b, c), 0)),
                pl.BlockSpec((chunk, LANES), lambda b, c: (at(b, c), 0))]

    def out_spec(direction):
        at = chunk_of(direction)
        return pl.BlockSpec((M_WIDTH, chunk), lambda b, c: (0, at(b, c)))

    return pl.pallas_call(
        _mlstm_kernel,
        grid=(batch, nc),
        in_specs=specs("fwd") + specs("bwd") + [pl.BlockSpec(m_meta.shape, lambda b, c: (0, 0)),
                                                pl.BlockSpec(mt_meta.shape, lambda b, c: (0, 0)),
                                                pl.BlockSpec(gates_meta.shape, lambda b, c: (0, 0))],
        out_specs=[out_spec("fwd"), out_spec("bwd")],
        out_shape=[jax.ShapeDtypeStruct((M_WIDTH, tokens), F32)] * 2,
        scratch_shapes=[pltpu.VMEM((2 * M_HEADS, STATE_ROWS, M_HEAD_DIM), F32),
                        pltpu.VMEM((2 * M_HEADS, 8, LANES), F32)],
        compiler_params=pltpu.CompilerParams(dimension_semantics=("parallel", "arbitrary"),
                                             vmem_limit_bytes=VMEM_LIMIT_BYTES),
        name="mlstm",
    )(mt, mt, m, gates, mt, mt, m, gates, m_meta, mt_meta, gates_meta)


def _back_kernel(h_ref, ao_ref, hf_ref, hb_ref, mo_ref, merge_ref, mgain_ref, wab_ref, wmb_ref, wout_ref,
                 g2_ref, w1_ref, w3_ref, w2_ref, y_ref):
    rows, d_model = h_ref.shape
    for r in range(0, rows, SUB_ROWS):
        rs = slice(r, r + SUB_ROWS)
        a_out = jnp.dot(ao_ref[rs, :], wab_ref[...], preferred_element_type=F32)

        hm = (hf_ref[:, rs] + hb_ref[:, rs]).T
        o_gate = jax.nn.sigmoid(mo_ref[rs, :].astype(F32))
        gated = []
        for hd in range(M_HEADS):
            sl = slice(hd * M_HEAD_DIM, (hd + 1) * M_HEAD_DIM)
            gated.append((_rmsnorm(hm[:, sl], mgain_ref[:, sl]) * o_gate[:, sl]).astype(BF16))
        m_out = jnp.dot(jnp.concatenate(gated, axis=1), wmb_ref[...], preferred_element_type=F32)

        merged = (merge_ref[rs, :d_model].astype(F32) * a_out + merge_ref[rs, d_model:].astype(F32) * m_out)
        h2 = h_ref[rs, :] + jnp.dot(merged.astype(BF16), wout_ref[...], preferred_element_type=F32)
        y_ref[rs, :] = _swiglu_half_residual(h2, g2_ref[...], w1_ref, w3_ref, w2_ref)


def _back_call(h, attn_o, hf, hb, m, merge, wts, rows):
    tokens, d_model = h.shape

    def row_spec(width, col=0):
        return pl.BlockSpec((rows, width), lambda i: (i, col))

    col_spec = pl.BlockSpec((M_WIDTH, rows), lambda i: (0, i))
    consts = (wts["mgain"], wts["wab"], wts["wmb"], wts["wout"], wts["g2"], wts["w1b"], wts["w3b"], wts["w2b"])
    return pl.pallas_call(
        _back_kernel,
        grid=(tokens // rows,),
        in_specs=[row_spec(d_model), row_spec(ATTN_WIDTH), col_spec, col_spec,
                  row_spec(M_WIDTH, 1), row_spec(2 * d_model)] + [_const_spec(c.shape) for c in consts],
        out_specs=row_spec(d_model),
        out_shape=jax.ShapeDtypeStruct((tokens, d_model), F32),
        compiler_params=pltpu.CompilerParams(dimension_semantics=("parallel",),
                                             vmem_limit_bytes=VMEM_LIMIT_BYTES),
        name="back",
    )(h, attn_o, hf, hb, m, merge, *consts)


def _pair_heads(w):
    lead = w.shape[:-1]
    w = w.reshape(*lead, N_KV_HEADS, Q_PER_KV, HEAD_DIM)
    return jnp.swapaxes(w, -3, -2).reshape(*lead, ATTN_WIDTH)


def _prepare_weights(g_ffn1, w1_ffn1, w3_ffn1, w2_ffn1, g_mix, w_in, b_i, b_f, q_gain, k_gain, m_gain,
                     w_attn_br, w_mlstm_br, w_out, g_ffn2, w1_ffn2, w3_ffn2, w2_ffn2):
    d_model = w_in.shape[1]
    attn_w = N_Q_HEADS * HEAD_DIM
    kv_w = N_KV_HEADS * HEAD_DIM
    splits = (attn_w, kv_w, kv_w, M_WIDTH, M_WIDTH, M_WIDTH, M_WIDTH, 2 * M_HEADS, 2 * M_HEADS, d_model, d_model)
    offs = np.cumsum((0,) + splits)
    assert offs[-1] == w_in.shape[2]
    cols = [w_in[0, :, offs[i]:offs[i + 1]] for i in range(len(splits))]
    aq, ak, av, mq, mk, mv, mo, gi, gf, ga, gm = cols
    gate_w = jnp.pad(jnp.concatenate([gi, gf], axis=1), ((0, 0), (0, LANES - 4 * M_HEADS)))
    gate_b = jnp.pad(jnp.concatenate([b_i[0], b_f[0]]), (0, LANES - 4 * M_HEADS))[None]
    row = lambda g: g.astype(F32)[None]
    both_halves = lambda g: jnp.tile(g[0].astype(F32), LANES // HEAD_DIM)[None]
    return {
        "g1": row(g_ffn1[0]), "w1a": w1_ffn1[0].astype(BF16), "w3a": w3_ffn1[0].astype(BF16),
        "w2a": w2_ffn1[0].astype(BF16),
        "gmix": row(g_mix[0]),
        "wq": _pair_heads(aq).astype(BF16), "wkv": jnp.concatenate([ak, av], axis=1).astype(BF16),
        "wm": jnp.concatenate([mq, mk, mv, mo], axis=1).astype(BF16),
        "wg": gate_w.astype(BF16), "gbias": gate_b.astype(F32),
        "wmerge": jnp.concatenate([ga, gm], axis=1).astype(BF16),
        "qgain": both_halves(q_gain), "kgain": both_halves(k_gain),
        "mgain": m_gain[0].astype(F32).reshape(1, M_WIDTH),
        "wab": _pair_heads(w_attn_br[0].T).T.astype(BF16), "wmb": w_mlstm_br[0].astype(BF16), "wout": w_out[0].astype(BF16),
        "g2": row(g_ffn2[0]), "w1b": w1_ffn2[0].astype(BF16), "w3b": w3_ffn2[0].astype(BF16),
        "w2b": w2_ffn2[0].astype(BF16),
    }


def _rope_tables(n_tok):
    t = jnp.arange(n_tok)
    pos = jnp.stack([(t // GRID_W).astype(F32), (t % GRID_W).astype(F32)], axis=-1)
    inv_freq = ROPE_THETA ** (-2.0 * jnp.arange(ROPE_FREQS, dtype=F32) / ROPE_AXIS_DIM)
    ang = pos[:, :, None] * inv_freq
    cos, sin = jnp.cos(ang), jnp.sin(ang)
    cos = jnp.stack([cos, cos], axis=2).reshape(n_tok, HEAD_DIM)
    sin = jnp.stack([-sin, sin], axis=2).reshape(n_tok, HEAD_DIM)
    reps = (1, LANES // HEAD_DIM)
    return jnp.tile(cos, reps), jnp.tile(sin, reps)


def _trunk(x, meta_parts, wts):
    batch, n_tok, d_model = x.shape
    k_meta, v_meta, mt_meta, m_meta, gates_meta, norms_meta = meta_parts
    cos, sin = _rope_tables(n_tok)
    h, q, k, v, mt, m, gates, merge, norms = _front_call(
        x.reshape(batch * n_tok, d_model), cos, sin, wts, FRONT_ROWS)
    q_rows, k_rows, chunk = min(ATTN_Q_ROWS, n_tok), min(ATTN_K_ROWS, n_tok), min(MLSTM_STEP_TOKENS, n_tok)
    assert n_tok % FRONT_ROWS == 0 and n_tok % q_rows == 0 and n_tok % k_rows == 0 and n_tok % chunk == 0
    assert q_rows % FRONT_ROWS == 0 and k_rows % FRONT_ROWS == 0
    safe, floor = _attn_safe_blocks(norms, norms_meta, batch, n_tok // q_rows, n_tok // k_rows)
    attn_o = _attn_call(q, k, v, k_meta, v_meta, safe, floor, batch, q_rows, k_rows)
    hf, hb = _mlstm_call(mt, m, gates, mt_meta, m_meta, gates_meta, batch, chunk)
    y = _back_call(h, attn_o, hf, hb, m, merge, wts, BACK_ROWS)
    return y.reshape(batch, n_tok, d_model)


def kernel(x_prompt, x_sample, meta, g_ffn1, w1_ffn1, w3_ffn1, w2_ffn1, g_mix, w_in, b_i, b_f, q_gain, k_gain,
           m_gain, w_attn_br, w_mlstm_br, w_out, g_ffn2, w1_ffn2, w3_ffn2, w2_ffn2):
    assert w_in.shape[0] == 1, "single-layer trunk: the meta rows' mixer outputs are never consumed"
    assert meta.shape[0] == N_META
    wts = _prepare_weights(g_ffn1, w1_ffn1, w3_ffn1, w2_ffn1, g_mix, w_in, b_i, b_f, q_gain, k_gain, m_gain,
                           w_attn_br, w_mlstm_br, w_out, g_ffn2, w1_ffn2, w3_ffn2, w2_ffn2)
    ones = jnp.ones((META_ROWS, LANES), F32)
    meta_rows = jnp.pad(meta.astype(F32), ((0, META_ROWS - N_META), (0, 0)))
    _, _, k_meta, v_meta, mt_meta, m_meta, gates_meta, _, norms_meta = _front_call(
        meta_rows, ones, jnp.zeros_like(ones), wts, META_ROWS)
    meta_parts = (k_meta, v_meta, mt_meta, m_meta, gates_meta, norms_meta)
    return (_trunk(x_prompt, meta_parts, wts), _trunk(x_sample, meta_parts, wts))
```

```python
import jax
import jax.numpy as jnp
import numpy as np
from jax import lax
from jax.experimental import pallas as pl
from jax.experimental.pallas import tpu as pltpu

F32 = jnp.float32
BF16 = jnp.bfloat16

N_META = 16
GRID_W = 64
EPS = 1e-6
NEG = -1e30
N_Q_HEADS = 8
N_KV_HEADS = 2
Q_PER_KV = N_Q_HEADS // N_KV_HEADS
HEAD_DIM = 64
ROPE_AXIS_DIM = HEAD_DIM // 2
ROPE_FREQS = ROPE_AXIS_DIM // 2
ROPE_THETA = 10000.0
M_HEADS = 4
M_HEAD_DIM = 128
M_WIDTH = M_HEADS * M_HEAD_DIM

LANES = 128
BF16_ROWS = 16
VMEM_LIMIT_BYTES = 60 * 1024 * 1024

SUB_ROWS = 256
FRONT_ROWS = 2 * SUB_ROWS
BACK_ROWS = 2 * SUB_ROWS
ATTN_Q_ROWS = 1024
ATTN_K_ROWS = 2048
MLSTM_CHUNK = 256
MLSTM_STEP_TOKENS = 4 * MLSTM_CHUNK

Q_PAD = N_Q_HEADS * LANES
ATTN_WIDTH = N_Q_HEADS * HEAD_DIM
KV_WIDTH = N_KV_HEADS * HEAD_DIM
assert KV_WIDTH == LANES and HEAD_DIM * 2 == LANES
Q_SCALE = HEAD_DIM ** -0.5 * 1.4426950408889634
ATTN_SAFE_LOG2 = 90.0
ATTN_V_ROWS = HEAD_DIM + BF16_ROWS
META_ROWS = LANES
STATE_ROWS = M_HEAD_DIM + BF16_ROWS


def _const_spec(shape):
    zeros = (0,) * len(shape)
    return pl.BlockSpec(shape, lambda *_: zeros, pipeline_mode=pl.Buffered(1))


def _rmsnorm(x, g):
    return x * lax.rsqrt(jnp.mean(x * x, axis=-1, keepdims=True) + EPS) * g


def _swiglu_half_residual(x, g, w1_ref, w3_ref, w2_ref):
    u = _rmsnorm(x, g).astype(BF16)
    a = jnp.dot(u, w1_ref[...], preferred_element_type=F32)
    b = jnp.dot(u, w3_ref[...], preferred_element_type=F32)
    act = (a * jax.nn.sigmoid(a) * b).astype(BF16)
    return x + 0.5 * jnp.dot(act, w2_ref[...], preferred_element_type=F32)


def _log_sigmoid(x):
    return jnp.minimum(x, 0.0) - jnp.log1p(jnp.exp(-jnp.abs(x)))


def _half_sums(x, low_half):
    low = jnp.sum(jnp.where(low_half, x, 0.0), axis=-1, keepdims=True)
    high = jnp.sum(jnp.where(low_half, 0.0, x), axis=-1, keepdims=True)
    return low, high


def _head_pair_norm_rope(x, gain, cos, sin_signed, low_half, first_half):
    ms_low, ms_high = _half_sums(x * x, low_half)
    inv = jnp.where(low_half, lax.rsqrt(ms_low * (1.0 / HEAD_DIM) + EPS), lax.rsqrt(ms_high * (1.0 / HEAD_DIM) + EPS))
    y = x * inv * gain
    partner = jnp.where(first_half, pltpu.roll(y, LANES - ROPE_FREQS, 1), pltpu.roll(y, ROPE_FREQS, 1))
    return y * cos + partner * sin_signed


def _front_kernel(x_ref, cos_ref, sin_ref, g1_ref, w1_ref, w3_ref, w2_ref, gmix_ref, wq_ref, wkv_ref,
                  wm_ref, wg_ref, wmerge_ref, qgain_ref, kgain_ref, gbias_ref,
                  h_ref, q_ref, k_ref, v_ref, mt_ref, m_ref, gate_ref, merge_ref, norm_ref):
    sub = min(x_ref.shape[0], SUB_ROWS)
    lane = lax.broadcasted_iota(jnp.int32, (sub, LANES), 1)
    low_half = lane < HEAD_DIM
    first_half = (lane % ROPE_AXIS_DIM) < ROPE_FREQS

    def sq_norms(y_bf16):
        y = y_bf16.astype(F32)
        return jnp.maximum(*_half_sums(y * y, low_half))

    q_sq = jnp.zeros((sub, 1), F32)
    k_sq = jnp.zeros((sub, 1), F32)
    for r in range(0, x_ref.shape[0], sub):
        rs = slice(r, r + sub)
        h = _swiglu_half_residual(x_ref[rs, :], g1_ref[...], w1_ref, w3_ref, w2_ref)
        h_ref[rs, :] = h
        u = _rmsnorm(h, gmix_ref[...]).astype(BF16)
        cos = cos_ref[rs, :]
        sin = sin_ref[rs, :]

        zq = jnp.dot(u, wq_ref[...], preferred_element_type=F32)
        for pair in range(Q_PER_KV):
            y = _head_pair_norm_rope(zq[:, pair * LANES:(pair + 1) * LANES], qgain_ref[...], cos, sin,
                                     low_half, first_half) * Q_SCALE
            q_sq = jnp.maximum(q_sq, sq_norms(y.astype(BF16)))
            for group, keep in enumerate((low_half, ~low_half)):
                hd = group * Q_PER_KV + pair
                q_ref[hd * LANES:(hd + 1) * LANES, rs] = jnp.where(keep, y, 0.0).T.astype(BF16)

        zkv = jnp.dot(u, wkv_ref[...], preferred_element_type=F32)
        k = _head_pair_norm_rope(zkv[:, :LANES], kgain_ref[...], cos, sin, low_half, first_half).astype(BF16)
        k_ref[rs, :] = k
        k_sq = jnp.maximum(k_sq, sq_norms(k))
        v_ref[:, rs] = zkv[:, LANES:].T.astype(BF16)

        zm = jnp.dot(u, wm_ref[...], preferred_element_type=F32)
        mt_ref[:M_WIDTH, rs] = zm[:, :M_WIDTH].T.astype(BF16)
        mt_ref[M_WIDTH:, rs] = zm[:, 2 * M_WIDTH:3 * M_WIDTH].T.astype(BF16)
        m_ref[rs, :M_WIDTH] = (zm[:, M_WIDTH:2 * M_WIDTH] * (M_HEAD_DIM ** -0.5)).astype(BF16)
        m_ref[rs, M_WIDTH:] = zm[:, 3 * M_WIDTH:].astype(BF16)

        zg = jnp.dot(u, wg_ref[...], preferred_element_type=F32) + gbias_ref[...]
        gate_ref[rs, :] = jnp.where(lane < 2 * M_HEADS, zg, _log_sigmoid(zg))

        zmerge = jnp.dot(u, wmerge_ref[...], preferred_element_type=F32)
        merge_ref[rs, :] = jax.nn.sigmoid(zmerge).astype(BF16)

    tile_row = lax.broadcasted_iota(jnp.int32, norm_ref.shape, 0)
    norm_ref[...] = jnp.where(tile_row == 0, jnp.max(q_sq, axis=0, keepdims=True),
                              jnp.max(k_sq, axis=0, keepdims=True))


def _front_call(x, cos, sin, wts, rows):
    tokens, d_model = x.shape
    steps = tokens // rows
    table_steps = cos.shape[0] // rows

    def row_spec(width):
        return pl.BlockSpec((rows, width), lambda i: (i, 0))

    table_spec = pl.BlockSpec((rows, LANES), lambda i: (i % table_steps, 0))
    consts = (wts["g1"], wts["w1a"], wts["w3a"], wts["w2a"], wts["gmix"], wts["wq"], wts["wkv"],
              wts["wm"], wts["wg"], wts["wmerge"], wts["qgain"], wts["kgain"], wts["gbias"])

    def col_spec(height):
        return pl.BlockSpec((height, rows), lambda i: (0, i))

    out_specs = [row_spec(d_model), col_spec(Q_PAD), row_spec(KV_WIDTH), col_spec(KV_WIDTH), col_spec(2 * M_WIDTH),
                 row_spec(2 * M_WIDTH), row_spec(LANES), row_spec(2 * d_model),
                 pl.BlockSpec((None, 8, LANES), lambda i: (i, 0, 0))]
    out_shape = [jax.ShapeDtypeStruct((tokens, d_model), F32), jax.ShapeDtypeStruct((Q_PAD, tokens), BF16),
                 jax.ShapeDtypeStruct((tokens, KV_WIDTH), BF16), jax.ShapeDtypeStruct((KV_WIDTH, tokens), BF16),
                 jax.ShapeDtypeStruct((2 * M_WIDTH, tokens), BF16),
                 jax.ShapeDtypeStruct((tokens, 2 * M_WIDTH), BF16), jax.ShapeDtypeStruct((tokens, LANES), F32),
                 jax.ShapeDtypeStruct((tokens, 2 * d_model), BF16), jax.ShapeDtypeStruct((steps, 8, LANES), F32)]
    return pl.pallas_call(
        _front_kernel,
        grid=(steps,),
        in_specs=[row_spec(d_model), table_spec, table_spec] + [_const_spec(c.shape) for c in consts],
        out_specs=out_specs,
        out_shape=out_shape,
        compiler_params=pltpu.CompilerParams(dimension_semantics=("parallel",),
                                             vmem_limit_bytes=VMEM_LIMIT_BYTES),
        name="front",
    )(x, cos, sin, *consts)


def _attn_kernel(safe_ref, floor_ref, qt_ref, k_ref, vt_ref, kmeta_ref, vtmeta_ref, o_ref, m_scr, acc_scr):
    ki = pl.program_id(2)
    q_cols = qt_ref.shape[1]
    step = (pl.program_id(0) * pl.num_programs(1) + pl.program_id(1)) * pl.num_programs(2) + ki
    safe = safe_ref[step]

    def scores_t(hd, keys_ref):
        return jnp.dot(keys_ref[...], qt_ref[hd * LANES:(hd + 1) * LANES, :], preferred_element_type=F32)

    def values_t(hd, values_ref):
        g = hd // Q_PER_KV
        row = lax.broadcasted_iota(jnp.int32, (ATTN_V_ROWS - HEAD_DIM, values_ref.shape[1]), 0)
        ones = jnp.where(row == 0, 1.0, 0.0).astype(BF16)
        return jnp.concatenate([values_ref[g * HEAD_DIM:(g + 1) * HEAD_DIM, :], ones], axis=0)

    def meta_scores_t(hd):
        real = lax.broadcasted_iota(jnp.int32, (kmeta_ref.shape[0], q_cols), 0) < N_META
        return jnp.where(real, scores_t(hd, kmeta_ref), NEG)

    @pl.when((ki == 0) & (safe == 1))
    def _seed_from_meta_keys_against_floor():
        floor = jnp.full((1, q_cols), floor_ref[pl.program_id(0) * pl.num_programs(1) + pl.program_id(1)], F32)
        scores = [jnp.dot(kmeta_ref[:N_META, :], qt_ref[hd * LANES:(hd + 1) * LANES, :],
                          preferred_element_type=F32) for hd in range(N_Q_HEADS)]
        for hd, s in enumerate(scores):
            p = jnp.exp2(s - floor).astype(BF16)
            acc = jnp.dot(values_t(hd, vtmeta_ref)[:, :N_META], p, preferred_element_type=F32)
            m = jnp.maximum(floor, jnp.max(s, axis=0, keepdims=True))
            acc_scr[hd] = jnp.exp2(floor - m) * acc
            m_scr[hd] = jnp.broadcast_to(m, m_scr.shape[1:])

    @pl.when((ki == 0) & (safe != 1))
    def _seed_from_meta_keys():
        for hd in range(N_Q_HEADS):
            s = meta_scores_t(hd)
            m = jnp.max(s, axis=0, keepdims=True)
            p = jnp.exp2(s - m).astype(BF16)
            acc_scr[hd] = jnp.dot(values_t(hd, vtmeta_ref), p, preferred_element_type=F32)
            m_scr[hd] = jnp.broadcast_to(m, m_scr.shape[1:])

    @pl.when(safe == 1)
    def _block_against_previous_maximum():
        s_next = scores_t(0, k_ref)
        for hd in range(N_Q_HEADS):
            m_old = m_scr[hd][:1]
            s = s_next
            if hd + 1 < N_Q_HEADS:
                s_next = scores_t(hd + 1, k_ref)
            p = jnp.exp2(s - m_old).astype(BF16)
            pv = jnp.dot(values_t(hd, vt_ref), p, preferred_element_type=F32)
            m_new = jnp.maximum(m_old, jnp.max(s, axis=0, keepdims=True))
            acc_scr[hd] = jnp.exp2(m_old - m_new) * (acc_scr[hd] + pv)
            m_scr[hd] = jnp.broadcast_to(m_new, m_scr.shape[1:])

    @pl.when(safe != 1)
    def _block_against_own_maximum():
        for hd in range(N_Q_HEADS):
            m_old = m_scr[hd][:1]
            s = scores_t(hd, k_ref)
            m_new = jnp.maximum(m_old, jnp.max(s, axis=0, keepdims=True))
            p = jnp.exp2(s - m_new).astype(BF16)
            pv = jnp.dot(values_t(hd, vt_ref), p, preferred_element_type=F32)
            acc_scr[hd] = jnp.exp2(m_old - m_new) * acc_scr[hd] + pv
            m_scr[hd] = jnp.broadcast_to(m_new, m_scr.shape[1:])

    @pl.when(ki == pl.num_programs(2) - 1)
    def _normalise():
        def normalised(hd):
            acc = acc_scr[hd]
            return acc[:HEAD_DIM] / acc[HEAD_DIM:HEAD_DIM + 1]

        for pair in range(Q_PER_KV):
            both_t = jnp.concatenate([normalised(pair), normalised(Q_PER_KV + pair)], axis=0)
            o_ref[:, pair * LANES:(pair + 1) * LANES] = both_t.T.astype(BF16)


def _attn_safe_blocks(norms, norms_meta, batch, nq, nk):
    def block_max(sq, blocks):
        return jnp.sqrt(jnp.max(sq.reshape(blocks, -1), axis=1))

    q_norm = block_max(norms[:, 0, 0], batch * nq).reshape(batch, nq, 1)
    k_norm = block_max(norms[:, 1, 0], batch * nk).reshape(batch, 1, nk)
    k_meta_norm = jnp.sqrt(jnp.max(norms_meta[:, 1, 0]))
    gap = q_norm * (jnp.maximum(k_norm, k_meta_norm) + k_meta_norm)
    safe = (gap <= ATTN_SAFE_LOG2).astype(jnp.int32).reshape(-1)
    floor = (-q_norm * k_meta_norm).reshape(-1)
    return safe, floor


def _attn_call(qt, k, vt, k_meta, vt_meta, safe, floor, batch, q_rows, k_rows):
    tokens = k.shape[0]
    n_tok = tokens // batch
    nq = n_tok // q_rows
    nk = n_tok // k_rows
    grid_spec = pltpu.PrefetchScalarGridSpec(
        num_scalar_prefetch=2,
        grid=(batch, nq, nk),
        in_specs=[
            pl.BlockSpec((Q_PAD, q_rows), lambda b, i, j, *_: (0, b * nq + i)),
            pl.BlockSpec((k_rows, KV_WIDTH), lambda b, i, j, *_: (b * nk + j, 0)),
            pl.BlockSpec((KV_WIDTH, k_rows), lambda b, i, j, *_: (0, b * nk + j)),
            pl.BlockSpec(k_meta.shape, lambda b, i, j, *_: (0, 0)),
            pl.BlockSpec(vt_meta.shape, lambda b, i, j, *_: (0, 0)),
        ],
        out_specs=pl.BlockSpec((q_rows, ATTN_WIDTH), lambda b, i, j, *_: (b * nq + i, 0)),
        scratch_shapes=[pltpu.VMEM((N_Q_HEADS, 8, q_rows), F32),
                        pltpu.VMEM((N_Q_HEADS, ATTN_V_ROWS, q_rows), F32)],
    )
    return pl.pallas_call(
        _attn_kernel,
        grid_spec=grid_spec,
        out_shape=jax.ShapeDtypeStruct((tokens, ATTN_WIDTH), BF16),
        compiler_params=pltpu.CompilerParams(dimension_semantics=("parallel", "parallel", "arbitrary"),
                                             vmem_limit_bytes=VMEM_LIMIT_BYTES),
        name="attn",
    )(safe, floor, qt, k, vt, k_meta, vt_meta)


def _split_hi_lo(x):
    hi = x.astype(BF16)
    lo = (x - hi.astype(F32)).astype(BF16)
    return hi, lo


def _gate_cumsum(gates, causal):
    t = gates.shape[0]
    row = lax.broadcasted_iota(jnp.int32, (t, t), 0)
    col = lax.broadcasted_iota(jnp.int32, (t, t), 1)
    mask = (col <= row) if causal else (col >= row)
    tri = jnp.where(mask, 1.0, 0.0).astype(BF16)
    hi, lo = _split_hi_lo(gates)
    sums = jnp.dot(tri, hi, preferred_element_type=F32) + jnp.dot(tri, lo, preferred_element_type=F32)
    return sums, mask


def _values_t_with_ones(vt):
    row = lax.broadcasted_iota(jnp.int32, (STATE_ROWS - M_HEAD_DIM, vt.shape[1]), 0)
    return jnp.concatenate([vt, jnp.where(row == 0, 1.0, 0.0).astype(vt.dtype)], axis=0)


def _state_update(state_t, m_in, k, vaug_t, li_row, b_row, b_tot):
    w_end = b_tot - b_row + li_row
    m_new = jnp.maximum(b_tot + m_in, jnp.max(w_end, axis=1, keepdims=True))
    vw = (vaug_t.astype(F32) * jnp.exp(w_end - m_new)).astype(BF16)
    return jnp.exp(b_tot + m_in - m_new) * state_t + jnp.dot(vw, k, preferred_element_type=F32), m_new


def _mlstm_kernel(qtf_ref, vtf_ref, kf_ref, gf_ref, qtb_ref, vtb_ref, kb_ref, gb_ref, kmeta_ref, vtmeta_ref,
                  gmeta_ref, hf_ref, hb_ref, state_scr, m_scr):
    c = pl.program_id(1)

    @pl.when(c == 0)
    def _reset_and_absorb_meta():
        state_scr[...] = jnp.zeros_like(state_scr)
        m_scr[...] = jnp.zeros_like(m_scr)
        gates = gmeta_ref[...]
        row = lax.broadcasted_iota(jnp.int32, gates.shape, 0)
        lane = lax.broadcasted_iota(jnp.int32, gates.shape, 1)
        gates = jnp.where(row < N_META, gates, jnp.where(lane < 2 * M_HEADS, NEG, 0.0))
        sums, _ = _gate_cumsum(gates, causal=True)
        gates_t = gates.T
        sums_t = sums.T
        last = gates.shape[0] - 1
        for hd in range(M_HEADS):
            f_lane = 2 * M_HEADS + hd
            sl = slice(hd * M_HEAD_DIM, (hd + 1) * M_HEAD_DIM)
            vt = vtmeta_ref[M_WIDTH + hd * M_HEAD_DIM:M_WIDTH + (hd + 1) * M_HEAD_DIM, :]
            new_state, m_new = _state_update(
                state_scr[hd], m_scr[hd][:1, :1], kmeta_ref[:, sl], _values_t_with_ones(vt),
                gates_t[hd:hd + 1, :], sums_t[f_lane:f_lane + 1, :], sums[last:last + 1, f_lane:f_lane + 1])
            state_scr[hd] = new_state
            m_scr[hd] = jnp.broadcast_to(m_new, m_scr.shape[1:])

    t = MLSTM_CHUNK if gf_ref.shape[0] % MLSTM_CHUNK == 0 else gf_ref.shape[0]
    n_sub = gf_ref.shape[0] // t
    src = lax.broadcasted_iota(jnp.int32, (t, t), 0)
    tgt = lax.broadcasted_iota(jnp.int32, (t, t), 1)
    links = []
    m_carry = {chain: m_scr[chain][:1, :1] for chain in range(2 * M_HEADS)}
    for order in range(n_sub):
        for causal, qt_ref, vt_ref, k_ref, g_ref, o_ref in ((True, qtf_ref, vtf_ref, kf_ref, gf_ref, hf_ref),
                                                            (False, qtb_ref, vtb_ref, kb_ref, gb_ref, hb_ref)):
            j = order if causal else n_sub - 1 - order
            ts = slice(j * t, (j + 1) * t)
            gates = g_ref[ts, :]
            sums, _ = _gate_cumsum(gates, causal)
            gates_t = gates.T
            sums_t = sums.T
            feeds = (src <= tgt) if causal else (src >= tgt)
            last = t - 1 if causal else 0
            for hd in range(M_HEADS):
                chain = hd if causal else M_HEADS + hd
                i_lane, f_lane = chain, 2 * M_HEADS + chain
                li_row, b_row = gates_t[i_lane:i_lane + 1, :], sums_t[f_lane:f_lane + 1, :]
                b_tot = sums[last:last + 1, f_lane:f_lane + 1]
                w_end = b_tot - b_row + li_row
                m_in = m_carry[chain]
                m_new = jnp.maximum(b_tot + m_in, jnp.max(w_end, axis=1, keepdims=True))
                m_carry[chain] = m_new
                links.append(dict(
                    chain=chain, first=order == 0, feeds=feeds, o_ref=o_ref, ts=ts,
                    sl=slice(hd * M_HEAD_DIM, (hd + 1) * M_HEAD_DIM), qt_ref=qt_ref, vt_ref=vt_ref, k_ref=k_ref,
                    r_col=gates[:, i_lane:i_lane + 1] - sums[:, f_lane:f_lane + 1],
                    b_row=b_row, b_tot=b_tot, w_end=w_end, m_in=m_in, m_new=m_new))

    for ln in links:
        k = ln["k_ref"][ln["ts"], ln["sl"]]
        qt = ln["qt_ref"][ln["sl"], ln["ts"]]
        if ln["first"]:
            both = jnp.dot(jnp.concatenate([k, state_scr[ln["chain"]].astype(BF16)], axis=0), qt,
                           preferred_element_type=F32)
            ln["kq"], ln["inter"] = both[:t], both[t:]
        else:
            ln["kq"] = jnp.dot(k, qt, preferred_element_type=F32)
    for ln in links:
        ln["r"] = jnp.where(ln["feeds"], ln["r_col"], NEG)
        ln["g"] = jnp.maximum(jnp.max(ln["r"], axis=0, keepdims=True), ln["m_in"])
    for ln in links:
        ln["s"] = (ln["kq"] * jnp.exp(ln["r"] - ln["g"])).astype(BF16)
        ln["vaug"] = _values_t_with_ones(ln["vt_ref"][ln["sl"], ln["ts"]])
        ln["vw"] = (ln["vaug"].astype(F32) * jnp.exp(ln["w_end"] - ln["m_new"])).astype(BF16)
    for ln in links:
        ln["tot"] = jnp.dot(ln["vaug"], ln["s"], preferred_element_type=F32)
        ln["local"] = jnp.dot(ln["vw"], ln["k_ref"][ln["ts"], ln["sl"]], preferred_element_type=F32)
    state = {chain: state_scr[chain] for chain in range(2 * M_HEADS)}
    for ln in links:
        chain = ln["chain"]
        if not ln["first"]:
            ln["inter"] = jnp.dot(state[chain].astype(BF16), ln["qt_ref"][ln["sl"], ln["ts"]],
                                  preferred_element_type=F32)
        tot = ln["tot"] + jnp.exp(ln["m_in"] - ln["g"]) * ln["inter"]
        den = jnp.maximum(jnp.abs(tot[M_HEAD_DIM:M_HEAD_DIM + 1]), jnp.exp(-(ln["b_row"] + ln["g"])))
        ln["o_ref"][ln["sl"], ln["ts"]] = tot[:M_HEAD_DIM] / den
        state[chain] = jnp.exp(ln["b_tot"] + ln["m_in"] - ln["m_new"]) * state[chain] + ln["local"]
    for chain in range(2 * M_HEADS):
        state_scr[chain] = state[chain]
        m_scr[chain] = jnp.broadcast_to(m_carry[chain], m_scr.shape[1:])


def _mlstm_call(mt, m, gates, mt_meta, m_meta, gates_meta, batch, chunk):
    tokens = m.shape[0]
    nc = tokens // batch // chunk

    def chunk_of(direction):
        return (lambda b, c: b * nc + c) if direction == "fwd" else (lambda b, c: b * nc + nc - 1 - c)

    def specs(direction):
        at = chunk_of(direction)
        return [pl.BlockSpec((M_WIDTH, chunk), lambda b, c: (0, at(b, c))),
                pl.BlockSpec((M_WIDTH, chunk), lambda b, c: (1, at(b, c))),
                pl.BlockSpec((chunk, M_WIDTH), lambda b, c: (at(b, c), 0)),
                pl.BlockSpec((chunk, LANES), lambda b, c: (at(b, c), 0))]

    def out_spec(direction):
        at = chunk_of(direction)
        return pl.BlockSpec((M_WIDTH, chunk), lambda b, c: (0, at(b, c)))

    return pl.pallas_call(
        _mlstm_kernel,
        grid=(batch, nc),
        in_specs=specs("fwd") + specs("bwd") + [pl.BlockSpec(m_meta.shape, lambda b, c: (0, 0)),
                                                pl.BlockSpec(mt_meta.shape, lambda b, c: (0, 0)),
                                                pl.BlockSpec(gates_meta.shape, lambda b, c: (0, 0))],
        out_specs=[out_spec("fwd"), out_spec("bwd")],
        out_shape=[jax.ShapeDtypeStruct((M_WIDTH, tokens), F32)] * 2,
        scratch_shapes=[pltpu.VMEM((2 * M_HEADS, STATE_ROWS, M_HEAD_DIM), F32),
                        pltpu.VMEM((2 * M_HEADS, 8, LANES), F32)],
        compiler_params=pltpu.CompilerParams(dimension_semantics=("parallel", "arbitrary"),
                                             vmem_limit_bytes=VMEM_LIMIT_BYTES),
        name="mlstm",
    )(mt, mt, m, gates, mt, mt, m, gates, m_meta, mt_meta, gates_meta)


def _back_kernel(h_ref, ao_ref, hf_ref, hb_ref, mo_ref, merge_ref, mgain_ref, wab_ref, wmb_ref, wout_ref,
                 g2_ref, w1_ref, w3_ref, w2_ref, y_ref):
    rows, d_model = h_ref.shape
    for r in range(0, rows, SUB_ROWS):
        rs = slice(r, r + SUB_ROWS)
        a_out = jnp.dot(ao_ref[rs, :], wab_ref[...], preferred_element_type=F32)

        hm = (hf_ref[:, rs] + hb_ref[:, rs]).T
        o_gate = jax.nn.sigmoid(mo_ref[rs, :].astype(F32))
        gated = []
        for hd in range(M_HEADS):
            sl = slice(hd * M_HEAD_DIM, (hd + 1) * M_HEAD_DIM)
            gated.append((_rmsnorm(hm[:, sl], mgain_ref[:, sl]) * o_gate[:, sl]).astype(BF16))
        m_out = jnp.dot(jnp.concatenate(gated, axis=1), wmb_ref[...], preferred_element_type=F32)

        merged = (merge_ref[rs, :d_model].astype(F32) * a_out + merge_ref[rs, d_model:].astype(F32) * m_out)
        h2 = h_ref[rs, :] + jnp.dot(merged.astype(BF16), wout_ref[...], preferred_element_type=F32)
        y_ref[rs, :] = _swiglu_half_residual(h2, g2_ref[...], w1_ref, w3_ref, w2_ref)


def _back_call(h, attn_o, hf, hb, m, merge, wts, rows):
    tokens, d_model = h.shape

    def row_spec(width, col=0):
        return pl.BlockSpec((rows, width), lambda i: (i, col))

    col_spec = pl.BlockSpec((M_WIDTH, rows), lambda i: (0, i))
    consts = (wts["mgain"], wts["wab"], wts["wmb"], wts["wout"], wts["g2"], wts["w1b"], wts["w3b"], wts["w2b"])
    return pl.pallas_call(
        _back_kernel,
        grid=(tokens // rows,),
        in_specs=[row_spec(d_model), row_spec(ATTN_WIDTH), col_spec, col_spec,
                  row_spec(M_WIDTH, 1), row_spec(2 * d_model)] + [_const_spec(c.shape) for c in consts],
        out_specs=row_spec(d_model),
        out_shape=jax.ShapeDtypeStruct((tokens, d_model), F32),
        compiler_params=pltpu.CompilerParams(dimension_semantics=("parallel",),
                                             vmem_limit_bytes=VMEM_LIMIT_BYTES),
        name="back",
    )(h, attn_o, hf, hb, m, merge, *consts)


def _pair_heads(w):
    lead = w.shape[:-1]
    w = w.reshape(*lead, N_KV_HEADS, Q_PER_KV, HEAD_DIM)
    return jnp.swapaxes(w, -3, -2).reshape(*lead, ATTN_WIDTH)


def _prepare_weights(g_ffn1, w1_ffn1, w3_ffn1, w2_ffn1, g_mix, w_in, b_i, b_f, q_gain, k_gain, m_gain,
                     w_attn_br, w_mlstm_br, w_out, g_ffn2, w1_ffn2, w3_ffn2, w2_ffn2):
    d_model = w_in.shape[1]
    attn_w = N_Q_HEADS * HEAD_DIM
    kv_w = N_KV_HEADS * HEAD_DIM
    splits = (attn_w, kv_w, kv_w, M_WIDTH, M_WIDTH, M_WIDTH, M_WIDTH, 2 * M_HEADS, 2 * M_HEADS, d_model, d_model)
    offs = np.cumsum((0,) + splits)
    assert offs[-1] == w_in.shape[2]
    cols = [w_in[0, :, offs[i]:offs[i + 1]] for i in range(len(splits))]
    aq, ak, av, mq, mk, mv, mo, gi, gf, ga, gm = cols
    gate_w = jnp.pad(jnp.concatenate([gi, gf], axis=1), ((0, 0), (0, LANES - 4 * M_HEADS)))
    gate_b = jnp.pad(jnp.concatenate([b_i[0], b_f[0]]), (0, LANES - 4 * M_HEADS))[None]
    row = lambda g: g.astype(F32)[None]
    both_halves = lambda g: jnp.tile(g[0].astype(F32), LANES // HEAD_DIM)[None]
    return {
        "g1": row(g_ffn1[0]), "w1a": w1_ffn1[0].astype(BF16), "w3a": w3_ffn1[0].astype(BF16),
        "w2a": w2_ffn1[0].astype(BF16),
        "gmix": row(g_mix[0]),
        "wq": _pair_heads(aq).astype(BF16), "wkv": jnp.concatenate([ak, av], axis=1).astype(BF16),
        "wm": jnp.concatenate([mq, mk, mv, mo], axis=1).astype(BF16),
        "wg": gate_w.astype(BF16), "gbias": gate_b.astype(F32),
        "wmerge": jnp.concatenate([ga, gm], axis=1).astype(BF16),
        "qgain": both_halves(q_gain), "kgain": both_halves(k_gain),
        "mgain": m_gain[0].astype(F32).reshape(1, M_WIDTH),
        "wab": _pair_heads(w_attn_br[0].T).T.astype(BF16), "wmb": w_mlstm_br[0].astype(BF16), "wout": w_out[0].astype(BF16),
        "g2": row(g_ffn2[0]), "w1b": w1_ffn2[0].astype(BF16), "w3b": w3_ffn2[0].astype(BF16),
        "w2b": w2_ffn2[0].astype(BF16),
    }


def _rope_tables(n_tok):
    t = jnp.arange(n_tok)
    pos = jnp.stack([(t // GRID_W).astype(F32), (t % GRID_W).astype(F32)], axis=-1)
    inv_freq = ROPE_THETA ** (-2.0 * jnp.arange(ROPE_FREQS, dtype=F32) / ROPE_AXIS_DIM)
    ang = pos[:, :, None] * inv_freq
    cos, sin = jnp.cos(ang), jnp.sin(ang)
    cos = jnp.stack([cos, cos], axis=2).reshape(n_tok, HEAD_DIM)
    sin = jnp.stack([-sin, sin], axis=2).reshape(n_tok, HEAD_DIM)
    reps = (1, LANES // HEAD_DIM)
    return jnp.tile(cos, reps), jnp.tile(sin, reps)


def _trunk(x, meta_parts, wts):
    batch, n_tok, d_model = x.shape
    k_meta, v_meta, mt_meta, m_meta, gates_meta, norms_meta = meta_parts
    cos, sin = _rope_tables(n_tok)
    h, q, k, v, mt, m, gates, merge, norms = _front_call(
        x.reshape(batch * n_tok, d_model), cos, sin, wts, FRONT_ROWS)
    q_rows, k_rows, chunk = min(ATTN_Q_ROWS, n_tok), min(ATTN_K_ROWS, n_tok), min(MLSTM_STEP_TOKENS, n_tok)
    assert n_tok % FRONT_ROWS == 0 and n_tok % q_rows == 0 and n_tok % k_rows == 0 and n_tok % chunk == 0
    assert q_rows % FRONT_ROWS == 0 and k_rows % FRONT_ROWS == 0
    safe, floor = _attn_safe_blocks(norms, norms_meta, batch, n_tok // q_rows, n_tok // k_rows)
    attn_o = _attn_call(q, k, v, k_meta, v_meta, safe, floor, batch, q_rows, k_rows)
    hf, hb = _mlstm_call(mt, m, gates, mt_meta, m_meta, gates_meta, batch, chunk)
    y = _back_call(h, attn_o, hf, hb, m, merge, wts, BACK_ROWS)
    return y.reshape(batch, n_tok, d_model)


def kernel(x_prompt, x_sample, meta, g_ffn1, w1_ffn1, w3_ffn1, w2_ffn1, g_mix, w_in, b_i, b_f, q_gain, k_gain,
           m_gain, w_attn_br, w_mlstm_br, w_out, g_ffn2, w1_ffn2, w3_ffn2, w2_ffn2):
    assert w_in.shape[0] == 1, "single-layer trunk: the meta rows' mixer outputs are never consumed"
    assert meta.shape[0] == N_META
    wts = _prepare_weights(g_ffn1, w1_ffn1, w3_ffn1, w2_ffn1, g_mix, w_in, b_i, b_f, q_gain, k_gain, m_gain,
                           w_attn_br, w_mlstm_br, w_out, g_ffn2, w1_ffn2, w3_ffn2, w2_ffn2)
    ones = jnp.ones((META_ROWS, LANES), F32)
    meta_rows = jnp.pad(meta.astype(F32), ((0, META_ROWS - N_META), (0, 0)))
    _, _, k_meta, v_meta, mt_meta, m_meta, gates_meta, _, norms_meta = _front_call(
        meta_rows, ones, jnp.zeros_like(ones), wts, META_ROWS)
    meta_parts = (k_meta, v_meta, mt_meta, m_meta, gates_meta, norms_meta)
    return (_trunk(x_prompt, meta_parts, wts), _trunk(x_sample, meta_parts, wts))
```

```python
import jax
import jax.numpy as jnp
import numpy as np
from jax import lax
from jax.experimental import pallas as pl
from jax.experimental.pallas import tpu as pltpu

F32 = jnp.float32
BF16 = jnp.bfloat16

N_META = 16
GRID_W = 64
EPS = 1e-6
NEG = -1e30
N_Q_HEADS = 8
N_KV_HEADS = 2
Q_PER_KV = N_Q_HEADS // N_KV_HEADS
HEAD_DIM = 64
ROPE_AXIS_DIM = HEAD_DIM // 2
ROPE_FREQS = ROPE_AXIS_DIM // 2
ROPE_THETA = 10000.0
M_HEADS = 4
M_HEAD_DIM = 128
M_WIDTH = M_HEADS * M_HEAD_DIM

LANES = 128
BF16_ROWS = 16
VMEM_LIMIT_BYTES = 60 * 1024 * 1024

SUB_ROWS = 256
FRONT_ROWS = 2 * SUB_ROWS
BACK_ROWS = 2 * SUB_ROWS
ATTN_Q_ROWS = 512
ATTN_K_ROWS = 4096
MLSTM_CHUNK = 256
MLSTM_STEP_TOKENS = 4 * MLSTM_CHUNK

Q_PAD = N_Q_HEADS * LANES
ATTN_WIDTH = N_Q_HEADS * HEAD_DIM
KV_WIDTH = N_KV_HEADS * HEAD_DIM
assert KV_WIDTH == LANES and HEAD_DIM * 2 == LANES
Q_SCALE = HEAD_DIM ** -0.5 * 1.4426950408889634
ATTN_SAFE_LOG2 = 90.0
ATTN_V_ROWS = HEAD_DIM + BF16_ROWS
META_ROWS = LANES
STATE_ROWS = M_HEAD_DIM + BF16_ROWS


def _const_spec(shape):
    zeros = (0,) * len(shape)
    return pl.BlockSpec(shape, lambda *_: zeros, pipeline_mode=pl.Buffered(1))


def _rmsnorm(x, g):
    return x * lax.rsqrt(jnp.mean(x * x, axis=-1, keepdims=True) + EPS) * g


def _swiglu_half_residual(x, g, w1_ref, w3_ref, w2_ref):
    u = _rmsnorm(x, g).astype(BF16)
    a = jnp.dot(u, w1_ref[...], preferred_element_type=F32)
    b = jnp.dot(u, w3_ref[...], preferred_element_type=F32)
    act = (a * jax.nn.sigmoid(a) * b).astype(BF16)
    return x + 0.5 * jnp.dot(act, w2_ref[...], preferred_element_type=F32)


def _log_sigmoid(x):
    return jnp.minimum(x, 0.0) - jnp.log1p(jnp.exp(-jnp.abs(x)))


def _half_sums(x, low_half):
    low = jnp.sum(jnp.where(low_half, x, 0.0), axis=-1, keepdims=True)
    high = jnp.sum(jnp.where(low_half, 0.0, x), axis=-1, keepdims=True)
    return low, high


def _head_pair_norm_rope(x, gain, cos, sin_signed, low_half, first_half):
    ms_low, ms_high = _half_sums(x * x, low_half)
    inv = jnp.where(low_half, lax.rsqrt(ms_low * (1.0 / HEAD_DIM) + EPS), lax.rsqrt(ms_high * (1.0 / HEAD_DIM) + EPS))
    y = x * inv * gain
    partner = jnp.where(first_half, pltpu.roll(y, LANES - ROPE_FREQS, 1), pltpu.roll(y, ROPE_FREQS, 1))
    return y * cos + partner * sin_signed


def _front_kernel(x_ref, cos_ref, sin_ref, g1_ref, w1_ref, w3_ref, w2_ref, gmix_ref, wq_ref, wkv_ref,
                  wm_ref, wg_ref, wmerge_ref, qgain_ref, kgain_ref, gbias_ref,
                  h_ref, q_ref, k_ref, v_ref, mt_ref, m_ref, gate_ref, merge_ref, norm_ref):
    sub = min(x_ref.shape[0], SUB_ROWS)
    lane = lax.broadcasted_iota(jnp.int32, (sub, LANES), 1)
    low_half = lane < HEAD_DIM
    first_half = (lane % ROPE_AXIS_DIM) < ROPE_FREQS

    def sq_norms(y_bf16):
        y = y_bf16.astype(F32)
        return jnp.maximum(*_half_sums(y * y, low_half))

    q_sq = jnp.zeros((sub, 1), F32)
    k_sq = jnp.zeros((sub, 1), F32)
    for r in range(0, x_ref.shape[0], sub):
        rs = slice(r, r + sub)
        h = _swiglu_half_residual(x_ref[rs, :], g1_ref[...], w1_ref, w3_ref, w2_ref)
        h_ref[rs, :] = h
        u = _rmsnorm(h, gmix_ref[...]).astype(BF16)
        cos = cos_ref[rs, :]
        sin = sin_ref[rs, :]

        zq = jnp.dot(u, wq_ref[...], preferred_element_type=F32)
        for pair in range(Q_PER_KV):
            y = _head_pair_norm_rope(zq[:, pair * LANES:(pair + 1) * LANES], qgain_ref[...], cos, sin,
                                     low_half, first_half) * Q_SCALE
            q_sq = jnp.maximum(q_sq, sq_norms(y.astype(BF16)))
            for group, keep in enumerate((low_half, ~low_half)):
                hd = group * Q_PER_KV + pair
                q_ref[hd * LANES:(hd + 1) * LANES, rs] = jnp.where(keep, y, 0.0).T.astype(BF16)

        zkv = jnp.dot(u, wkv_ref[...], preferred_element_type=F32)
        k = _head_pair_norm_rope(zkv[:, :LANES], kgain_ref[...], cos, sin, low_half, first_half).astype(BF16)
        k_ref[rs, :] = k
        k_sq = jnp.maximum(k_sq, sq_norms(k))
        v_ref[:, rs] = zkv[:, LANES:].T.astype(BF16)

        zm = jnp.dot(u, wm_ref[...], preferred_element_type=F32)
        mt_ref[:M_WIDTH, rs] = zm[:, :M_WIDTH].T.astype(BF16)
        mt_ref[M_WIDTH:, rs] = zm[:, 2 * M_WIDTH:3 * M_WIDTH].T.astype(BF16)
        m_ref[rs, :M_WIDTH] = (zm[:, M_WIDTH:2 * M_WIDTH] * (M_HEAD_DIM ** -0.5)).astype(BF16)
        m_ref[rs, M_WIDTH:] = zm[:, 3 * M_WIDTH:].astype(BF16)

        zg = jnp.dot(u, wg_ref[...], preferred_element_type=F32) + gbias_ref[...]
        gate_ref[rs, :] = jnp.where(lane < 2 * M_HEADS, zg, _log_sigmoid(zg))

        zmerge = jnp.dot(u, wmerge_ref[...], preferred_element_type=F32)
        merge_ref[rs, :] = jax.nn.sigmoid(zmerge).astype(BF16)

    tile_row = lax.broadcasted_iota(jnp.int32, norm_ref.shape, 0)
    norm_ref[...] = jnp.where(tile_row == 0, jnp.max(q_sq, axis=0, keepdims=True),
                              jnp.max(k_sq, axis=0, keepdims=True))


def _front_call(x, cos, sin, wts, rows):
    tokens, d_model = x.shape
    steps = tokens // rows
    table_steps = cos.shape[0] // rows

    def row_spec(width):
        return pl.BlockSpec((rows, width), lambda i: (i, 0))

    table_spec = pl.BlockSpec((rows, LANES), lambda i: (i % table_steps, 0))
    consts = (wts["g1"], wts["w1a"], wts["w3a"], wts["w2a"], wts["gmix"], wts["wq"], wts["wkv"],
              wts["wm"], wts["wg"], wts["wmerge"], wts["qgain"], wts["kgain"], wts["gbias"])

    def col_spec(height):
        return pl.BlockSpec((height, rows), lambda i: (0, i))

    out_specs = [row_spec(d_model), col_spec(Q_PAD), row_spec(KV_WIDTH), col_spec(KV_WIDTH), col_spec(2 * M_WIDTH),
                 row_spec(2 * M_WIDTH), row_spec(LANES), row_spec(2 * d_model),
                 pl.BlockSpec((None, 8, LANES), lambda i: (i, 0, 0))]
    out_shape = [jax.ShapeDtypeStruct((tokens, d_model), F32), jax.ShapeDtypeStruct((Q_PAD, tokens), BF16),
                 jax.ShapeDtypeStruct((tokens, KV_WIDTH), BF16), jax.ShapeDtypeStruct((KV_WIDTH, tokens), BF16),
                 jax.ShapeDtypeStruct((2 * M_WIDTH, tokens), BF16),
                 jax.ShapeDtypeStruct((tokens, 2 * M_WIDTH), BF16), jax.ShapeDtypeStruct((tokens, LANES), F32),
                 jax.ShapeDtypeStruct((tokens, 2 * d_model), BF16), jax.ShapeDtypeStruct((steps, 8, LANES), F32)]
    return pl.pallas_call(
        _front_kernel,
        grid=(steps,),
        in_specs=[row_spec(d_model), table_spec, table_spec] + [_const_spec(c.shape) for c in consts],
        out_specs=out_specs,
        out_shape=out_shape,
        compiler_params=pltpu.CompilerParams(dimension_semantics=("parallel",),
                                             vmem_limit_bytes=VMEM_LIMIT_BYTES),
        name="front",
    )(x, cos, sin, *consts)


def _attn_kernel(safe_ref, floor_ref, qt_ref, k_ref, vt_ref, kmeta_ref, vtmeta_ref, o_ref, m_scr, acc_scr):
    ki = pl.program_id(2)
    q_cols = qt_ref.shape[1]
    step = (pl.program_id(0) * pl.num_programs(1) + pl.program_id(1)) * pl.num_programs(2) + ki
    safe = safe_ref[step]

    def scores_t(hd, keys_ref):
        return jnp.dot(keys_ref[...], qt_ref[hd * LANES:(hd + 1) * LANES, :], preferred_element_type=F32)

    def values_t(hd, values_ref):
        g = hd // Q_PER_KV
        row = lax.broadcasted_iota(jnp.int32, (ATTN_V_ROWS - HEAD_DIM, values_ref.shape[1]), 0)
        ones = jnp.where(row == 0, 1.0, 0.0).astype(BF16)
        return jnp.concatenate([values_ref[g * HEAD_DIM:(g + 1) * HEAD_DIM, :], ones], axis=0)

    def meta_scores_t(hd):
        real = lax.broadcasted_iota(jnp.int32, (kmeta_ref.shape[0], q_cols), 0) < N_META
        return jnp.where(real, scores_t(hd, kmeta_ref), NEG)

    @pl.when((ki == 0) & (safe == 1))
    def _seed_from_meta_keys_against_floor():
        floor = jnp.full((1, q_cols), floor_ref[pl.program_id(0) * pl.num_programs(1) + pl.program_id(1)], F32)
        scores = [jnp.dot(kmeta_ref[:N_META, :], qt_ref[hd * LANES:(hd + 1) * LANES, :],
                          preferred_element_type=F32) for hd in range(N_Q_HEADS)]
        for hd, s in enumerate(scores):
            p = jnp.exp2(s - floor).astype(BF16)
            acc = jnp.dot(values_t(hd, vtmeta_ref)[:, :N_META], p, preferred_element_type=F32)
            m = jnp.maximum(floor, jnp.max(s, axis=0, keepdims=True))
            acc_scr[hd] = jnp.exp2(floor - m) * acc
            m_scr[hd] = jnp.broadcast_to(m, m_scr.shape[1:])

    @pl.when((ki == 0) & (safe != 1))
    def _seed_from_meta_keys():
        for hd in range(N_Q_HEADS):
            s = meta_scores_t(hd)
            m = jnp.max(s, axis=0, keepdims=True)
            p = jnp.exp2(s - m).astype(BF16)
            acc_scr[hd] = jnp.dot(values_t(hd, vtmeta_ref), p, preferred_element_type=F32)
            m_scr[hd] = jnp.broadcast_to(m, m_scr.shape[1:])

    @pl.when(safe == 1)
    def _block_against_previous_maximum():
        s_next = scores_t(0, k_ref)
        for hd in range(N_Q_HEADS):
            m_old = m_scr[hd][:1]
            s = s_next
            if hd + 1 < N_Q_HEADS:
                s_next = scores_t(hd + 1, k_ref)
            p = jnp.exp2(s - m_old).astype(BF16)
            pv = jnp.dot(values_t(hd, vt_ref), p, preferred_element_type=F32)
            m_new = jnp.maximum(m_old, jnp.max(s, axis=0, keepdims=True))
            acc_scr[hd] = jnp.exp2(m_old - m_new) * (acc_scr[hd] + pv)
            m_scr[hd] = jnp.broadcast_to(m_new, m_scr.shape[1:])

    @pl.when(safe != 1)
    def _block_against_own_maximum():
        for hd in range(N_Q_HEADS):
            m_old = m_scr[hd][:1]
            s = scores_t(hd, k_ref)
            m_new = jnp.maximum(m_old, jnp.max(s, axis=0, keepdims=True))
            p = jnp.exp2(s - m_new).astype(BF16)
            pv = jnp.dot(values_t(hd, vt_ref), p, preferred_element_type=F32)
            acc_scr[hd] = jnp.exp2(m_old - m_new) * acc_scr[hd] + pv
            m_scr[hd] = jnp.broadcast_to(m_new, m_scr.shape[1:])

    @pl.when(ki == pl.num_programs(2) - 1)
    def _normalise():
        def normalised(hd):
            acc = acc_scr[hd]
            return acc[:HEAD_DIM] / acc[HEAD_DIM:HEAD_DIM + 1]

        for pair in range(Q_PER_KV):
            both_t = jnp.concatenate([normalised(pair), normalised(Q_PER_KV + pair)], axis=0)
            o_ref[:, pair * LANES:(pair + 1) * LANES] = both_t.T.astype(BF16)


def _attn_safe_blocks(norms, norms_meta, batch, nq, nk):
    def block_max(sq, blocks):
        return jnp.sqrt(jnp.max(sq.reshape(blocks, -1), axis=1))

    q_norm = block_max(norms[:, 0, 0], batch * nq).reshape(batch, nq, 1)
    k_norm = block_max(norms[:, 1, 0], batch * nk).reshape(batch, 1, nk)
    k_meta_norm = jnp.sqrt(jnp.max(norms_meta[:, 1, 0]))
    gap = q_norm * (jnp.maximum(k_norm, k_meta_norm) + k_meta_norm)
    safe = (gap <= ATTN_SAFE_LOG2).astype(jnp.int32).reshape(-1)
    floor = (-q_norm * k_meta_norm).reshape(-1)
    return safe, floor


def _attn_call(qt, k, vt, k_meta, vt_meta, safe, floor, batch, q_rows, k_rows):
    tokens = k.shape[0]
    n_tok = tokens // batch
    nq = n_tok // q_rows
    nk = n_tok // k_rows
    grid_spec = pltpu.PrefetchScalarGridSpec(
        num_scalar_prefetch=2,
        grid=(batch, nq, nk),
        in_specs=[
            pl.BlockSpec((Q_PAD, q_rows), lambda b, i, j, *_: (0, b * nq + i)),
            pl.BlockSpec((k_rows, KV_WIDTH), lambda b, i, j, *_: (b * nk + j, 0)),
            pl.BlockSpec((KV_WIDTH, k_rows), lambda b, i, j, *_: (0, b * nk + j)),
            pl.BlockSpec(k_meta.shape, lambda b, i, j, *_: (0, 0)),
            pl.BlockSpec(vt_meta.shape, lambda b, i, j, *_: (0, 0)),
        ],
        out_specs=pl.BlockSpec((q_rows, ATTN_WIDTH), lambda b, i, j, *_: (b * nq + i, 0)),
        scratch_shapes=[pltpu.VMEM((N_Q_HEADS, 8, q_rows), F32),
                        pltpu.VMEM((N_Q_HEADS, ATTN_V_ROWS, q_rows), F32)],
    )
    return pl.pallas_call(
        _attn_kernel,
        grid_spec=grid_spec,
        out_shape=jax.ShapeDtypeStruct((tokens, ATTN_WIDTH), BF16),
        compiler_params=pltpu.CompilerParams(dimension_semantics=("parallel", "parallel", "arbitrary"),
                                             vmem_limit_bytes=VMEM_LIMIT_BYTES),
        name="attn",
    )(safe, floor, qt, k, vt, k_meta, vt_meta)


def _split_hi_lo(x):
    hi = x.astype(BF16)
    lo = (x - hi.astype(F32)).astype(BF16)
    return hi, lo


def _gate_cumsum(gates, causal):
    t = gates.shape[0]
    row = lax.broadcasted_iota(jnp.int32, (t, t), 0)
    col = lax.broadcasted_iota(jnp.int32, (t, t), 1)
    mask = (col <= row) if causal else (col >= row)
    tri = jnp.where(mask, 1.0, 0.0).astype(BF16)
    hi, lo = _split_hi_lo(gates)
    sums = jnp.dot(tri, hi, preferred_element_type=F32) + jnp.dot(tri, lo, preferred_element_type=F32)
    return sums, mask


def _values_t_with_ones(vt):
    row = lax.broadcasted_iota(jnp.int32, (STATE_ROWS - M_HEAD_DIM, vt.shape[1]), 0)
    return jnp.concatenate([vt, jnp.where(row == 0, 1.0, 0.0).astype(vt.dtype)], axis=0)


def _state_update(state_t, m_in, k, vaug_t, li_row, b_row, b_tot):
    w_end = b_tot - b_row + li_row
    m_new = jnp.maximum(b_tot + m_in, jnp.max(w_end, axis=1, keepdims=True))
    vw = (vaug_t.astype(F32) * jnp.exp(w_end - m_new)).astype(BF16)
    return jnp.exp(b_tot + m_in - m_new) * state_t + jnp.dot(vw, k, preferred_element_type=F32), m_new


def _mlstm_kernel(qtf_ref, vtf_ref, kf_ref, gf_ref, qtb_ref, vtb_ref, kb_ref, gb_ref, kmeta_ref, vtmeta_ref,
                  gmeta_ref, hf_ref, hb_ref, state_scr, m_scr):
    c = pl.program_id(1)

    @pl.when(c == 0)
    def _reset_and_absorb_meta():
        state_scr[...] = jnp.zeros_like(state_scr)
        m_scr[...] = jnp.zeros_like(m_scr)
        gates = gmeta_ref[...]
        row = lax.broadcasted_iota(jnp.int32, gates.shape, 0)
        lane = lax.broadcasted_iota(jnp.int32, gates.shape, 1)
        gates = jnp.where(row < N_META, gates, jnp.where(lane < 2 * M_HEADS, NEG, 0.0))
        sums, _ = _gate_cumsum(gates, causal=True)
        gates_t = gates.T
        sums_t = sums.T
        last = gates.shape[0] - 1
        for hd in range(M_HEADS):
            f_lane = 2 * M_HEADS + hd
            sl = slice(hd * M_HEAD_DIM, (hd + 1) * M_HEAD_DIM)
            vt = vtmeta_ref[M_WIDTH + hd * M_HEAD_DIM:M_WIDTH + (hd + 1) * M_HEAD_DIM, :]
            new_state, m_new = _state_update(
                state_scr[hd], m_scr[hd][:1, :1], kmeta_ref[:, sl], _values_t_with_ones(vt),
                gates_t[hd:hd + 1, :], sums_t[f_lane:f_lane + 1, :], sums[last:last + 1, f_lane:f_lane + 1])
            state_scr[hd] = new_state
            m_scr[hd] = jnp.broadcast_to(m_new, m_scr.shape[1:])

    t = MLSTM_CHUNK if gf_ref.shape[0] % MLSTM_CHUNK == 0 else gf_ref.shape[0]
    n_sub = gf_ref.shape[0] // t
    src = lax.broadcasted_iota(jnp.int32, (t, t), 0)
    tgt = lax.broadcasted_iota(jnp.int32, (t, t), 1)
    links = []
    m_carry = {chain: m_scr[chain][:1, :1] for chain in range(2 * M_HEADS)}
    for order in range(n_sub):
        for causal, qt_ref, vt_ref, k_ref, g_ref, o_ref in ((True, qtf_ref, vtf_ref, kf_ref, gf_ref, hf_ref),
                                                            (False, qtb_ref, vtb_ref, kb_ref, gb_ref, hb_ref)):
            j = order if causal else n_sub - 1 - order
            ts = slice(j * t, (j + 1) * t)
            gates = g_ref[ts, :]
            sums, _ = _gate_cumsum(gates, causal)
            gates_t = gates.T
            sums_t = sums.T
            feeds = (src <= tgt) if causal else (src >= tgt)
            last = t - 1 if causal else 0
            for hd in range(M_HEADS):
                chain = hd if causal else M_HEADS + hd
                i_lane, f_lane = chain, 2 * M_HEADS + chain
                li_row, b_row = gates_t[i_lane:i_lane + 1, :], sums_t[f_lane:f_lane + 1, :]
                b_tot = sums[last:last + 1, f_lane:f_lane + 1]
                w_end = b_tot - b_row + li_row
                m_in = m_carry[chain]
                m_new = jnp.maximum(b_tot + m_in, jnp.max(w_end, axis=1, keepdims=True))
                m_carry[chain] = m_new
                links.append(dict(
                    chain=chain, first=order == 0, feeds=feeds, o_ref=o_ref, ts=ts,
                    sl=slice(hd * M_HEAD_DIM, (hd + 1) * M_HEAD_DIM), qt_ref=qt_ref, vt_ref=vt_ref, k_ref=k_ref,
                    r_col=gates[:, i_lane:i_lane + 1] - sums[:, f_lane:f_lane + 1],
                    b_row=b_row, b_tot=b_tot, w_end=w_end, m_in=m_in, m_new=m_new))

    for ln in links:
        k = ln["k_ref"][ln["ts"], ln["sl"]]
        qt = ln["qt_ref"][ln["sl"], ln["ts"]]
        if ln["first"]:
            both = jnp.dot(jnp.concatenate([k, state_scr[ln["chain"]].astype(BF16)], axis=0), qt,
                           preferred_element_type=F32)
            ln["kq"], ln["inter"] = both[:t], both[t:]
        else:
            ln["kq"] = jnp.dot(k, qt, preferred_element_type=F32)
    for ln in links:
        ln["r"] = jnp.where(ln["feeds"], ln["r_col"], NEG)
        ln["g"] = jnp.maximum(jnp.max(ln["r"], axis=0, keepdims=True), ln["m_in"])
    for ln in links:
        ln["s"] = (ln["kq"] * jnp.exp(ln["r"] - ln["g"])).astype(BF16)
        ln["vaug"] = _values_t_with_ones(ln["vt_ref"][ln["sl"], ln["ts"]])
        ln["vw"] = (ln["vaug"].astype(F32) * jnp.exp(ln["w_end"] - ln["m_new"])).astype(BF16)
    for ln in links:
        ln["tot"] = jnp.dot(ln["vaug"], ln["s"], preferred_element_type=F32)
        ln["local"] = jnp.dot(ln["vw"], ln["k_ref"][ln["ts"], ln["sl"]], preferred_element_type=F32)
    state = {chain: state_scr[chain] for chain in range(2 * M_HEADS)}
    for ln in links:
        chain = ln["chain"]
        if not ln["first"]:
            ln["inter"] = jnp.dot(state[chain].astype(BF16), ln["qt_ref"][ln["sl"], ln["ts"]],
                                  preferred_element_type=F32)
        tot = ln["tot"] + jnp.exp(ln["m_in"] - ln["g"]) * ln["inter"]
        den = jnp.maximum(jnp.abs(tot[M_HEAD_DIM:M_HEAD_DIM + 1]), jnp.exp(-(ln["b_row"] + ln["g"])))
        ln["o_ref"][ln["sl"], ln["ts"]] = tot[:M_HEAD_DIM] / den
        state[chain] = jnp.exp(ln["b_tot"] + ln["m_in"] - ln["m_new"]) * state[chain] + ln["local"]
    for chain in range(2 * M_HEADS):
        state_scr[chain] = state[chain]
        m_scr[chain] = jnp.broadcast_to(m_carry[chain], m_scr.shape[1:])


def _mlstm_call(mt, m, gates, mt_meta, m_meta, gates_meta, batch, chunk):
    tokens = m.shape[0]
    nc = tokens // batch // chunk

    def chunk_of(direction):
        return (lambda b, c: b * nc + c) if direction == "fwd" else (lambda b, c: b * nc + nc - 1 - c)

    def specs(direction):
        at = chunk_of(direction)
        return [pl.BlockSpec((M_WIDTH, chunk), lambda b, c: (0, at(b, c))),
                pl.BlockSpec((M_WIDTH, chunk), lambda b, c: (1, at(b, c))),
                pl.BlockSpec((chunk, M_WIDTH), lambda b, c: (at(b, c), 0)),
                pl.BlockSpec((chunk, LANES), lambda b, c: (at(b, c), 0))]

    def out_spec(direction):
        at = chunk_of(direction)
        return pl.BlockSpec((M_WIDTH, chunk), lambda b, c: (0, at(b, c)))

    return pl.pallas_call(
        _mlstm_kernel,
        grid=(batch, nc),
        in_specs=specs("fwd") + specs("bwd") + [pl.BlockSpec(m_meta.shape, lambda b, c: (0, 0)),
                                                pl.BlockSpec(mt_meta.shape, lambda b, c: (0, 0)),
                                                pl.BlockSpec(gates_meta.shape, lambda b, c: (0, 0))],
        out_specs=[out_spec("fwd"), out_spec("bwd")],
        out_shape=[jax.ShapeDtypeStruct((M_WIDTH, tokens), F32)] * 2,
        scratch_shapes=[pltpu.VMEM((2 * M_HEADS, STATE_ROWS, M_HEAD_DIM), F32),
                        pltpu.VMEM((2 * M_HEADS, 8, LANES), F32)],
        compiler_params=pltpu.CompilerParams(dimension_semantics=("parallel", "arbitrary"),
                                             vmem_limit_bytes=VMEM_LIMIT_BYTES),
        name="mlstm",
    )(mt, mt, m, gates, mt, mt, m, gates, m_meta, mt_meta, gates_meta)


def _back_kernel(h_ref, ao_ref, hf_ref, hb_ref, mo_ref, merge_ref, mgain_ref, wab_ref, wmb_ref, wout_ref,
                 g2_ref, w1_ref, w3_ref, w2_ref, y_ref):
    rows, d_model = h_ref.shape
    for r in range(0, rows, SUB_ROWS):
        rs = slice(r, r + SUB_ROWS)
        a_out = jnp.dot(ao_ref[rs, :], wab_ref[...], preferred_element_type=F32)

        hm = (hf_ref[:, rs] + hb_ref[:, rs]).T
        o_gate = jax.nn.sigmoid(mo_ref[rs, :].astype(F32))
        gated = []
        for hd in range(M_HEADS):
            sl = slice(hd * M_HEAD_DIM, (hd + 1) * M_HEAD_DIM)
            gated.append((_rmsnorm(hm[:, sl], mgain_ref[:, sl]) * o_gate[:, sl]).astype(BF16))
        m_out = jnp.dot(jnp.concatenate(gated, axis=1), wmb_ref[...], preferred_element_type=F32)

        merged = (merge_ref[rs, :d_model].astype(F32) * a_out + merge_ref[rs, d_model:].astype(F32) * m_out)
        h2 = h_ref[rs, :] + jnp.dot(merged.astype(BF16), wout_ref[...], preferred_element_type=F32)
        y_ref[rs, :] = _swiglu_half_residual(h2, g2_ref[...], w1_ref, w3_ref, w2_ref)


def _back_call(h, attn_o, hf, hb, m, merge, wts, rows):
    tokens, d_model = h.shape

    def row_spec(width, col=0):
        return pl.BlockSpec((rows, width), lambda i: (i, col))

    col_spec = pl.BlockSpec((M_WIDTH, rows), lambda i: (0, i))
    consts = (wts["mgain"], wts["wab"], wts["wmb"], wts["wout"], wts["g2"], wts["w1b"], wts["w3b"], wts["w2b"])
    return pl.pallas_call(
        _back_kernel,
        grid=(tokens // rows,),
        in_specs=[row_spec(d_model), row_spec(ATTN_WIDTH), col_spec, col_spec,
                  row_spec(M_WIDTH, 1), row_spec(2 * d_model)] + [_const_spec(c.shape) for c in consts],
        out_specs=row_spec(d_model),
        out_shape=jax.ShapeDtypeStruct((tokens, d_model), F32),
        compiler_params=pltpu.CompilerParams(dimension_semantics=("parallel",),
                                             vmem_limit_bytes=VMEM_LIMIT_BYTES),
        name="back",
    )(h, attn_o, hf, hb, m, merge, *consts)


def _pair_heads(w):
    lead = w.shape[:-1]
    w = w.reshape(*lead, N_KV_HEADS, Q_PER_KV, HEAD_DIM)
    return jnp.swapaxes(w, -3, -2).reshape(*lead, ATTN_WIDTH)


def _prepare_weights(g_ffn1, w1_ffn1, w3_ffn1, w2_ffn1, g_mix, w_in, b_i, b_f, q_gain, k_gain, m_gain,
                     w_attn_br, w_mlstm_br, w_out, g_ffn2, w1_ffn2, w3_ffn2, w2_ffn2):
    d_model = w_in.shape[1]
    attn_w = N_Q_HEADS * HEAD_DIM
    kv_w = N_KV_HEADS * HEAD_DIM
    splits = (attn_w, kv_w, kv_w, M_WIDTH, M_WIDTH, M_WIDTH, M_WIDTH, 2 * M_HEADS, 2 * M_HEADS, d_model, d_model)
    offs = np.cumsum((0,) + splits)
    assert offs[-1] == w_in.shape[2]
    cols = [w_in[0, :, offs[i]:offs[i + 1]] for i in range(len(splits))]
    aq, ak, av, mq, mk, mv, mo, gi, gf, ga, gm = cols
    gate_w = jnp.pad(jnp.concatenate([gi, gf], axis=1), ((0, 0), (0, LANES - 4 * M_HEADS)))
    gate_b = jnp.pad(jnp.concatenate([b_i[0], b_f[0]]), (0, LANES - 4 * M_HEADS))[None]
    row = lambda g: g.astype(F32)[None]
    both_halves = lambda g: jnp.tile(g[0].astype(F32), LANES // HEAD_DIM)[None]
    return {
        "g1": row(g_ffn1[0]), "w1a": w1_ffn1[0].astype(BF16), "w3a": w3_ffn1[0].astype(BF16),
        "w2a": w2_ffn1[0].astype(BF16),
        "gmix": row(g_mix[0]),
        "wq": _pair_heads(aq).astype(BF16), "wkv": jnp.concatenate([ak, av], axis=1).astype(BF16),
        "wm": jnp.concatenate([mq, mk, mv, mo], axis=1).astype(BF16),
        "wg": gate_w.astype(BF16), "gbias": gate_b.astype(F32),
        "wmerge": jnp.concatenate([ga, gm], axis=1).astype(BF16),
        "qgain": both_halves(q_gain), "kgain": both_halves(k_gain),
        "mgain": m_gain[0].astype(F32).reshape(1, M_WIDTH),
        "wab": _pair_heads(w_attn_br[0].T).T.astype(BF16), "wmb": w_mlstm_br[0].astype(BF16), "wout": w_out[0].astype(BF16),
        "g2": row(g_ffn2[0]), "w1b": w1_ffn2[0].astype(BF16), "w3b": w3_ffn2[0].astype(BF16),
        "w2b": w2_ffn2[0].astype(BF16),
    }


def _rope_tables(n_tok):
    t = jnp.arange(n_tok)
    pos = jnp.stack([(t // GRID_W).astype(F32), (t % GRID_W).astype(F32)], axis=-1)
    inv_freq = ROPE_THETA ** (-2.0 * jnp.arange(ROPE_FREQS, dtype=F32) / ROPE_AXIS_DIM)
    ang = pos[:, :, None] * inv_freq
    cos, sin = jnp.cos(ang), jnp.sin(ang)
    cos = jnp.stack([cos, cos], axis=2).reshape(n_tok, HEAD_DIM)
    sin = jnp.stack([-sin, sin], axis=2).reshape(n_tok, HEAD_DIM)
    reps = (1, LANES // HEAD_DIM)
    return jnp.tile(cos, reps), jnp.tile(sin, reps)


def _trunk(x, meta_parts, wts):
    batch, n_tok, d_model = x.shape
    k_meta, v_meta, mt_meta, m_meta, gates_meta, norms_meta = meta_parts
    cos, sin = _rope_tables(n_tok)
    h, q, k, v, mt, m, gates, merge, norms = _front_call(
        x.reshape(batch * n_tok, d_model), cos, sin, wts, FRONT_ROWS)
    q_rows, k_rows, chunk = min(ATTN_Q_ROWS, n_tok), min(ATTN_K_ROWS, n_tok), min(MLSTM_STEP_TOKENS, n_tok)
    assert n_tok % FRONT_ROWS == 0 and n_tok % q_rows == 0 and n_tok % k_rows == 0 and n_tok % chunk == 0
    assert q_rows % FRONT_ROWS == 0 and k_rows % FRONT_ROWS == 0
    safe, floor = _attn_safe_blocks(norms, norms_meta, batch, n_tok // q_rows, n_tok // k_rows)
    attn_o = _attn_call(q, k, v, k_meta, v_meta, safe, floor, batch, q_rows, k_rows)
    hf, hb = _mlstm_call(mt, m, gates, mt_meta, m_meta, gates_meta, batch, chunk)
    y = _back_call(h, attn_o, hf, hb, m, merge, wts, BACK_ROWS)
    return y.reshape(batch, n_tok, d_model)


def kernel(x_prompt, x_sample, meta, g_ffn1, w1_ffn1, w3_ffn1, w2_ffn1, g_mix, w_in, b_i, b_f, q_gain, k_gain,
           m_gain, w_attn_br, w_mlstm_br, w_out, g_ffn2, w1_ffn2, w3_ffn2, w2_ffn2):
    assert w_in.shape[0] == 1, "single-layer trunk: the meta rows' mixer outputs are never consumed"
    assert meta.shape[0] == N_META
    wts = _prepare_weights(g_ffn1, w1_ffn1, w3_ffn1, w2_ffn1, g_mix, w_in, b_i, b_f, q_gain, k_gain, m_gain,
                           w_attn_br, w_mlstm_br, w_out, g_ffn2, w1_ffn2, w3_ffn2, w2_ffn2)
    ones = jnp.ones((META_ROWS, LANES), F32)
    meta_rows = jnp.pad(meta.astype(F32), ((0, META_ROWS - N_META), (0, 0)))
    _, _, k_meta, v_meta, mt_meta, m_meta, gates_meta, _, norms_meta = _front_call(
        meta_rows, ones, jnp.zeros_like(ones), wts, META_ROWS)
    meta_parts = (k_meta, v_meta, mt_meta, m_meta, gates_meta, norms_meta)
    return (_trunk(x_prompt, meta_parts, wts), _trunk(x_sample, meta_parts, wts))
```

```python
import jax
import jax.numpy as jnp
import numpy as np
from jax import lax
from jax.experimental import pallas as pl
from jax.experimental.pallas import tpu as pltpu

F32 = jnp.float32
BF16 = jnp.bfloat16

N_META = 16
GRID_W = 64
EPS = 1e-6
NEG = -1e30
N_Q_HEADS = 8
N_KV_HEADS = 2
Q_PER_KV = N_Q_HEADS // N_KV_HEADS
HEAD_DIM = 64
ROPE_AXIS_DIM = HEAD_DIM // 2
ROPE_FREQS = ROPE_AXIS_DIM // 2
ROPE_THETA = 10000.0
M_HEADS = 4
M_HEAD_DIM = 128
M_WIDTH = M_HEADS * M_HEAD_DIM

LANES = 128
BF16_ROWS = 16
VMEM_LIMIT_BYTES = 60 * 1024 * 1024

SUB_ROWS = 256
FRONT_ROWS = 2 * SUB_ROWS
BACK_ROWS = 2 * SUB_ROWS
ATTN_Q_ROWS = 2048
ATTN_K_ROWS = 2048
MLSTM_CHUNK = 256
MLSTM_STEP_TOKENS = 4 * MLSTM_CHUNK

Q_PAD = N_Q_HEADS * LANES
ATTN_WIDTH = N_Q_HEADS * HEAD_DIM
KV_WIDTH = N_KV_HEADS * HEAD_DIM
assert KV_WIDTH == LANES and HEAD_DIM * 2 == LANES
Q_SCALE = HEAD_DIM ** -0.5 * 1.4426950408889634
ATTN_SAFE_LOG2 = 90.0
ATTN_V_ROWS = HEAD_DIM + BF16_ROWS
META_ROWS = LANES
STATE_ROWS = M_HEAD_DIM + BF16_ROWS


def _const_spec(shape):
    zeros = (0,) * len(shape)
    return pl.BlockSpec(shape, lambda *_: zeros, pipeline_mode=pl.Buffered(1))


def _rmsnorm(x, g):
    return x * lax.rsqrt(jnp.mean(x * x, axis=-1, keepdims=True) + EPS) * g


def _swiglu_half_residual(x, g, w1_ref, w3_ref, w2_ref):
    u = _rmsnorm(x, g).astype(BF16)
    a = jnp.dot(u, w1_ref[...], preferred_element_type=F32)
    b = jnp.dot(u, w3_ref[...], preferred_element_type=F32)
    act = (a * jax.nn.sigmoid(a) * b).astype(BF16)
    return x + 0.5 * jnp.dot(act, w2_ref[...], preferred_element_type=F32)


def _log_sigmoid(x):
    return jnp.minimum(x, 0.0) - jnp.log1p(jnp.exp(-jnp.abs(x)))


def _half_sums(x, low_half):
    low = jnp.sum(jnp.where(low_half, x, 0.0), axis=-1, keepdims=True)
    high = jnp.sum(jnp.where(low_half, 0.0, x), axis=-1, keepdims=True)
    return low, high


def _head_pair_norm_rope(x, gain, cos, sin_signed, low_half, first_half):
    ms_low, ms_high = _half_sums(x * x, low_half)
    inv = jnp.where(low_half, lax.rsqrt(ms_low * (1.0 / HEAD_DIM) + EPS), lax.rsqrt(ms_high * (1.0 / HEAD_DIM) + EPS))
    y = x * inv * gain
    partner = jnp.where(first_half, pltpu.roll(y, LANES - ROPE_FREQS, 1), pltpu.roll(y, ROPE_FREQS, 1))
    return y * cos + partner * sin_signed


def _front_kernel(x_ref, cos_ref, sin_ref, g1_ref, w1_ref, w3_ref, w2_ref, gmix_ref, wq_ref, wkv_ref,
                  wm_ref, wg_ref, wmerge_ref, qgain_ref, kgain_ref, gbias_ref,
                  h_ref, q_ref, k_ref, v_ref, mt_ref, m_ref, gate_ref, merge_ref, norm_ref):
    sub = min(x_ref.shape[0], SUB_ROWS)
    lane = lax.broadcasted_iota(jnp.int32, (sub, LANES), 1)
    low_half = lane < HEAD_DIM
    first_half = (lane % ROPE_AXIS_DIM) < ROPE_FREQS

    def sq_norms(y_bf16):
        y = y_bf16.astype(F32)
        return jnp.maximum(*_half_sums(y * y, low_half))

    q_sq = jnp.zeros((sub, 1), F32)
    k_sq = jnp.zeros((sub, 1), F32)
    for r in range(0, x_ref.shape[0], sub):
        rs = slice(r, r + sub)
        h = _swiglu_half_residual(x_ref[rs, :], g1_ref[...], w1_ref, w3_ref, w2_ref)
        h_ref[rs, :] = h
        u = _rmsnorm(h, gmix_ref[...]).astype(BF16)
        cos = cos_ref[rs, :]
        sin = sin_ref[rs, :]

        zq = jnp.dot(u, wq_ref[...], preferred_element_type=F32)
        for pair in range(Q_PER_KV):
            y = _head_pair_norm_rope(zq[:, pair * LANES:(pair + 1) * LANES], qgain_ref[...], cos, sin,
                                     low_half, first_half) * Q_SCALE
            q_sq = jnp.maximum(q_sq, sq_norms(y.astype(BF16)))
            for group, keep in enumerate((low_half, ~low_half)):
                hd = group * Q_PER_KV + pair
                q_ref[hd * LANES:(hd + 1) * LANES, rs] = jnp.where(keep, y, 0.0).T.astype(BF16)

        zkv = jnp.dot(u, wkv_ref[...], preferred_element_type=F32)
        k = _head_pair_norm_rope(zkv[:, :LANES], kgain_ref[...], cos, sin, low_half, first_half).astype(BF16)
        k_ref[rs, :] = k
        k_sq = jnp.maximum(k_sq, sq_norms(k))
        v_ref[:, rs] = zkv[:, LANES:].T.astype(BF16)

        zm = jnp.dot(u, wm_ref[...], preferred_element_type=F32)
        mt_ref[:M_WIDTH, rs] = zm[:, :M_WIDTH].T.astype(BF16)
        mt_ref[M_WIDTH:, rs] = zm[:, 2 * M_WIDTH:3 * M_WIDTH].T.astype(BF16)
        m_ref[rs, :M_WIDTH] = (zm[:, M_WIDTH:2 * M_WIDTH] * (M_HEAD_DIM ** -0.5)).astype(BF16)
        m_ref[rs, M_WIDTH:] = zm[:, 3 * M_WIDTH:].astype(BF16)

        zg = jnp.dot(u, wg_ref[...], preferred_element_type=F32) + gbias_ref[...]
        gate_ref[rs, :] = jnp.where(lane < 2 * M_HEADS, zg, _log_sigmoid(zg))

        zmerge = jnp.dot(u, wmerge_ref[...], preferred_element_type=F32)
        merge_ref[rs, :] = jax.nn.sigmoid(zmerge).astype(BF16)

    tile_row = lax.broadcasted_iota(jnp.int32, norm_ref.shape, 0)
    norm_ref[...] = jnp.where(tile_row == 0, jnp.max(q_sq, axis=0, keepdims=True),
                              jnp.max(k_sq, axis=0, keepdims=True))


def _front_call(x, cos, sin, wts, rows):
    tokens, d_model = x.shape
    steps = tokens // rows
    table_steps = cos.shape[0] // rows

    def row_spec(width):
        return pl.BlockSpec((rows, width), lambda i: (i, 0))

    table_spec = pl.BlockSpec((rows, LANES), lambda i: (i % table_steps, 0))
    consts = (wts["g1"], wts["w1a"], wts["w3a"], wts["w2a"], wts["gmix"], wts["wq"], wts["wkv"],
              wts["wm"], wts["wg"], wts["wmerge"], wts["qgain"], wts["kgain"], wts["gbias"])

    def col_spec(height):
        return pl.BlockSpec((height, rows), lambda i: (0, i))

    out_specs = [row_spec(d_model), col_spec(Q_PAD), row_spec(KV_WIDTH), col_spec(KV_WIDTH), col_spec(2 * M_WIDTH),
                 row_spec(2 * M_WIDTH), row_spec(LANES), row_spec(2 * d_model),
                 pl.BlockSpec((None, 8, LANES), lambda i: (i, 0, 0))]
    out_shape = [jax.ShapeDtypeStruct((tokens, d_model), F32), jax.ShapeDtypeStruct((Q_PAD, tokens), BF16),
                 jax.ShapeDtypeStruct((tokens, KV_WIDTH), BF16), jax.ShapeDtypeStruct((KV_WIDTH, tokens), BF16),
                 jax.ShapeDtypeStruct((2 * M_WIDTH, tokens), BF16),
                 jax.ShapeDtypeStruct((tokens, 2 * M_WIDTH), BF16), jax.ShapeDtypeStruct((tokens, LANES), F32),
                 jax.ShapeDtypeStruct((tokens, 2 * d_model), BF16), jax.ShapeDtypeStruct((steps, 8, LANES), F32)]
    return pl.pallas_call(
        _front_kernel,
        grid=(steps,),
        in_specs=[row_spec(d_model), table_spec, table_spec] + [_const_spec(c.shape) for c in consts],
        out_specs=out_specs,
        out_shape=out_shape,
        compiler_params=pltpu.CompilerParams(dimension_semantics=("parallel",),
                                             vmem_limit_bytes=VMEM_LIMIT_BYTES),
        name="front",
    )(x, cos, sin, *consts)


def _attn_kernel(safe_ref, floor_ref, qt_ref, k_ref, vt_ref, kmeta_ref, vtmeta_ref, o_ref, m_scr, acc_scr):
    ki = pl.program_id(2)
    q_cols = qt_ref.shape[1]
    step = (pl.program_id(0) * pl.num_programs(1) + pl.program_id(1)) * pl.num_programs(2) + ki
    safe = safe_ref[step]

    def scores_t(hd, keys_ref):
        return jnp.dot(keys_ref[...], qt_ref[hd * LANES:(hd + 1) * LANES, :], preferred_element_type=F32)

    def values_t(hd, values_ref):
        g = hd // Q_PER_KV
        row = lax.broadcasted_iota(jnp.int32, (ATTN_V_ROWS - HEAD_DIM, values_ref.shape[1]), 0)
        ones = jnp.where(row == 0, 1.0, 0.0).astype(BF16)
        return jnp.concatenate([values_ref[g * HEAD_DIM:(g + 1) * HEAD_DIM, :], ones], axis=0)

    def meta_scores_t(hd):
        real = lax.broadcasted_iota(jnp.int32, (kmeta_ref.shape[0], q_cols), 0) < N_META
        return jnp.where(real, scores_t(hd, kmeta_ref), NEG)

    @pl.when((ki == 0) & (safe == 1))
    def _seed_from_meta_keys_against_floor():
        floor = jnp.full((1, q_cols), floor_ref[pl.program_id(0) * pl.num_programs(1) + pl.program_id(1)], F32)
        scores = [jnp.dot(kmeta_ref[:N_META, :], qt_ref[hd * LANES:(hd + 1) * LANES, :],
                          preferred_element_type=F32) for hd in range(N_Q_HEADS)]
        for hd, s in enumerate(scores):
            p = jnp.exp2(s - floor).astype(BF16)
            acc = jnp.dot(values_t(hd, vtmeta_ref)[:, :N_META], p, preferred_element_type=F32)
            m = jnp.maximum(floor, jnp.max(s, axis=0, keepdims=True))
            acc_scr[hd] = jnp.exp2(floor - m) * acc
            m_scr[hd] = jnp.broadcast_to(m, m_scr.shape[1:])

    @pl.when((ki == 0) & (safe != 1))
    def _seed_from_meta_keys():
        for hd in range(N_Q_HEADS):
            s = meta_scores_t(hd)
            m = jnp.max(s, axis=0, keepdims=True)
            p = jnp.exp2(s - m).astype(BF16)
            acc_scr[hd] = jnp.dot(values_t(hd, vtmeta_ref), p, preferred_element_type=F32)
            m_scr[hd] = jnp.broadcast_to(m, m_scr.shape[1:])

    @pl.when(safe == 1)
    def _block_against_previous_maximum():
        s_next = scores_t(0, k_ref)
        for hd in range(N_Q_HEADS):
            m_old = m_scr[hd][:1]
            s = s_next
            if hd + 1 < N_Q_HEADS:
                s_next = scores_t(hd + 1, k_ref)
            p = jnp.exp2(s - m_old).astype(BF16)
            pv = jnp.dot(values_t(hd, vt_ref), p, preferred_element_type=F32)
            m_new = jnp.maximum(m_old, jnp.max(s, axis=0, keepdims=True))
            acc_scr[hd] = jnp.exp2(m_old - m_new) * (acc_scr[hd] + pv)
            m_scr[hd] = jnp.broadcast_to(m_new, m_scr.shape[1:])

    @pl.when(safe != 1)
    def _block_against_own_maximum():
        for hd in range(N_Q_HEADS):
            m_old = m_scr[hd][:1]
            s = scores_t(hd, k_ref)
            m_new = jnp.maximum(m_old, jnp.max(s, axis=0, keepdims=True))
            p = jnp.exp2(s - m_new).astype(BF16)
            pv = jnp.dot(values_t(hd, vt_ref), p, preferred_element_type=F32)
            acc_scr[hd] = jnp.exp2(m_old - m_new) * acc_scr[hd] + pv
            m_scr[hd] = jnp.broadcast_to(m_new, m_scr.shape[1:])

    @pl.when(ki == pl.num_programs(2) - 1)
    def _normalise():
        def normalised(hd):
            acc = acc_scr[hd]
            return acc[:HEAD_DIM] / acc[HEAD_DIM:HEAD_DIM + 1]

        for pair in range(Q_PER_KV):
            both_t = jnp.concatenate([normalised(pair), normalised(Q_PER_KV + pair)], axis=0)
            o_ref[:, pair * LANES:(pair + 1) * LANES] = both_t.T.astype(BF16)


def _attn_safe_blocks(norms, norms_meta, batch, nq, nk):
    def block_max(sq, blocks):
        return jnp.sqrt(jnp.max(sq.reshape(blocks, -1), axis=1))

    q_norm = block_max(norms[:, 0, 0], batch * nq).reshape(batch, nq, 1)
    k_norm = block_max(norms[:, 1, 0], batch * nk).reshape(batch, 1, nk)
    k_meta_norm = jnp.sqrt(jnp.max(norms_meta[:, 1, 0]))
    gap = q_norm * (jnp.maximum(k_norm, k_meta_norm) + k_meta_norm)
    safe = (gap <= ATTN_SAFE_LOG2).astype(jnp.int32).reshape(-1)
    floor = (-q_norm * k_meta_norm).reshape(-1)
    return safe, floor


def _attn_call(qt, k, vt, k_meta, vt_meta, safe, floor, batch, q_rows, k_rows):
    tokens = k.shape[0]
    n_tok = tokens // batch
    nq = n_tok // q_rows
    nk = n_tok // k_rows
    grid_spec = pltpu.PrefetchScalarGridSpec(
        num_scalar_prefetch=2,
        grid=(batch, nq, nk),
        in_specs=[
            pl.BlockSpec((Q_PAD, q_rows), lambda b, i, j, *_: (0, b * nq + i)),
            pl.BlockSpec((k_rows, KV_WIDTH), lambda b, i, j, *_: (b * nk + j, 0)),
            pl.BlockSpec((KV_WIDTH, k_rows), lambda b, i, j, *_: (0, b * nk + j)),
            pl.BlockSpec(k_meta.shape, lambda b, i, j, *_: (0, 0)),
            pl.BlockSpec(vt_meta.shape, lambda b, i, j, *_: (0, 0)),
        ],
        out_specs=pl.BlockSpec((q_rows, ATTN_WIDTH), lambda b, i, j, *_: (b * nq + i, 0)),
        scratch_shapes=[pltpu.VMEM((N_Q_HEADS, 8, q_rows), F32),
                        pltpu.VMEM((N_Q_HEADS, ATTN_V_ROWS, q_rows), F32)],
    )
    return pl.pallas_call(
        _attn_kernel,
        grid_spec=grid_spec,
        out_shape=jax.ShapeDtypeStruct((tokens, ATTN_WIDTH), BF16),
        compiler_params=pltpu.CompilerParams(dimension_semantics=("parallel", "parallel", "arbitrary"),
                                             vmem_limit_bytes=VMEM_LIMIT_BYTES),
        name="attn",
    )(safe, floor, qt, k, vt, k_meta, vt_meta)


def _split_hi_lo(x):
    hi = x.astype(BF16)
    lo = (x - hi.astype(F32)).astype(BF16)
    return hi, lo


def _gate_cumsum(gates, causal):
    t = gates.shape[0]
    row = lax.broadcasted_iota(jnp.int32, (t, t), 0)
    col = lax.broadcasted_iota(jnp.int32, (t, t), 1)
    mask = (col <= row) if causal else (col >= row)
    tri = jnp.where(mask, 1.0, 0.0).astype(BF16)
    hi, lo = _split_hi_lo(gates)
    sums = jnp.dot(tri, hi, preferred_element_type=F32) + jnp.dot(tri, lo, preferred_element_type=F32)
    return sums, mask


def _values_t_with_ones(vt):
    row = lax.broadcasted_iota(jnp.int32, (STATE_ROWS - M_HEAD_DIM, vt.shape[1]), 0)
    return jnp.concatenate([vt, jnp.where(row == 0, 1.0, 0.0).astype(vt.dtype)], axis=0)


def _state_update(state_t, m_in, k, vaug_t, li_row, b_row, b_tot):
    w_end = b_tot - b_row + li_row
    m_new = jnp.maximum(b_tot + m_in, jnp.max(w_end, axis=1, keepdims=True))
    vw = (vaug_t.astype(F32) * jnp.exp(w_end - m_new)).astype(BF16)
    return jnp.exp(b_tot + m_in - m_new) * state_t + jnp.dot(vw, k, preferred_element_type=F32), m_new


def _mlstm_kernel(qtf_ref, vtf_ref, kf_ref, gf_ref, qtb_ref, vtb_ref, kb_ref, gb_ref, kmeta_ref, vtmeta_ref,
                  gmeta_ref, hf_ref, hb_ref, state_scr, m_scr):
    c = pl.program_id(1)

    @pl.when(c == 0)
    def _reset_and_absorb_meta():
        state_scr[...] = jnp.zeros_like(state_scr)
        m_scr[...] = jnp.zeros_like(m_scr)
        gates = gmeta_ref[...]
        row = lax.broadcasted_iota(jnp.int32, gates.shape, 0)
        lane = lax.broadcasted_iota(jnp.int32, gates.shape, 1)
        gates = jnp.where(row < N_META, gates, jnp.where(lane < 2 * M_HEADS, NEG, 0.0))
        sums, _ = _gate_cumsum(gates, causal=True)
        gates_t = gates.T
        sums_t = sums.T
        last = gates.shape[0] - 1
        for hd in range(M_HEADS):
            f_lane = 2 * M_HEADS + hd
            sl = slice(hd * M_HEAD_DIM, (hd + 1) * M_HEAD_DIM)
            vt = vtmeta_ref[M_WIDTH + hd * M_HEAD_DIM:M_WIDTH + (hd + 1) * M_HEAD_DIM, :]
            new_state, m_new = _state_update(
                state_scr[hd], m_scr[hd][:1, :1], kmeta_ref[:, sl], _values_t_with_ones(vt),
                gates_t[hd:hd + 1, :], sums_t[f_lane:f_lane + 1, :], sums[last:last + 1, f_lane:f_lane + 1])
            state_scr[hd] = new_state
            m_scr[hd] = jnp.broadcast_to(m_new, m_scr.shape[1:])

    t = MLSTM_CHUNK if gf_ref.shape[0] % MLSTM_CHUNK == 0 else gf_ref.shape[0]
    n_sub = gf_ref.shape[0] // t
    src = lax.broadcasted_iota(jnp.int32, (t, t), 0)
    tgt = lax.broadcasted_iota(jnp.int32, (t, t), 1)
    links = []
    m_carry = {chain: m_scr[chain][:1, :1] for chain in range(2 * M_HEADS)}
    for order in range(n_sub):
        for causal, qt_ref, vt_ref, k_ref, g_ref, o_ref in ((True, qtf_ref, vtf_ref, kf_ref, gf_ref, hf_ref),
                                                            (False, qtb_ref, vtb_ref, kb_ref, gb_ref, hb_ref)):
            j = order if causal else n_sub - 1 - order
            ts = slice(j * t, (j + 1) * t)
            gates = g_ref[ts, :]
            sums, _ = _gate_cumsum(gates, causal)
            gates_t = gates.T
            sums_t = sums.T
            feeds = (src <= tgt) if causal else (src >= tgt)
            last = t - 1 if causal else 0
            for hd in range(M_HEADS):
                chain = hd if causal else M_HEADS + hd
                i_lane, f_lane = chain, 2 * M_HEADS + chain
                li_row, b_row = gates_t[i_lane:i_lane + 1, :], sums_t[f_lane:f_lane + 1, :]
                b_tot = sums[last:last + 1, f_lane:f_lane + 1]
                w_end = b_tot - b_row + li_row
                m_in = m_carry[chain]
                m_new = jnp.maximum(b_tot + m_in, jnp.max(w_end, axis=1, keepdims=True))
                m_carry[chain] = m_new
                links.append(dict(
                    chain=chain, first=order == 0, feeds=feeds, o_ref=o_ref, ts=ts,
                    sl=slice(hd * M_HEAD_DIM, (hd + 1) * M_HEAD_DIM), qt_ref=qt_ref, vt_ref=vt_ref, k_ref=k_ref,
                    r_col=gates[:, i_lane:i_lane + 1] - sums[:, f_lane:f_lane + 1],
                    b_row=b_row, b_tot=b_tot, w_end=w_end, m_in=m_in, m_new=m_new))

    for ln in links:
        k = ln["k_ref"][ln["ts"], ln["sl"]]
        qt = ln["qt_ref"][ln["sl"], ln["ts"]]
        if ln["first"]:
            both = jnp.dot(jnp.concatenate([k, state_scr[ln["chain"]].astype(BF16)], axis=0), qt,
                           preferred_element_type=F32)
            ln["kq"], ln["inter"] = both[:t], both[t:]
        else:
            ln["kq"] = jnp.dot(k, qt, preferred_element_type=F32)
    for ln in links:
        ln["r"] = jnp.where(ln["feeds"], ln["r_col"], NEG)
        ln["g"] = jnp.maximum(jnp.max(ln["r"], axis=0, keepdims=True), ln["m_in"])
    for ln in links:
        ln["s"] = (ln["kq"] * jnp.exp(ln["r"] - ln["g"])).astype(BF16)
        ln["vaug"] = _values_t_with_ones(ln["vt_ref"][ln["sl"], ln["ts"]])
        ln["vw"] = (ln["vaug"].astype(F32) * jnp.exp(ln["w_end"] - ln["m_new"])).astype(BF16)
    for ln in links:
        ln["tot"] = jnp.dot(ln["vaug"], ln["s"], preferred_element_type=F32)
        ln["local"] = jnp.dot(ln["vw"], ln["k_ref"][ln["ts"], ln["sl"]], preferred_element_type=F32)
    state = {chain: state_scr[chain] for chain in range(2 * M_HEADS)}
    for ln in links:
        chain = ln["chain"]
        if not ln["first"]:
            ln["inter"] = jnp.dot(state[chain].astype(BF16), ln["qt_ref"][ln["sl"], ln["ts"]],
                                  preferred_element_type=F32)
        tot = ln["tot"] + jnp.exp(ln["m_in"] - ln["g"]) * ln["inter"]
        den = jnp.maximum(jnp.abs(tot[M_HEAD_DIM:M_HEAD_DIM + 1]), jnp.exp(-(ln["b_row"] + ln["g"])))
        ln["o_ref"][ln["sl"], ln["ts"]] = tot[:M_HEAD_DIM] / den
        state[chain] = jnp.exp(ln["b_tot"] + ln["m_in"] - ln["m_new"]) * state[chain] + ln["local"]
    for chain in range(2 * M_HEADS):
        state_scr[chain] = state[chain]
        m_scr[chain] = jnp.broadcast_to(m_carry[chain], m_scr.shape[1:])


def _mlstm_call(mt, m, gates, mt_meta, m_meta, gates_meta, batch, chunk):
    tokens = m.shape[0]
    nc = tokens // batch // chunk

    def chunk_of(direction):
        return (lambda b, c: b * nc + c) if direction == "fwd" else (lambda b, c: b * nc + nc - 1 - c)

    def specs(direction):
        at = chunk_of(direction)
        return [pl.BlockSpec((M_WIDTH, chunk), lambda b, c: (0, at(b, c))),
                pl.BlockSpec((M_WIDTH, chunk), lambda b, c: (1, at(b, c))),
                pl.BlockSpec((chunk, M_WIDTH), lambda b, c: (at(b, c), 0)),
                pl.BlockSpec((chunk, LANES), lambda b, c: (at(b, c), 0))]

    def out_spec(direction):
        at = chunk_of(direction)
        return pl.BlockSpec((M_WIDTH, chunk), lambda b, c: (0, at(b, c)))

    return pl.pallas_call(
        _mlstm_kernel,
        grid=(batch, nc),
        in_specs=specs("fwd") + specs("bwd") + [pl.BlockSpec(m_meta.shape, lambda b, c: (0, 0)),
                                                pl.BlockSpec(mt_meta.shape, lambda b, c: (0, 0)),
                                                pl.BlockSpec(gates_meta.shape, lambda b, c: (0, 0))],
        out_specs=[out_spec("fwd"), out_spec("bwd")],
        out_shape=[jax.ShapeDtypeStruct((M_WIDTH, tokens), F32)] * 2,
        scratch_shapes=[pltpu.VMEM((2 * M_HEADS, STATE_ROWS, M_HEAD_DIM), F32),
                        pltpu.VMEM((2 * M_HEADS, 8, LANES), F32)],
        compiler_params=pltpu.CompilerParams(dimension_semantics=("parallel", "arbitrary"),
                                             vmem_limit_bytes=VMEM_LIMIT_BYTES),
        name="mlstm",
    )(mt, mt, m, gates, mt, mt, m, gates, m_meta, mt_meta, gates_meta)


def _back_kernel(h_ref, ao_ref, hf_ref, hb_ref, mo_ref, merge_ref, mgain_ref, wab_ref, wmb_ref, wout_ref,
                 g2_ref, w1_ref, w3_ref, w2_ref, y_ref):
    rows, d_model = h_ref.shape
    for r in range(0, rows, SUB_ROWS):
        rs = slice(r, r + SUB_ROWS)
        a_out = jnp.dot(ao_ref[rs, :], wab_ref[...], preferred_element_type=F32)

        hm = (hf_ref[:, rs] + hb_ref[:, rs]).T
        o_gate = jax.nn.sigmoid(mo_ref[rs, :].astype(F32))
        gated = []
        for hd in range(M_HEADS):
            sl = slice(hd * M_HEAD_DIM, (hd + 1) * M_HEAD_DIM)
            gated.append((_rmsnorm(hm[:, sl], mgain_ref[:, sl]) * o_gate[:, sl]).astype(BF16))
        m_out = jnp.dot(jnp.concatenate(gated, axis=1), wmb_ref[...], preferred_element_type=F32)

        merged = (merge_ref[rs, :d_model].astype(F32) * a_out + merge_ref[rs, d_model:].astype(F32) * m_out)
        h2 = h_ref[rs, :] + jnp.dot(merged.astype(BF16), wout_ref[...], preferred_element_type=F32)
        y_ref[rs, :] = _swiglu_half_residual(h2, g2_ref[...], w1_ref, w3_ref, w2_ref)


def _back_call(h, attn_o, hf, hb, m, merge, wts, rows):
    tokens, d_model = h.shape

    def row_spec(width, col=0):
        return pl.BlockSpec((rows, width), lambda i: (i, col))

    col_spec = pl.BlockSpec((M_WIDTH, rows), lambda i: (0, i))
    consts = (wts["mgain"], wts["wab"], wts["wmb"], wts["wout"], wts["g2"], wts["w1b"], wts["w3b"], wts["w2b"])
    return pl.pallas_call(
        _back_kernel,
        grid=(tokens // rows,),
        in_specs=[row_spec(d_model), row_spec(ATTN_WIDTH), col_spec, col_spec,
                  row_spec(M_WIDTH, 1), row_spec(2 * d_model)] + [_const_spec(c.shape) for c in consts],
        out_specs=row_spec(d_model),
        out_shape=jax.ShapeDtypeStruct((tokens, d_model), F32),
        compiler_params=pltpu.CompilerParams(dimension_semantics=("parallel",),
                                             vmem_limit_bytes=VMEM_LIMIT_BYTES),
        name="back",
    )(h, attn_o, hf, hb, m, merge, *consts)


def _pair_heads(w):
    lead = w.shape[:-1]
    w = w.reshape(*lead, N_KV_HEADS, Q_PER_KV, HEAD_DIM)
    return jnp.swapaxes(w, -3, -2).reshape(*lead, ATTN_WIDTH)


def _prepare_weights(g_ffn1, w1_ffn1, w3_ffn1, w2_ffn1, g_mix, w_in, b_i, b_f, q_gain, k_gain, m_gain,
                     w_attn_br, w_mlstm_br, w_out, g_ffn2, w1_ffn2, w3_ffn2, w2_ffn2):
    d_model = w_in.shape[1]
    attn_w = N_Q_HEADS * HEAD_DIM
    kv_w = N_KV_HEADS * HEAD_DIM
    splits = (attn_w, kv_w, kv_w, M_WIDTH, M_WIDTH, M_WIDTH, M_WIDTH, 2 * M_HEADS, 2 * M_HEADS, d_model, d_model)
    offs = np.cumsum((0,) + splits)
    assert offs[-1] == w_in.shape[2]
    cols = [w_in[0, :, offs[i]:offs[i + 1]] for i in range(len(splits))]
    aq, ak, av, mq, mk, mv, mo, gi, gf, ga, gm = cols
    gate_w = jnp.pad(jnp.concatenate([gi, gf], axis=1), ((0, 0), (0, LANES - 4 * M_HEADS)))
    gate_b = jnp.pad(jnp.concatenate([b_i[0], b_f[0]]), (0, LANES - 4 * M_HEADS))[None]
    row = lambda g: g.astype(F32)[None]
    both_halves = lambda g: jnp.tile(g[0].astype(F32), LANES // HEAD_DIM)[None]
    return {
        "g1": row(g_ffn1[0]), "w1a": w1_ffn1[0].astype(BF16), "w3a": w3_ffn1[0].astype(BF16),
        "w2a": w2_ffn1[0].astype(BF16),
        "gmix": row(g_mix[0]),
        "wq": _pair_heads(aq).astype(BF16), "wkv": jnp.concatenate([ak, av], axis=1).astype(BF16),
        "wm": jnp.concatenate([mq, mk, mv, mo], axis=1).astype(BF16),
        "wg": gate_w.astype(BF16), "gbias": gate_b.astype(F32),
        "wmerge": jnp.concatenate([ga, gm], axis=1).astype(BF16),
        "qgain": both_halves(q_gain), "kgain": both_halves(k_gain),
        "mgain": m_gain[0].astype(F32).reshape(1, M_WIDTH),
        "wab": _pair_heads(w_attn_br[0].T).T.astype(BF16), "wmb": w_mlstm_br[0].astype(BF16), "wout": w_out[0].astype(BF16),
        "g2": row(g_ffn2[0]), "w1b": w1_ffn2[0].astype(BF16), "w3b": w3_ffn2[0].astype(BF16),
        "w2b": w2_ffn2[0].astype(BF16),
    }


def _rope_tables(n_tok):
    t = jnp.arange(n_tok)
    pos = jnp.stack([(t // GRID_W).astype(F32), (t % GRID_W).astype(F32)], axis=-1)
    inv_freq = ROPE_THETA ** (-2.0 * jnp.arange(ROPE_FREQS, dtype=F32) / ROPE_AXIS_DIM)
    ang = pos[:, :, None] * inv_freq
    cos, sin = jnp.cos(ang), jnp.sin(ang)
    cos = jnp.stack([cos, cos], axis=2).reshape(n_tok, HEAD_DIM)
    sin = jnp.stack([-sin, sin], axis=2).reshape(n_tok, HEAD_DIM)
    reps = (1, LANES // HEAD_DIM)
    return jnp.tile(cos, reps), jnp.tile(sin, reps)


def _trunk(x, meta_parts, wts):
    batch, n_tok, d_model = x.shape
    k_meta, v_meta, mt_meta, m_meta, gates_meta, norms_meta = meta_parts
    cos, sin = _rope_tables(n_tok)
    h, q, k, v, mt, m, gates, merge, norms = _front_call(
        x.reshape(batch * n_tok, d_model), cos, sin, wts, FRONT_ROWS)
    q_rows, k_rows, chunk = min(ATTN_Q_ROWS, n_tok), min(ATTN_K_ROWS, n_tok), min(MLSTM_STEP_TOKENS, n_tok)
    assert n_tok % FRONT_ROWS == 0 and n_tok % q_rows == 0 and n_tok % k_rows == 0 and n_tok % chunk == 0
    assert q_rows % FRONT_ROWS == 0 and k_rows % FRONT_ROWS == 0
    safe, floor = _attn_safe_blocks(norms, norms_meta, batch, n_tok // q_rows, n_tok // k_rows)
    attn_o = _attn_call(q, k, v, k_meta, v_meta, safe, floor, batch, q_rows, k_rows)
    hf, hb = _mlstm_call(mt, m, gates, mt_meta, m_meta, gates_meta, batch, chunk)
    y = _back_call(h, attn_o, hf, hb, m, merge, wts, BACK_ROWS)
    return y.reshape(batch, n_tok, d_model)


def kernel(x_prompt, x_sample, meta, g_ffn1, w1_ffn1, w3_ffn1, w2_ffn1, g_mix, w_in, b_i, b_f, q_gain, k_gain,
           m_gain, w_attn_br, w_mlstm_br, w_out, g_ffn2, w1_ffn2, w3_ffn2, w2_ffn2):
    assert w_in.shape[0] == 1, "single-layer trunk: the meta rows' mixer outputs are never consumed"
    assert meta.shape[0] == N_META
    wts = _prepare_weights(g_ffn1, w1_ffn1, w3_ffn1, w2_ffn1, g_mix, w_in, b_i, b_f, q_gain, k_gain, m_gain,
                           w_attn_br, w_mlstm_br, w_out, g_ffn2, w1_ffn2, w3_ffn2, w2_ffn2)
    ones = jnp.ones((META_ROWS, LANES), F32)
    meta_rows = jnp.pad(meta.astype(F32), ((0, META_ROWS - N_META), (0, 0)))
    _, _, k_meta, v_meta, mt_meta, m_meta, gates_meta, _, norms_meta = _front_call(
        meta_rows, ones, jnp.zeros_like(ones), wts, META_ROWS)
    meta_parts = (k_meta, v_meta, mt_meta, m_meta, gates_meta, norms_meta)
    return (_trunk(x_prompt, meta_parts, wts), _trunk(x_sample, meta_parts, wts))
```
